```python
import math
import jax, jax.numpy as jnp
from jax import lax
import numpy as np

D_MODEL = 1024
BATCH = 8
SEQ = 2048
DEPTH = 1
DEC_BATCH = 128
DEC_SEQ = 1
PAST_LEN = 16384
PAGE_SIZE = 128

MIX_WIDTH = D_MODEL
M_HEADS = 4
M_DV = MIX_WIDTH // 2 // M_HEADS
M_DK = M_DV // 2
M_CHUNK = 64
A_HEADS = 8
A_HD = MIX_WIDTH // 2 // A_HEADS
A_KV = 2
A_GROUP = A_HEADS // A_KV
WINDOW = 128
ATT_BLOCK = WINDOW
NUM_BUCKETS = 32
MAX_DISTANCE = 128
N_EXPERTS = 32
TOP_K = 4
D_FF = D_MODEL
SWIGLU_LIMIT = 7.0
SWIGLU_ALPHA = 1.702
MOE_BLOCK = 128
EPS = 1e-5
SPLITS = (M_HEADS * M_DK, M_HEADS * M_DK, M_HEADS * M_DV, M_HEADS * M_DV, M_HEADS, M_HEADS,
          A_HEADS * A_HD, A_KV * A_HD, A_KV * A_HD)
PROJ_DIM = sum(SPLITS)

kernel_name = "hymba_mlstm_swa_sink_moe_step"

F32 = jnp.float32


def rms_norm(x, g):
    xf = x.astype(F32)
    y = xf * lax.rsqrt(jnp.mean(xf * xf, axis=-1, keepdims=True) + EPS) * g.astype(F32)
    return y.astype(x.dtype)


def split_proj(xn, w_in, b_if):
    z = xn @ w_in
    offs, acc = [], 0
    for w in SPLITS[:-1]:
        acc += w
        offs.append(acc)
    mq, mk, mv, mo, mi, mf, aq, ak, av = jnp.split(z, offs, axis=-1)
    i_pre = (mi + b_if[:M_HEADS]).astype(F32)
    logf = jax.nn.log_sigmoid((mf + b_if[M_HEADS:]).astype(F32))
    return mq, mk, mv, mo, i_pre, logf, aq, ak, av


def mlstm_chunked(q, k, v, i_pre, logf, C0, n0, m0, chunk):
    B, T, H, _ = q.shape
    nc = T // chunk

    def to_chunks(a):
        a = a.astype(F32)
        return a.reshape((B, nc, chunk) + a.shape[2:]).swapaxes(0, 1)

    xs = tuple(to_chunks(a) for a in (q, k, v, i_pre, logf))
    causal = jnp.tril(jnp.ones((chunk, chunk), bool))

    def step(carry, inp):
        C, n, m = carry
        qc, kc, vc, ic, fc = inp
        b = jnp.cumsum(fc, axis=1)
        a = b + m[:, None, :]
        dmat = b[:, :, None, :] - b[:, None, :, :] + ic[:, None, :, :]
        dmat = jnp.where(causal[None, :, :, None], dmat, -jnp.inf)
        mt = jnp.maximum(a, jnp.max(dmat, axis=2))
        w_intra = jnp.exp(dmat - mt[:, :, None, :])
        w_inter = jnp.exp(a - mt)
        s = jnp.einsum('bthd,bshd->btsh', qc, kc) * w_intra
        num = (w_inter[..., None] * jnp.einsum('bthd,bhde->bthe', qc, C)
               + jnp.einsum('btsh,bshe->bthe', s, vc))
        qn = w_inter * jnp.einsum('bthd,bhd->bth', qc, n) + jnp.sum(s, axis=2)
        den = jnp.maximum(jnp.abs(qn), jnp.exp(-mt))
        h = num / den[..., None]
        bl = b[:, -1]
        m_new = mt[:, -1]
        g_prev = jnp.exp(bl + m - m_new)
        g_s = jnp.exp(bl[:, None] - b + ic - m_new[:, None])
        C_new = g_prev[..., None, None] * C + jnp.einsum('bsh,bshd,bshe->bhde', g_s, kc, vc)
        n_new = g_prev[..., None] * n + jnp.einsum('bsh,bshd->bhd', g_s, kc)
        return (C_new, n_new, m_new), h

    (C, n, m), hs = lax.scan(step, (C0.astype(F32), n0.astype(F32), m0.astype(F32)), xs)
    h = hs.swapaxes(0, 1).reshape(B, T, H, v.shape[-1])
    return h, C, n, m


def mlstm_mix(mq, mk, mv, mo, i_pre, logf, C0, n0, m0, m_gain, chunk):
    B, T, _ = mq.shape
    q = mq.reshape(B, T, M_HEADS, M_DK)
    k = mk.reshape(B, T, M_HEADS, M_DK) * (M_DK ** -0.5)
    v = mv.reshape(B, T, M_HEADS, M_DV)
    h, C, n, m = mlstm_chunked(q, k, v, i_pre, logf, C0, n0, m0, chunk)
    hn = h * lax.rsqrt(jnp.mean(h * h, axis=-1, keepdims=True) + EPS) * m_gain.astype(F32).reshape(M_HEADS, M_DV)
    out = hn * jax.nn.sigmoid(mo.astype(F32).reshape(B, T, M_HEADS, M_DV))
    return out.reshape(B, T, M_HEADS * M_DV).astype(mq.dtype), C, n, m


def t5_bucket(dist):
    n = jnp.maximum(dist, 0)
    max_exact = NUM_BUCKETS // 2
    large = max_exact + (jnp.log(jnp.maximum(n, 1).astype(F32) / max_exact)
                         / math.log(MAX_DISTANCE / max_exact) * (NUM_BUCKETS - max_exact)).astype(jnp.int32)
    large = jnp.minimum(large, NUM_BUCKETS - 1)
    return jnp.where(n < max_exact, n, large)


def sink_window_attention(q, k, v, dist, valid, rel_bias, sinks):
    N, Lq, Lk = dist.shape
    logits = jnp.einsum('bnqhgd,bnshd->bnhgqs', q, k).astype(F32) * (A_HD ** -0.5)
    bias = rel_bias.astype(F32)[t5_bucket(dist)]
    bias = bias.reshape(N, Lq, Lk, A_KV, A_GROUP).transpose(0, 3, 4, 1, 2)
    logits = jnp.where(valid[:, None, None], logits + bias[None], -jnp.inf)
    sink = jnp.broadcast_to(sinks.astype(F32).reshape(A_KV, A_GROUP, 1, 1), logits.shape[:-1] + (1,))
    probs = jax.nn.softmax(jnp.concatenate([logits, sink], axis=-1), axis=-1)[..., :-1]
    return jnp.einsum('bnhgqs,bnshd->bnqhgd', probs.astype(v.dtype), v)


def attn_prompt(aq, ak, av, rel_bias, sinks):
    B, T, _ = aq.shape
    nb = T // ATT_BLOCK
    q = aq.reshape(B, nb, ATT_BLOCK, A_KV, A_GROUP, A_HD)
    k = ak.reshape(B, nb, ATT_BLOCK, A_KV, A_HD)
    v = av.reshape(B, nb, ATT_BLOCK, A_KV, A_HD)
    padw = ((0, 0), (1, 0), (0, 0), (0, 0), (0, 0))
    k2 = jnp.concatenate([jnp.pad(k, padw)[:, :-1], k], axis=2)
    v2 = jnp.concatenate([jnp.pad(v, padw)[:, :-1], v], axis=2)
    blk = jnp.arange(nb, dtype=jnp.int32)[:, None]
    qpos = blk * ATT_BLOCK + jnp.arange(ATT_BLOCK, dtype=jnp.int32)[None]
    kpos = (blk - 1) * ATT_BLOCK + jnp.arange(2 * ATT_BLOCK, dtype=jnp.int32)[None]
    dist = qpos[:, :, None] - kpos[:, None, :]
    valid = (dist >= 0) & (dist <= WINDOW) & (kpos[:, None, :] >= 0)
    o = sink_window_attention(q, k2, v2, dist, valid, rel_bias, sinks)
    new_k = ak.reshape(B, T, A_KV, A_HD)[:, -WINDOW:]
    new_v = av.reshape(B, T, A_KV, A_HD)[:, -WINDOW:]
    return o.reshape(B, T, A_HEADS * A_HD), new_k, new_v


def attn_sample(aq, ak, av, cache_k, cache_v, rel_bias, sinks):
    B, T, _ = aq.shape
    W = cache_k.shape[1]
    q = aq.reshape(B, 1, T, A_KV, A_GROUP, A_HD)
    k_all = jnp.concatenate([cache_k.astype(ak.dtype), ak.reshape(B, T, A_KV, A_HD)], axis=1)
    v_all = jnp.concatenate([cache_v.astype(av.dtype), av.reshape(B, T, A_KV, A_HD)], axis=1)
    qpos = PAST_LEN + jnp.arange(T, dtype=jnp.int32)
    kpos = PAST_LEN - W + jnp.arange(W + T, dtype=jnp.int32)
    dist = (qpos[:, None] - kpos[None, :])[None]
    valid = (dist >= 0) & (dist <= WINDOW) & (kpos[None, None, :] >= 0)
    o = sink_window_attention(q, k_all[:, None], v_all[:, None], dist, valid, rel_bias, sinks)
    return o.reshape(B, T, A_HEADS * A_HD), k_all[:, -W:], v_all[:, -W:]


def moe(xn, w_router, b_router, w_gate, b_gate, w_up, b_up, w_down, b_down):
    B, T, D = xn.shape
    x2 = xn.reshape(-1, D)
    N = x2.shape[0]
    nk = N * TOP_K
    logits = (x2 @ w_router + b_router).astype(F32)
    top_val, top_idx = lax.top_k(logits, TOP_K)
    gates = jax.nn.softmax(top_val, axis=-1)
    flat_e = top_idx.reshape(-1)
    flat_g = gates.reshape(-1)
    order = jnp.argsort(flat_e)
    sorted_e = flat_e[order]
    counts = jnp.bincount(flat_e, length=N_EXPERTS)
    padded = (counts + MOE_BLOCK - 1) // MOE_BLOCK * MOE_BLOCK
    pad_end = jnp.cumsum(padded)
    pad_start = pad_end - padded
    start = jnp.cumsum(counts) - counts
    dest = pad_start[sorted_e] + jnp.arange(nk) - start[sorted_e]
    n_blocks = (nk + N_EXPERTS * (MOE_BLOCK - 1) + MOE_BLOCK - 1) // MOE_BLOCK
    rows = n_blocks * MOE_BLOCK
    row_tok = jnp.zeros((rows,), jnp.int32).at[dest].set((order // TOP_K).astype(jnp.int32))
    row_w = jnp.zeros((rows,), F32).at[dest].set(flat_g[order])
    block_e = jnp.minimum(jnp.searchsorted(pad_end, jnp.arange(n_blocks) * MOE_BLOCK, side='right'),
                          N_EXPERTS - 1)
    xs = x2[row_tok].reshape(n_blocks, MOE_BLOCK, D)

    def expert_block(args):
        xb, e = args
        g = xb @ w_gate[e] + b_gate[e]
        u = xb @ w_up[e] + b_up[e]
        g = jnp.minimum(g, SWIGLU_LIMIT)
        u = jnp.clip(u, -SWIGLU_LIMIT, SWIGLU_LIMIT)
        h = (u + 1) * (g * jax.nn.sigmoid(SWIGLU_ALPHA * g))
        return h @ w_down[e] + b_down[e]

    out = lax.map(expert_block, (xs, block_e))
    y = jax.ops.segment_sum(out.reshape(rows, D).astype(F32) * row_w[:, None], row_tok, num_segments=N)
    return y.astype(xn.dtype).reshape(B, T, D)


def setup_inputs(seed: int = 0) -> dict:
    key = jax.random.key(seed)
    ks = jax.random.split(key, 32)
    cache_win = min(WINDOW, PAST_LEN)
    nrm = lambda k, s, sc: jax.random.normal(k, s, F32) * sc
    f_bias = jnp.linspace(3.0, 6.0, M_HEADS, dtype=F32)[None] + nrm(ks[10], (DEPTH, M_HEADS), 0.1)
    i_bias = nrm(ks[11], (DEPTH, M_HEADS), 0.1)
    return {
        "x_prompt": nrm(ks[0], (BATCH, SEQ, D_MODEL), 1.0),
        "x_sample": nrm(ks[1], (DEC_BATCH, DEC_SEQ, D_MODEL), 1.0),
        "state_C": nrm(ks[2], (DEPTH, DEC_BATCH, M_HEADS, M_DK, M_DV), 0.1),
        "state_n": nrm(ks[3], (DEPTH, DEC_BATCH, M_HEADS, M_DK), 0.1),
        "state_m": nrm(ks[4], (DEPTH, DEC_BATCH, M_HEADS), 1.0),
        "cache_k": nrm(ks[5], (DEPTH, DEC_BATCH, cache_win, A_KV, A_HD), 1.0),
        "cache_v": nrm(ks[6], (DEPTH, DEC_BATCH, cache_win, A_KV, A_HD), 1.0),
        "rel_bias": nrm(ks[7], (NUM_BUCKETS, A_HEADS), 0.1),
        "norm1": 1.0 + nrm(ks[8], (DEPTH, D_MODEL), 0.02),
        "w_in": nrm(ks[9], (DEPTH, D_MODEL, PROJ_DIM), D_MODEL ** -0.5),
        "b_if": jnp.concatenate([i_bias, f_bias], axis=-1),
        "m_gain": 1.0 + nrm(ks[12], (DEPTH, M_HEADS * M_DV), 0.02),
        "sinks": nrm(ks[13], (DEPTH, A_HEADS), 0.5),
        "w_out": nrm(ks[14], (DEPTH, MIX_WIDTH, D_MODEL), MIX_WIDTH ** -0.5),
        "norm2": 1.0 + nrm(ks[15], (DEPTH, D_MODEL), 0.02),
        "w_router": nrm(ks[16], (DEPTH, D_MODEL, N_EXPERTS), D_MODEL ** -0.5),
        "b_router": nrm(ks[17], (DEPTH, N_EXPERTS), 0.01),
        "w_gate": nrm(ks[18], (DEPTH, N_EXPERTS, D_MODEL, D_FF), D_MODEL ** -0.5),
        "b_gate": nrm(ks[19], (DEPTH, N_EXPERTS, D_FF), 0.01),
        "w_up": nrm(ks[20], (DEPTH, N_EXPERTS, D_MODEL, D_FF), D_MODEL ** -0.5),
        "b_up": nrm(ks[21], (DEPTH, N_EXPERTS, D_FF), 0.01),
        "w_down": nrm(ks[22], (DEPTH, N_EXPERTS, D_FF, D_MODEL), D_FF ** -0.5),
        "b_down": nrm(ks[23], (DEPTH, N_EXPERTS, D_MODEL), 0.01),
        "final_norm": 1.0 + nrm(ks[24], (D_MODEL,), 0.02),
    }


def reference(x_prompt, x_sample, state_C, state_n, state_m, cache_k, cache_v, rel_bias,
              norm1, w_in, b_if, m_gain, sinks, w_out, norm2, w_router, b_router,
              w_gate, b_gate, w_up, b_up, w_down, b_down, final_norm):
    yp, ys = x_prompt, x_sample
    Bp = x_prompt.shape[0]
    pC, pn, pm, pk, pv = [], [], [], [], []
    sC, sn, sm, sk, sv = [], [], [], [], []
    for l in range(DEPTH):
        mq, mk, mv, mo, i_pre, logf, aq, ak, av = split_proj(rms_norm(yp, norm1[l]), w_in[l], b_if[l])
        zC = jnp.zeros((Bp, M_HEADS, M_DK, M_DV), F32)
        zn = jnp.zeros((Bp, M_HEADS, M_DK), F32)
        zm = jnp.zeros((Bp, M_HEADS), F32)
        m_out, C1, n1, m1 = mlstm_mix(mq, mk, mv, mo, i_pre, logf, zC, zn, zm, m_gain[l], M_CHUNK)
        a_out, k1, v1 = attn_prompt(aq, ak, av, rel_bias, sinks[l])
        yp = yp + jnp.concatenate([m_out, a_out], axis=-1) @ w_out[l]
        yp = yp + moe(rms_norm(yp, norm2[l]), w_router[l], b_router[l], w_gate[l], b_gate[l],
                      w_up[l], b_up[l], w_down[l], b_down[l])
        pC.append(C1); pn.append(n1); pm.append(m1); pk.append(k1); pv.append(v1)
        mq, mk, mv, mo, i_pre, logf, aq, ak, av = split_proj(rms_norm(ys, norm1[l]), w_in[l], b_if[l])
        m_out, C2, n2, m2 = mlstm_mix(mq, mk, mv, mo, i_pre, logf, state_C[l], state_n[l], state_m[l],
                                      m_gain[l], ys.shape[1])
        a_out, k2, v2 = attn_sample(aq, ak, av, cache_k[l], cache_v[l], rel_bias, sinks[l])
        ys = ys + jnp.concatenate([m_out, a_out], axis=-1) @ w_out[l]
        ys = ys + moe(rms_norm(ys, norm2[l]), w_router[l], b_router[l], w_gate[l], b_gate[l],
                      w_up[l], b_up[l], w_down[l], b_down[l])
        sC.append(C2); sn.append(n2); sm.append(m2); sk.append(k2); sv.append(v2)
    yp = rms_norm(yp, final_norm)
    ys = rms_norm(ys, final_norm)
    p_C, p_n, p_m = jnp.stack(pC), jnp.stack(pn), jnp.stack(pm)
    p_k, p_v = jnp.stack(pk), jnp.stack(pv)
    s_C, s_n, s_m = jnp.stack(sC), jnp.stack(sn), jnp.stack(sm)
    s_k, s_v = jnp.stack(sk), jnp.stack(sv)
    return (yp, ys, p_C, p_n, p_m, p_k, p_v, s_C, s_n, s_m, s_k, s_v)
```

```python
import functools
import math

import numpy as np
import jax
import jax.numpy as jnp
from jax import lax
from jax.experimental import pallas as pl
from jax.experimental.pallas import tpu as pltpu

F32 = jnp.float32
BF16 = jnp.bfloat16
I32 = jnp.int32

D_MODEL = 1024
M_HEADS = 4
M_DK = 64
M_DV = 128
A_HEADS = 8
A_KV = 2
A_GROUP = A_HEADS // A_KV
A_HD = 64
WINDOW = 128
NUM_BUCKETS = 32
MAX_DISTANCE = 128
N_EXPERTS = 32
TOP_K = 4
D_FF = 1024
SWIGLU_LIMIT = 7.0
SWIGLU_ALPHA = 1.702
EPS = 1e-5

LANES = 128
NEG = -1e30
VMEM_LIMIT = 48 * 1024 * 1024

C_MQ, C_MK, C_MV, C_MO, C_AQ, C_AK, C_AV, C_GATE = 0, 256, 512, 1024, 1536, 2048, 2176, 2304
PROJ_W = 2432

MLSTM_CHUNK = 128
ATT_BLOCK = 128
TOK_TILE = 512
FFN_ROWS = 256
FFN_COLS = 256
DISPATCH_TILE = 256
COMBINE_TILE = 128
SAMPLE_MLSTM_TB = 16
SAMPLE_ATT_TB = 8


def _t5_bucket_np(dist):
    n = np.maximum(dist, 0)
    max_exact = NUM_BUCKETS // 2
    ratio = np.log(np.maximum(n, 1).astype(np.float32) / np.float32(max_exact)) / np.float32(
        math.log(MAX_DISTANCE / max_exact))
    large = max_exact + (ratio * np.float32(NUM_BUCKETS - max_exact)).astype(np.int32)
    large = np.minimum(large, NUM_BUCKETS - 1)
    return np.where(n < max_exact, n, large).astype(np.int32)


def _cparams(sem):
    return pltpu.CompilerParams(dimension_semantics=sem, vmem_limit_bytes=VMEM_LIMIT)


def _rms(x, g):
    return x * lax.rsqrt(jnp.mean(x * x, axis=-1, keepdims=True) + EPS) * g


def _log_sigmoid(x):
    return jnp.minimum(x, 0.0) - jnp.log(1.0 + jnp.exp(-jnp.abs(x)))


def _sigmoid(x):
    return 1.0 / (1.0 + jnp.exp(-x))


def _inproj_kernel(x_ref, g_ref, w_ref, z_ref):
    xn = _rms(x_ref[...], g_ref[...]).astype(BF16)
    z_ref[...] = jnp.dot(xn, w_ref[...], preferred_element_type=F32)


def _inproj(x2, norm_row, w_bf16, tile):
    n = x2.shape[0]
    return pl.pallas_call(
        _inproj_kernel,
        out_shape=jax.ShapeDtypeStruct((n, PROJ_W), F32),
        grid=(n // tile,),
        in_specs=[
            pl.BlockSpec((tile, D_MODEL), lambda i: (i, 0)),
            pl.BlockSpec((1, D_MODEL), lambda i: (0, 0)),
            pl.BlockSpec((D_MODEL, PROJ_W), lambda i: (0, 0)),
        ],
        out_specs=pl.BlockSpec((tile, PROJ_W), lambda i: (i, 0)),
        compiler_params=_cparams(("arbitrary",)),
        name="inproj",
    )(x2, norm_row, w_bf16)


def _mlstm_prompt_kernel(q_ref, k_ref, v_ref, o_ref, gt_ref, bias_ref, gain_ref,
                         out_ref, c_ref, n_ref, m_ref, s_scr, m_scr):
    L = MLSTM_CHUNK
    c = pl.program_id(1)

    @pl.when(c == 0)
    def _():
        s_scr[...] = jnp.zeros_like(s_scr)
        m_scr[...] = jnp.zeros_like(m_scr)

    gb = gt_ref[...] + bias_ref[...]
    ls = _log_sigmoid(gb)
    row = lax.broadcasted_iota(I32, (L, L), 0)
    col = lax.broadcasted_iota(I32, (L, L), 1)
    causal = col <= row
    tril = causal.astype(F32)
    bcum = jnp.dot(tril, ls, preferred_element_type=F32, precision=lax.Precision.HIGHEST)
    gb_t = gb.T
    bcum_t = bcum.T
    k_t = (k_ref[...] * (M_DK ** -0.5)).T
    ones = jnp.ones((L, M_DV), BF16)

    for h in range(M_HEADS):
        b_col = bcum[:, M_HEADS + h:M_HEADS + h + 1]
        b_row = bcum_t[M_HEADS + h:M_HEADS + h + 1, :]
        i_row = gb_t[h:h + 1, :]
        m_prev = m_scr[h:h + 1, 0:1]
        dmat = jnp.where(causal, b_col + (i_row - b_row), NEG)
        a_col = b_col + m_prev
        mt = jnp.maximum(a_col, jnp.max(dmat, axis=1, keepdims=True))
        w_intra = jnp.exp(dmat - mt)
        w_inter = jnp.exp(a_col - mt)
        q_h = q_ref[:, h * M_DK:(h + 1) * M_DK].astype(BF16)
        kt_h = k_t[h * M_DK:(h + 1) * M_DK, :]
        qk = jnp.dot(q_h, kt_h.astype(BF16), preferred_element_type=F32)
        s_w = (qk * w_intra).astype(BF16)
        v_ext = jnp.concatenate([v_ref[:, h * M_DV:(h + 1) * M_DV].astype(BF16), ones], axis=1)
        state = s_scr[h]
        inter = jnp.dot(q_h, state.astype(BF16), preferred_element_type=F32)
        tot = w_inter * inter + jnp.dot(s_w, v_ext, preferred_element_type=F32)
        num = tot[:, :M_DV]
        qn = tot[:, M_DV:]
        den = jnp.maximum(jnp.abs(qn), jnp.exp(-mt))
        hh = num / den
        hn = hh * lax.rsqrt(jnp.mean(hh * hh, axis=-1, keepdims=True) + EPS)
        hn = hn * gain_ref[:, h * M_DV:(h + 1) * M_DV]
        out = hn * _sigmoid(o_ref[:, h * M_DV:(h + 1) * M_DV])
        out_ref[:, h * M_DV:(h + 1) * M_DV] = out.astype(out_ref.dtype)
        b_last = b_col[L - 1:L, :]
        m_new = mt[L - 1:L, :]
        g_prev = jnp.exp(b_last + m_prev - m_new)
        g_row = jnp.exp(b_last - b_row + i_row - m_new)
        kg_t = (kt_h * g_row).astype(BF16)
        s_scr[h] = g_prev * state + jnp.dot(kg_t, v_ext, preferred_element_type=F32)
        m_scr[h:h + 1, :] = jnp.broadcast_to(m_new, (1, LANES))

    @pl.when(c == pl.num_programs(1) - 1)
    def _():
        for h in range(M_HEADS):
            st = s_scr[h]
            c_ref[0, h] = st[:, :M_DV]
            n_ref[0, h] = st[:, M_DV:]
        m_ref[0] = m_scr[...]


def _mlstm_prompt(z, bias_row, gain_row, batch, seq):
    L = MLSTM_CHUNK
    nc = seq // L
    rows = lambda b, c: b * nc + c
    return pl.pallas_call(
        _mlstm_prompt_kernel,
        out_shape=(
            jax.ShapeDtypeStruct((batch * seq, M_HEADS * M_DV), BF16),
            jax.ShapeDtypeStruct((batch, M_HEADS, M_DK, M_DV), F32),
            jax.ShapeDtypeStruct((batch, M_HEADS, M_DK, M_DV), F32),
            jax.ShapeDtypeStruct((batch, 8, LANES), F32),
        ),
        grid=(batch, nc),
        in_specs=[
            pl.BlockSpec((L, 256), lambda b, c: (rows(b, c), C_MQ // 256)),
            pl.BlockSpec((L, 256), lambda b, c: (rows(b, c), C_MK // 256)),
            pl.BlockSpec((L, 512), lambda b, c: (rows(b, c), C_MV // 512)),
            pl.BlockSpec((L, 512), lambda b, c: (rows(b, c), C_MO // 512)),
            pl.BlockSpec((L, LANES), lambda b, c: (rows(b, c), C_GATE // LANES)),
            pl.BlockSpec((1, LANES), lambda b, c: (0, 0)),
            pl.BlockSpec((1, M_HEADS * M_DV), lambda b, c: (0, 0)),
        ],
        out_specs=(
            pl.BlockSpec((L, M_HEADS * M_DV), lambda b, c: (rows(b, c), 0)),
            pl.BlockSpec((1, M_HEADS, M_DK, M_DV), lambda b, c: (b, 0, 0, 0)),
            pl.BlockSpec((1, M_HEADS, M_DK, M_DV), lambda b, c: (b, 0, 0, 0)),
            pl.BlockSpec((1, 8, LANES), lambda b, c: (b, 0, 0)),
        ),
        scratch_shapes=[pltpu.VMEM((M_HEADS, M_DK, 2 * M_DV), F32), pltpu.VMEM((8, LANES), F32)],
        compiler_params=_cparams(("arbitrary", "arbitrary")),
        name="mlstm_prompt",
    )(z, z, z, z, z, bias_row, gain_row)


def _attn_prompt_kernel(relb_ref, sink_ref, q_ref, kp_ref, kc_ref, vp_ref, vc_ref, bucket_ref,
                        out_ref, bias_scr):
    B = ATT_BLOCK
    j = pl.program_id(1)

    @pl.when((pl.program_id(0) == 0) & (j == 0))
    def _():
        bucket = bucket_ref[...]
        for h in range(A_HEADS):
            acc = jnp.full((B, 2 * B), NEG, F32)
            for bk in range(NUM_BUCKETS):
                acc = jnp.where(bucket == bk, relb_ref[bk * A_HEADS + h], acc)
            bias_scr[h] = acc

    scale = A_HD ** -0.5
    s_iota = lax.broadcasted_iota(I32, (B, 2 * B), 1)
    first = jnp.where((s_iota < B) & (j == 0), NEG, 0.0)
    outs = []
    for h in range(A_HEADS):
        g = h // A_GROUP
        q_h = q_ref[:, h * A_HD:(h + 1) * A_HD].astype(BF16)
        k2 = jnp.concatenate([kp_ref[:, g * A_HD:(g + 1) * A_HD], kc_ref[:, g * A_HD:(g + 1) * A_HD]],
                             axis=0).astype(BF16)
        v2 = jnp.concatenate([vp_ref[:, g * A_HD:(g + 1) * A_HD], vc_ref[:, g * A_HD:(g + 1) * A_HD]],
                             axis=0).astype(BF16)
        logits = lax.dot_general(q_h, k2, (((1,), (1,)), ((), ())), preferred_element_type=F32)
        logits = logits * scale + bias_scr[h] + first
        sink = sink_ref[h]
        m = jnp.maximum(jnp.max(logits, axis=-1, keepdims=True), sink)
        p = jnp.exp(logits - m)
        den = jnp.sum(p, axis=-1, keepdims=True) + jnp.exp(sink - m)
        o = jnp.dot(p.astype(BF16), v2, preferred_element_type=F32) / den
        outs.append(o)
    out_ref[...] = jnp.concatenate(outs, axis=1).astype(out_ref.dtype)


def _attn_prompt(z, rel_bias, sinks, batch, seq):
    B = ATT_BLOCK
    nb = seq // B
    qi = np.arange(B)[:, None]
    si = np.arange(2 * B)[None, :]
    dist = qi + B - si
    bucket = np.where((dist >= 0) & (dist <= WINDOW), _t5_bucket_np(dist), -1).astype(np.int32)
    cur = lambda b, j, *_: b * nb + j
    prev = lambda b, j, *_: b * nb + jnp.maximum(j - 1, 0)
    grid_spec = pltpu.PrefetchScalarGridSpec(
        num_scalar_prefetch=2,
        grid=(batch, nb),
        in_specs=[
            pl.BlockSpec((B, 512), lambda b, j, *_: (cur(b, j), C_AQ // 512)),
            pl.BlockSpec((B, LANES), lambda b, j, *_: (prev(b, j), C_AK // LANES)),
            pl.BlockSpec((B, LANES), lambda b, j, *_: (cur(b, j), C_AK // LANES)),
            pl.BlockSpec((B, LANES), lambda b, j, *_: (prev(b, j), C_AV // LANES)),
            pl.BlockSpec((B, LANES), lambda b, j, *_: (cur(b, j), C_AV // LANES)),
            pl.BlockSpec((B, 2 * B), lambda b, j, *_: (0, 0)),
        ],
        out_specs=pl.BlockSpec((B, A_HEADS * A_HD), lambda b, j, *_: (cur(b, j), 0)),
        scratch_shapes=[pltpu.VMEM((A_HEADS, B, 2 * B), F32)],
    )
    return pl.pallas_call(
        _attn_prompt_kernel,
        out_shape=jax.ShapeDtypeStruct((batch * seq, A_HEADS * A_HD), BF16),
        grid_spec=grid_spec,
        compiler_params=_cparams(("arbitrary", "arbitrary")),
        name="attn_prompt",
    )(rel_bias.reshape(-1), sinks, z, z, z, z, z, jnp.asarray(bucket))


def _outproj_router_kernel(x_ref, mo_ref, ao_ref, wm_ref, wa_ref, g_ref, wr_ref, br_ref, cin_ref,
                           y_ref, yn_ref, eidx_ref, gate_ref, rank_ref, cout_ref, carry):
    T = x_ref.shape[0]
    i = pl.program_id(0)

    @pl.when(i == 0)
    def _():
        carry[...] = cin_ref[...]

    y = (x_ref[...] + jnp.dot(mo_ref[...], wm_ref[...], preferred_element_type=F32)
         + jnp.dot(ao_ref[...], wa_ref[...], preferred_element_type=F32))
    y_ref[...] = y
    yn = _rms(y, g_ref[...])
    yn_ref[...] = yn
    logits = jnp.dot(yn, wr_ref[...], preferred_element_type=F32,
                     precision=lax.Precision.HIGHEST) + br_ref[...]
    lane = lax.broadcasted_iota(I32, (T, LANES), 1)
    lane_f = lane.astype(F32)
    vals, idxs, hots = [], [], []
    l = logits
    for _ in range(TOP_K):
        mx = jnp.max(l, axis=-1, keepdims=True)
        idx = jnp.min(jnp.where(l == mx, lane_f, float(LANES)), axis=-1, keepdims=True)
        hot = lane_f == idx
        l = jnp.where(hot, -jnp.inf, l)
        vals.append(mx)
        idxs.append(idx)
        hots.append(hot)
    es = [jnp.exp(v - vals[0]) for v in vals]
    tot = es[0] + es[1] + es[2] + es[3]
    sel = jnp.where(hots[0] | hots[1] | hots[2] | hots[3], 1.0, 0.0)
    row = lax.broadcasted_iota(I32, (T, T), 0)
    col = lax.broadcasted_iota(I32, (T, T), 1)
    strict = (col < row).astype(BF16)
    before = carry[...] + jnp.dot(strict, sel.astype(BF16), preferred_element_type=F32)
    eidx = jnp.zeros((T, LANES), I32)
    gate = jnp.zeros((T, LANES), F32)
    rank = jnp.zeros((T, LANES), I32)
    for k in range(TOP_K):
        r_k = jnp.sum(jnp.where(hots[k], before, 0.0), axis=-1, keepdims=True)
        eidx = jnp.where(lane == k, idxs[k].astype(I32), eidx)
        gate = jnp.where(lane == k, es[k] / tot, gate)
        rank = jnp.where(lane == k, r_k.astype(I32), rank)
    eidx_ref[...] = eidx
    gate_ref[...] = gate
    rank_ref[...] = rank
    carry[...] = carry[...] + jnp.sum(sel, axis=0, keepdims=True)
    cout_ref[...] = carry[...]


def _outproj_router(x2, m_out, a_out, w_m, w_a, norm_row, w_r, b_r, counts_in, tile):
    n = x2.shape[0]
    tok = lambda i: (i, 0)
    fix = lambda i: (0, 0)
    return pl.pallas_call(
        _outproj_router_kernel,
        out_shape=(
            jax.ShapeDtypeStruct((n, D_MODEL), F32),
            jax.ShapeDtypeStruct((n, D_MODEL), F32),
            jax.ShapeDtypeStruct((n, LANES), I32),
            jax.ShapeDtypeStruct((n, LANES), F32),
            jax.ShapeDtypeStruct((n, LANES), I32),
            jax.ShapeDtypeStruct((1, LANES), F32),
        ),
        grid=(n // tile,),
        in_specs=[
            pl.BlockSpec((tile, D_MODEL), tok),
            pl.BlockSpec((tile, 512), tok),
            pl.BlockSpec((tile, 512), tok),
            pl.BlockSpec((512, D_MODEL), fix),
            pl.BlockSpec((512, D_MODEL), fix),
            pl.BlockSpec((1, D_MODEL), fix),
            pl.BlockSpec((D_MODEL, LANES), fix),
            pl.BlockSpec((1, LANES), fix),
            pl.BlockSpec((1, LANES), fix),
        ],
        out_specs=(
            pl.BlockSpec((tile, D_MODEL), tok),
            pl.BlockSpec((tile, D_MODEL), tok),
            pl.BlockSpec((tile, LANES), tok),
            pl.BlockSpec((tile, LANES), tok),
            pl.BlockSpec((tile, LANES), tok),
            pl.BlockSpec((1, LANES), fix),
        ),
        scratch_shapes=[pltpu.VMEM((1, LANES), F32)],
        compiler_params=_cparams(("arbitrary",)),
        name="outproj_router",
    )(x2, m_out, a_out, w_m, w_a, norm_row, w_r, b_r, counts_in)


def _dispatch_kernel(dest_ref, tail_ref, yn_ref, *rest, first):
    xs_ref, zero_scr, sem = rest if first else rest[1:]
    T = yn_ref.shape[0]
    i = pl.program_id(0)

    if first:
        @pl.when(i == 0)
        def _():
            zero_scr[...] = jnp.zeros_like(zero_scr)

            def fill(e, carry):
                @pl.when(tail_ref[e] >= 0)
                def _():
                    pltpu.make_async_copy(zero_scr, xs_ref.at[pl.ds(pl.multiple_of(tail_ref[e], FFN_ROWS), FFN_ROWS), :], sem).start()
                return carry

            def drain(e, carry):
                @pl.when(tail_ref[e] >= 0)
                def _():
                    pltpu.make_async_copy(zero_scr, xs_ref.at[pl.ds(pl.multiple_of(tail_ref[e], FFN_ROWS), FFN_ROWS), :], sem).wait()
                return carry

            lax.fori_loop(0, N_EXPERTS, fill, 0)
            lax.fori_loop(0, N_EXPERTS, drain, 0)

    base = i * (T * TOP_K)

    def issue(t, carry):
        for k in range(TOP_K):
            d = dest_ref[base + t * TOP_K + k]
            pltpu.make_async_copy(yn_ref.at[pl.ds(t, 1), :], xs_ref.at[pl.ds(d, 1), :], sem).start()
        return carry

    def drain_rows(t, carry):
        for k in range(TOP_K):
            pltpu.make_async_copy(yn_ref.at[pl.ds(0, 1), :], xs_ref.at[pl.ds(0, 1), :], sem).wait()
        return carry

    lax.fori_loop(0, T, issue, 0)
    lax.fori_loop(0, T, drain_rows, 0)


def _dispatch(dest_flat, tail, yn, xs, n_rows, tile):
    n = yn.shape[0]
    first = xs is None
    in_specs = [pl.BlockSpec((tile, D_MODEL), lambda i, *_: (i, 0))]
    operands = [dest_flat, tail, yn]
    if not first:
        in_specs.append(pl.BlockSpec(memory_space=pl.ANY))
        operands.append(xs)
    grid_spec = pltpu.PrefetchScalarGridSpec(
        num_scalar_prefetch=2,
        grid=(n // tile,),
        in_specs=in_specs,
        out_specs=pl.BlockSpec(memory_space=pl.ANY),
        scratch_shapes=[pltpu.VMEM((FFN_ROWS, D_MODEL), F32), pltpu.SemaphoreType.DMA],
    )
    return pl.pallas_call(
        functools.partial(_dispatch_kernel, first=first),
        out_shape=jax.ShapeDtypeStruct((n_rows, D_MODEL), F32),
        grid_spec=grid_spec,
        input_output_aliases={} if first else {3: 0},
        compiler_params=_cparams(("arbitrary",)),
        name="moe_dispatch_first" if first else "moe_dispatch_more",
    )(*operands)


def _ffn_kernel(be_ref, nused_ref, xs_ref, wg_ref, bg_ref, wu_ref, bu_ref, wd_ref, bd_ref, out_ref):
    del be_ref

    @pl.when(pl.program_id(0) < nused_ref[0])
    def _():
        x = xs_ref[...].astype(BF16)
        acc = jnp.zeros((FFN_ROWS, D_MODEL), F32)
        for c in range(D_FF // FFN_COLS):
            cs = slice(c * FFN_COLS, (c + 1) * FFN_COLS)
            g = jnp.dot(x, wg_ref[:, cs], preferred_element_type=F32) + bg_ref[:, cs]
            u = jnp.dot(x, wu_ref[:, cs], preferred_element_type=F32) + bu_ref[:, cs]
            g = jnp.minimum(g, SWIGLU_LIMIT)
            u = jnp.clip(u, -SWIGLU_LIMIT, SWIGLU_LIMIT)
            hcol = (u + 1.0) * (g * _sigmoid(SWIGLU_ALPHA * g))
            acc = acc + jnp.dot(hcol.astype(BF16), wd_ref[cs, :], preferred_element_type=F32)
        out_ref[...] = acc + bd_ref[...]


def _ffn(block_e, nused, xs, wg, bg, wu, bu, wd, bd):
    nb = xs.shape[0] // FFN_ROWS
    blk = lambda i, be, nu: (jnp.minimum(i, nu[0] - 1), 0)
    wsel = lambda i, be, nu: (be[i], 0, 0)
    grid_spec = pltpu.PrefetchScalarGridSpec(
        num_scalar_prefetch=2,
        grid=(nb,),
        in_specs=[
            pl.BlockSpec((FFN_ROWS, D_MODEL), blk),
            pl.BlockSpec((None, D_MODEL, D_FF), wsel),
            pl.BlockSpec((None, 1, D_FF), wsel),
            pl.BlockSpec((None, D_MODEL, D_FF), wsel),
            pl.BlockSpec((None, 1, D_FF), wsel),
            pl.BlockSpec((None, D_FF, D_MODEL), wsel),
            pl.BlockSpec((None, 1, D_MODEL), wsel),
        ],
        out_specs=pl.BlockSpec((FFN_ROWS, D_MODEL), blk),
    )
    return pl.pallas_call(
        _ffn_kernel,
        out_shape=jax.ShapeDtypeStruct(xs.shape, F32),
        grid_spec=grid_spec,
        compiler_params=_cparams(("arbitrary",)),
        name="moe_ffn",
    )(block_e, nused, xs, wg, bg, wu, bu, wd, bd)


def _combine_kernel(dest_ref, y_ref, gate_ref, fn_ref, ffn_ref, out_ref, buf, sem):
    T = y_ref.shape[0]
    i = pl.program_id(0)
    n = pl.num_programs(0)
    slot = i % 2

    def issue(tile, s):
        base = tile * (T * TOP_K)

        def body(t, carry):
            for k in range(TOP_K):
                d = dest_ref[base + t * TOP_K + k]
                pltpu.make_async_copy(ffn_ref.at[pl.ds(d, 1), :], buf.at[s, k, pl.ds(t, 1), :],
                                      sem.at[s]).start()
            return carry

        lax.fori_loop(0, T, body, 0)

    @pl.when(i == 0)
    def _():
        issue(0, 0)

    @pl.when(i + 1 < n)
    def _():
        issue(i + 1, 1 - slot)

    for k in range(TOP_K):
        pltpu.make_async_copy(ffn_ref.at[pl.ds(0, T), :], buf.at[slot, k], sem.at[slot]).wait()

    acc = y_ref[...]
    gate = gate_ref[...]
    for k in range(TOP_K):
        acc = acc + gate[:, k:k + 1] * buf[slot, k]
    out_ref[...] = _rms(acc, fn_ref[...])


def _combine(dest_flat, y, gate, fnorm_row, ffn_out, tile):
    n = y.shape[0]
    grid_spec = pltpu.PrefetchScalarGridSpec(
        num_scalar_prefetch=1,
        grid=(n // tile,),
        in_specs=[
            pl.BlockSpec((tile, D_MODEL), lambda i, *_: (i, 0)),
            pl.BlockSpec((tile, LANES), lambda i, *_: (i, 0)),
            pl.BlockSpec((1, D_MODEL), lambda i, *_: (0, 0)),
            pl.BlockSpec(memory_space=pl.ANY),
        ],
        out_specs=pl.BlockSpec((tile, D_MODEL), lambda i, *_: (i, 0)),
        scratch_shapes=[pltpu.VMEM((2, TOP_K, tile, D_MODEL), F32), pltpu.SemaphoreType.DMA((2,))],
    )
    return pl.pallas_call(
        _combine_kernel,
        out_shape=jax.ShapeDtypeStruct((n, D_MODEL), F32),
        grid_spec=grid_spec,
        compiler_params=_cparams(("arbitrary",)),
        name="moe_combine",
    )(dest_flat, y, gate, fnorm_row, ffn_out)


def _mlstm_step_kernel(q_ref, k_ref, v_ref, o_ref, gt_ref, bias_ref, gain_ref, c0_ref, n0_ref, m0_ref,
                       out_ref, c_ref, n_ref, m_ref):
    TB = SAMPLE_MLSTM_TB
    gb = gt_ref[...] + bias_ref[...]
    ls = _log_sigmoid(gb)
    lane = lax.broadcasted_iota(I32, (TB, LANES), 1)
    eye = (lax.broadcasted_iota(I32, (M_DK, M_DK), 0) == lax.broadcasted_iota(I32, (M_DK, M_DK), 1)).astype(F32)
    nt = (((1,), (1,)), ((), ()))
    m_all = jnp.zeros((TB, LANES), F32)
    for h in range(M_HEADS):
        i_pre = gb[:, h:h + 1]
        a = ls[:, M_HEADS + h:M_HEADS + h + 1] + m0_ref[:, h:h + 1]
        mt = jnp.maximum(a, i_pre)
        w_intra = jnp.exp(i_pre - mt)
        w_inter = jnp.exp(a - mt)
        q_h = q_ref[:, h * M_DK:(h + 1) * M_DK]
        k_h = k_ref[:, h * M_DK:(h + 1) * M_DK] * (M_DK ** -0.5)
        v_h = v_ref[:, h * M_DV:(h + 1) * M_DV]
        n0_h = n0_ref[:, h, :]
        s = jnp.sum(q_h * k_h, axis=-1, keepdims=True) * w_intra
        qn = w_inter * jnp.sum(q_h * n0_h, axis=-1, keepdims=True) + s
        den = jnp.maximum(jnp.abs(qn), jnp.exp(-mt))
        q_t = lax.dot_general(eye, q_h, nt, preferred_element_type=F32, precision=lax.Precision.HIGHEST)
        k_t = lax.dot_general(eye, k_h, nt, preferred_element_type=F32, precision=lax.Precision.HIGHEST)
        rows = []
        for b in range(TB):
            c0 = c0_ref[b, h]
            qc = jnp.sum(c0 * q_t[:, b:b + 1], axis=0, keepdims=True)
            v_b = v_h[b:b + 1, :]
            rows.append(w_inter[b:b + 1, :] * qc + s[b:b + 1, :] * v_b)
            c_ref[b, h] = w_inter[b:b + 1, :] * c0 + (w_intra[b:b + 1, :] * k_t[:, b:b + 1]) * v_b
        num = jnp.concatenate(rows, axis=0)
        hh = num / den
        hn = hh * lax.rsqrt(jnp.mean(hh * hh, axis=-1, keepdims=True) + EPS)
        hn = hn * gain_ref[:, h * M_DV:(h + 1) * M_DV]
        out_ref[:, h * M_DV:(h + 1) * M_DV] = (hn * _sigmoid(o_ref[:, h * M_DV:(h + 1) * M_DV])).astype(out_ref.dtype)
        n_ref[:, h * M_DK:(h + 1) * M_DK] = w_inter * n0_h + w_intra * k_h
        m_all = jnp.where(lane == h, mt, m_all)
    m_ref[...] = m_all


def _mlstm_step(zs, bias_row, gain_row, c0, n0, m0):
    TB = SAMPLE_MLSTM_TB
    nb = zs.shape[0]
    tok = lambda i: (i, 0)
    return pl.pallas_call(
        _mlstm_step_kernel,
        out_shape=(
            jax.ShapeDtypeStruct((nb, M_HEADS * M_DV), BF16),
            jax.ShapeDtypeStruct((nb, M_HEADS, M_DK, M_DV), F32),
            jax.ShapeDtypeStruct((nb, M_HEADS * M_DK), F32),
            jax.ShapeDtypeStruct((nb, LANES), F32),
        ),
        grid=(nb // TB,),
        in_specs=[
            pl.BlockSpec((TB, 256), lambda i: (i, C_MQ // 256)),
            pl.BlockSpec((TB, 256), lambda i: (i, C_MK // 256)),
            pl.BlockSpec((TB, 512), lambda i: (i, C_MV // 512)),
            pl.BlockSpec((TB, 512), lambda i: (i, C_MO // 512)),
            pl.BlockSpec((TB, LANES), lambda i: (i, C_GATE // LANES)),
            pl.BlockSpec((1, LANES), lambda i: (0, 0)),
            pl.BlockSpec((1, M_HEADS * M_DV), lambda i: (0, 0)),
            pl.BlockSpec((TB, M_HEADS, M_DK, M_DV), lambda i: (i, 0, 0, 0)),
            pl.BlockSpec((TB, M_HEADS, M_DK), lambda i: (i, 0, 0)),
            pl.BlockSpec((TB, M_HEADS), tok),
        ],
        out_specs=(
            pl.BlockSpec((TB, M_HEADS * M_DV), tok),
            pl.BlockSpec((TB, M_HEADS, M_DK, M_DV), lambda i: (i, 0, 0, 0)),
            pl.BlockSpec((TB, M_HEADS * M_DK), tok),
            pl.BlockSpec((TB, LANES), tok),
        ),
        compiler_params=_cparams(("arbitrary",)),
        name="mlstm_step",
    )(zs, zs, zs, zs, zs, bias_row, gain_row, c0, n0, m0)


def _attn_step_kernel(q_ref, kn_ref, vn_ref, ck_ref, cv_ref, bucket_ref, relt_ref, sink_ref,
                      out_ref, nk_ref, nv_ref, bias_scr):
    TB = SAMPLE_ATT_TB
    W = ck_ref.shape[1]

    @pl.when(pl.program_id(0) == 0)
    def _():
        bucket = jnp.broadcast_to(bucket_ref[...], (A_HEADS, W))
        acc = jnp.zeros((A_HEADS, W), F32)
        for bk in range(NUM_BUCKETS):
            acc = jnp.where(bucket == bk, relt_ref[:, bk:bk + 1], acc)
        bias_scr[...] = acc

    scale = A_HD ** -0.5
    nt = (((1,), (1,)), ((), ()))
    bias = bias_scr[...]
    bias_new = relt_ref[:, 0:1]
    sink = sink_ref[...]
    low = lax.broadcasted_iota(I32, (A_HEADS, 1), 0) < A_GROUP
    for b in range(TB):
        q = q_ref[b]
        qb = q.astype(BF16)
        kc = ck_ref[b]
        vc = cv_ref[b]
        kn = kn_ref[b:b + 1, :]
        vn = vn_ref[b:b + 1, :]
        l0 = lax.dot_general(qb, kc[:, :A_HD].astype(BF16), nt, preferred_element_type=F32)
        l1 = lax.dot_general(qb, kc[:, A_HD:].astype(BF16), nt, preferred_element_type=F32)
        logits = jnp.where(low, l0, l1) * scale + bias
        kn_h = jnp.where(low, kn[:, :A_HD], kn[:, A_HD:])
        vn_h = jnp.where(low, vn[:, :A_HD], vn[:, A_HD:])
        l_new = jnp.sum(q * kn_h, axis=-1, keepdims=True) * scale + bias_new
        m = jnp.maximum(jnp.maximum(jnp.max(logits, axis=-1, keepdims=True), l_new), sink)
        p = jnp.exp(logits - m)
        p_new = jnp.exp(l_new - m)
        den = jnp.sum(p, axis=-1, keepdims=True) + p_new + jnp.exp(sink - m)
        pb = p.astype(BF16)
        o0 = jnp.dot(pb, vc[:, :A_HD].astype(BF16), preferred_element_type=F32)
        o1 = jnp.dot(pb, vc[:, A_HD:].astype(BF16), preferred_element_type=F32)
        o = jnp.where(low, o0, o1) + p_new * vn_h
        out_ref[b] = o / den
        nk_ref[b, 0:W - 1, :] = ck_ref[b, 1:W, :]
        nk_ref[b, W - 1:W, :] = kn
        nv_ref[b, 0:W - 1, :] = cv_ref[b, 1:W, :]
        nv_ref[b, W - 1:W, :] = vn


def _attn_step(q3, k_new, v_new, ck, cv, rel_bias, sinks):
    TB = SAMPLE_ATT_TB
    nb, W = ck.shape[0], ck.shape[1]
    bucket = _t5_bucket_np(W - np.arange(W))[None, :].astype(np.int32)
    tok = lambda i: (i, 0)
    tok3 = lambda i: (i, 0, 0)
    fix = lambda i: (0, 0)
    return pl.pallas_call(
        _attn_step_kernel,
        out_shape=(
            jax.ShapeDtypeStruct((nb, A_HEADS, A_HD), F32),
            jax.ShapeDtypeStruct(ck.shape, F32),
            jax.ShapeDtypeStruct(cv.shape, F32),
        ),
        grid=(nb // TB,),
        in_specs=[
            pl.BlockSpec((TB, A_HEADS, A_HD), tok3),
            pl.BlockSpec((TB, LANES), lambda i: (i, C_AK // LANES)),
            pl.BlockSpec((TB, LANES), lambda i: (i, C_AV // LANES)),
            pl.BlockSpec((TB, W, A_KV * A_HD), tok3),
            pl.BlockSpec((TB, W, A_KV * A_HD), tok3),
            pl.BlockSpec((1, W), fix),
            pl.BlockSpec((A_HEADS, NUM_BUCKETS), fix),
            pl.BlockSpec((A_HEADS, 1), fix),
        ],
        out_specs=(
            pl.BlockSpec((TB, A_HEADS, A_HD), tok3),
            pl.BlockSpec((TB, W, A_KV * A_HD), tok3),
            pl.BlockSpec((TB, W, A_KV * A_HD), tok3),
        ),
        scratch_shapes=[pltpu.VMEM((A_HEADS, W), F32)],
        compiler_params=_cparams(("arbitrary",)),
        name="attn_step",
    )(q3, k_new, v_new, ck, cv, jnp.asarray(bucket), rel_bias.T, sinks.reshape(A_HEADS, 1))


def _reorder_w_in(w_in):
    o = 0
    parts = {}
    for name, width in (("mq", 256), ("mk", 256), ("mv", 512), ("mo", 512), ("mi", 4), ("mf", 4),
                        ("aq", 512), ("ak", 128), ("av", 128)):
        parts[name] = w_in[:, o:o + width]
        o += width
    pad = jnp.zeros((w_in.shape[0], LANES - 2 * M_HEADS), w_in.dtype)
    cols = [parts[n] for n in ("mq", "mk", "mv", "mo", "aq", "ak", "av", "mi", "mf")] + [pad]
    return jnp.concatenate(cols, axis=1).astype(BF16)


def _lane_row(v, fill=0.0):
    return jnp.concatenate([v.astype(F32), jnp.full((LANES - v.shape[0],), fill, F32)])[None, :]


def kernel(x_prompt, x_sample, state_C, state_n, state_m, cache_k, cache_v, rel_bias, norm1, w_in, b_if,
           m_gain, sinks, w_out, norm2, w_router, b_router, w_gate, b_gate, w_up, b_up, w_down, b_down,
           final_norm):
    assert norm1.shape[0] == 1, "single-layer trunk"
    batch, seq, _ = x_prompt.shape
    nsmp = x_sample.shape[0]
    n_p = batch * seq
    W = cache_k.shape[2]

    xp = x_prompt.reshape(n_p, D_MODEL)
    xs_ = x_sample.reshape(nsmp, D_MODEL)
    w_in_r = _reorder_w_in(w_in[0])
    n1 = norm1[0][None, :]
    n2 = norm2[0][None, :]
    fn = final_norm[None, :]
    bias_row = _lane_row(b_if[0])
    gain_row = m_gain[0][None, :]
    w_m = w_out[0][:M_HEADS * M_DV].astype(BF16)
    w_a = w_out[0][M_HEADS * M_DV:].astype(BF16)
    w_r = jnp.concatenate([w_router[0], jnp.zeros((D_MODEL, LANES - N_EXPERTS), F32)], axis=1)
    b_r = _lane_row(b_router[0], NEG)

    zp = _inproj(xp, n1, w_in_r, TOK_TILE)
    zs = _inproj(xs_, n1, w_in_r, nsmp)
    m_out_p, p_c, p_nrep, p_mrep = _mlstm_prompt(zp, bias_row, gain_row, batch, seq)
    a_out_p = _attn_prompt(zp, rel_bias, sinks[0], batch, seq)
    m_out_s, s_c, s_n, s_mrep = _mlstm_step(zs, bias_row, gain_row, state_C[0], state_n[0], state_m[0])
    q3 = zs[:, C_AQ:C_AQ + A_HEADS * A_HD].reshape(nsmp, A_HEADS, A_HD)
    a3, s_k, s_v = _attn_step(q3, zs, zs, cache_k[0].reshape(nsmp, W, A_KV * A_HD),
                              cache_v[0].reshape(nsmp, W, A_KV * A_HD), rel_bias, sinks[0])
    a_out_s = a3.reshape(nsmp, A_HEADS * A_HD).astype(BF16)

    zero_counts = jnp.zeros((1, LANES), F32)
    y_p, yn_p, e_p, g_p, r_p, cnt_p = _outproj_router(xp, m_out_p, a_out_p, w_m, w_a, n2, w_r, b_r,
                                                      zero_counts, TOK_TILE)
    y_s, yn_s, e_s, g_s, r_s, cnt = _outproj_router(xs_, m_out_s, a_out_s, w_m, w_a, n2, w_r, b_r,
                                                    cnt_p, nsmp)

    counts = cnt[0, :N_EXPERTS].astype(I32)
    padded = (counts + FFN_ROWS - 1) // FFN_ROWS * FFN_ROWS
    pad_end = jnp.cumsum(padded)
    pad_start = pad_end - padded
    n_rows = ((n_p + nsmp) * TOP_K + N_EXPERTS * (FFN_ROWS - 1) + FFN_ROWS - 1) // FFN_ROWS * FFN_ROWS
    n_blocks = n_rows // FFN_ROWS
    nused = jnp.maximum(pad_end[-1] // FFN_ROWS, 1).astype(I32)
    blk_start = jnp.minimum(jnp.arange(n_blocks, dtype=I32), nused - 1) * FFN_ROWS
    block_e = jnp.minimum(jnp.searchsorted(pad_end, blk_start, side="right"), N_EXPERTS - 1).astype(I32)
    tail = jnp.where(counts % FFN_ROWS != 0, pad_end - FFN_ROWS, -1).astype(I32)
    dest_p = (pad_start[e_p[:, :TOP_K]] + r_p[:, :TOP_K]).reshape(-1).astype(I32)
    dest_s = (pad_start[e_s[:, :TOP_K]] + r_s[:, :TOP_K]).reshape(-1).astype(I32)

    xs_rows = _dispatch(dest_p, tail, yn_p, None, n_rows, DISPATCH_TILE)
    xs_rows = _dispatch(dest_s, tail, yn_s, xs_rows, n_rows, nsmp)
    ffn_out = _ffn(block_e, nused.reshape(1), xs_rows,
                   w_gate[0].astype(BF16), b_gate[0][:, None, :], w_up[0].astype(BF16), b_up[0][:, None, :],
                   w_down[0].astype(BF16), b_down[0][:, None, :])
    out_p = _combine(dest_p, y_p, g_p, fn, ffn_out, COMBINE_TILE)
    out_s = _combine(dest_s, y_s, g_s, fn, ffn_out, COMBINE_TILE)

    kv_shape = (1, batch, WINDOW, A_KV, A_HD)
    zk = zp[:, C_AK:C_AK + A_KV * A_HD].reshape(batch, seq, A_KV * A_HD)[:, seq - WINDOW:]
    zv = zp[:, C_AV:C_AV + A_KV * A_HD].reshape(batch, seq, A_KV * A_HD)[:, seq - WINDOW:]
    return (
        out_p.reshape(batch, seq, D_MODEL),
        out_s.reshape(nsmp, 1, D_MODEL),
        p_c[None],
        p_nrep[None, :, :, :, 0],
        p_mrep[None, :, :M_HEADS, 0],
        zk.reshape(kv_shape),
        zv.reshape(kv_shape),
        s_c[None],
        s_n.reshape(1, nsmp, M_HEADS, M_DK),
        s_mrep[None, :, :M_HEADS],
        s_k.reshape(1, nsmp, W, A_KV, A_HD),
        s_v.reshape(1, nsmp, W, A_KV, A_HD),
    )
```

```python
import functools
import math

import numpy as np
import jax
import jax.numpy as jnp
from jax import lax
from jax.experimental import pallas as pl
from jax.experimental.pallas import tpu as pltpu

F32 = jnp.float32
BF16 = jnp.bfloat16
I32 = jnp.int32

D_MODEL = 1024
M_HEADS = 4
M_DK = 64
M_DV = 128
A_HEADS = 8
A_KV = 2
A_GROUP = A_HEADS // A_KV
A_HD = 64
WINDOW = 128
NUM_BUCKETS = 32
MAX_DISTANCE = 128
N_EXPERTS = 32
TOP_K = 4
D_FF = 1024
SWIGLU_LIMIT = 7.0
SWIGLU_ALPHA = 1.702
EPS = 1e-5

LANES = 128
NEG = -1e30
VMEM_LIMIT = 48 * 1024 * 1024

C_MQ, C_MK, C_MV, C_MO, C_AQ, C_AK, C_AV, C_GATE = 0, 256, 512, 1024, 1536, 2048, 2176, 2304
PROJ_W = 2432

MLSTM_CHUNK = 128
MLSTM_SEQS = 2
ATT_BLOCK = 128
TOK_TILE = 512
FFN_ROWS = 256
FFN_COLS = 256
CAST_ROWS = 128
DISPATCH_TILE = 256
COMBINE_TILE = 128
DMA_UNROLL = 8
SAMPLE_MLSTM_TB = 16
SAMPLE_ATT_TB = 8


def _t5_bucket_np(dist):
    n = np.maximum(dist, 0)
    max_exact = NUM_BUCKETS // 2
    ratio = np.log(np.maximum(n, 1).astype(np.float32) / np.float32(max_exact)) / np.float32(
        math.log(MAX_DISTANCE / max_exact))
    large = max_exact + (ratio * np.float32(NUM_BUCKETS - max_exact)).astype(np.int32)
    large = np.minimum(large, NUM_BUCKETS - 1)
    return np.where(n < max_exact, n, large).astype(np.int32)


def _cparams(sem):
    return pltpu.CompilerParams(dimension_semantics=sem, vmem_limit_bytes=VMEM_LIMIT)


def _rms(x, g):
    return x * lax.rsqrt(jnp.mean(x * x, axis=-1, keepdims=True) + EPS) * g


def _log_sigmoid(x):
    return jnp.minimum(x, 0.0) - jnp.log(1.0 + jnp.exp(-jnp.abs(x)))


def _sigmoid(x):
    return 1.0 / (1.0 + jnp.exp(-x))


def _inproj_kernel(x_ref, g_ref, w_ref, z_ref):
    xn = _rms(x_ref[...], g_ref[...]).astype(BF16)
    z_ref[...] = jnp.dot(xn, w_ref[...], preferred_element_type=F32)


def _inproj(x2, norm_row, w_bf16, tile):
    n = x2.shape[0]
    return pl.pallas_call(
        _inproj_kernel,
        out_shape=jax.ShapeDtypeStruct((n, PROJ_W), F32),
        grid=(n // tile,),
        in_specs=[
            pl.BlockSpec((tile, D_MODEL), lambda i: (i, 0)),
            pl.BlockSpec((1, D_MODEL), lambda i: (0, 0)),
            pl.BlockSpec((D_MODEL, PROJ_W), lambda i: (0, 0)),
        ],
        out_specs=pl.BlockSpec((tile, PROJ_W), lambda i: (i, 0)),
        compiler_params=_cparams(("arbitrary",)),
        name="inproj",
    )(x2, norm_row, w_bf16)


def _mlstm_prompt_kernel(q_ref, k_ref, v_ref, o_ref, gt_ref, bias_ref, gain_ref,
                         out_ref, c_ref, n_ref, m_ref, s_scr, m_scr):
    L = MLSTM_CHUNK
    c = pl.program_id(1)

    @pl.when(c == 0)
    def _():
        s_scr[...] = jnp.zeros_like(s_scr)
        m_scr[...] = jnp.zeros_like(m_scr)

    row = lax.broadcasted_iota(I32, (L, L), 0)
    col = lax.broadcasted_iota(I32, (L, L), 1)
    causal = col <= row
    tril = causal.astype(F32)
    ones = jnp.ones((L, M_DV), BF16)

    for nb in range(MLSTM_SEQS):
        gb = gt_ref[nb] + bias_ref[...]
        ls = _log_sigmoid(gb)
        bcum = jnp.dot(tril, ls, preferred_element_type=F32, precision=lax.Precision.HIGHEST)
        gb_t = gb.T
        bcum_t = bcum.T
        k_t = (k_ref[nb] * (M_DK ** -0.5)).T
        for h in range(M_HEADS):
            sh = nb * M_HEADS + h
            b_col = bcum[:, M_HEADS + h:M_HEADS + h + 1]
            b_row = bcum_t[M_HEADS + h:M_HEADS + h + 1, :]
            i_row = gb_t[h:h + 1, :]
            m_prev = m_scr[nb, h:h + 1, 0:1]
            dmat = jnp.where(causal, b_col + (i_row - b_row), NEG)
            a_col = b_col + m_prev
            mt = jnp.maximum(a_col, jnp.max(dmat, axis=1, keepdims=True))
            w_intra = jnp.exp(dmat - mt)
            w_inter = jnp.exp(a_col - mt)
            q_h = q_ref[nb, :, h * M_DK:(h + 1) * M_DK].astype(BF16)
            kt_h = k_t[h * M_DK:(h + 1) * M_DK, :]
            qk = jnp.dot(q_h, kt_h.astype(BF16), preferred_element_type=F32)
            s_w = (qk * w_intra).astype(BF16)
            v_ext = jnp.concatenate([v_ref[nb, :, h * M_DV:(h + 1) * M_DV].astype(BF16), ones], axis=1)
            state = s_scr[sh]
            inter = jnp.dot(q_h, state.astype(BF16), preferred_element_type=F32)
            tot = w_inter * inter + jnp.dot(s_w, v_ext, preferred_element_type=F32)
            num = tot[:, :M_DV]
            qn = tot[:, M_DV:]
            den = jnp.maximum(jnp.abs(qn), jnp.exp(-mt))
            hh = num / den
            hn = hh * lax.rsqrt(jnp.mean(hh * hh, axis=-1, keepdims=True) + EPS)
            hn = hn * gain_ref[:, h * M_DV:(h + 1) * M_DV]
            out = hn * _sigmoid(o_ref[nb, :, h * M_DV:(h + 1) * M_DV])
            out_ref[nb, :, h * M_DV:(h + 1) * M_DV] = out.astype(out_ref.dtype)
            b_last = b_col[L - 1:L, :]
            m_new = mt[L - 1:L, :]
            g_prev = jnp.exp(b_last + m_prev - m_new)
            g_row = jnp.exp(b_last - b_row + i_row - m_new)
            kg_t = (kt_h * g_row).astype(BF16)
            s_scr[sh] = g_prev * state + jnp.dot(kg_t, v_ext, preferred_element_type=F32)
            m_scr[nb, h:h + 1, :] = jnp.broadcast_to(m_new, (1, LANES))

    @pl.when(c == pl.num_programs(1) - 1)
    def _():
        for nb in range(MLSTM_SEQS):
            for h in range(M_HEADS):
                st = s_scr[nb * M_HEADS + h]
                c_ref[nb, h] = st[:, :M_DV]
                n_ref[nb, h] = st[:, M_DV:]
        m_ref[...] = m_scr[...]


def _mlstm_prompt(z, bias_row, gain_row, batch, seq):
    L = MLSTM_CHUNK
    S = MLSTM_SEQS
    z3 = z.reshape(batch, seq, PROJ_W)
    return pl.pallas_call(
        _mlstm_prompt_kernel,
        out_shape=(
            jax.ShapeDtypeStruct((batch, seq, M_HEADS * M_DV), BF16),
            jax.ShapeDtypeStruct((batch, M_HEADS, M_DK, M_DV), F32),
            jax.ShapeDtypeStruct((batch, M_HEADS, M_DK, M_DV), F32),
            jax.ShapeDtypeStruct((batch, 8, LANES), F32),
        ),
        grid=(batch // S, seq // L),
        in_specs=[
            pl.BlockSpec((S, L, 256), lambda b, c: (b, c, C_MQ // 256)),
            pl.BlockSpec((S, L, 256), lambda b, c: (b, c, C_MK // 256)),
            pl.BlockSpec((S, L, 512), lambda b, c: (b, c, C_MV // 512)),
            pl.BlockSpec((S, L, 512), lambda b, c: (b, c, C_MO // 512)),
            pl.BlockSpec((S, L, LANES), lambda b, c: (b, c, C_GATE // LANES)),
            pl.BlockSpec((1, LANES), lambda b, c: (0, 0)),
            pl.BlockSpec((1, M_HEADS * M_DV), lambda b, c: (0, 0)),
        ],
        out_specs=(
            pl.BlockSpec((S, L, M_HEADS * M_DV), lambda b, c: (b, c, 0)),
            pl.BlockSpec((S, M_HEADS, M_DK, M_DV), lambda b, c: (b, 0, 0, 0)),
            pl.BlockSpec((S, M_HEADS, M_DK, M_DV), lambda b, c: (b, 0, 0, 0)),
            pl.BlockSpec((S, 8, LANES), lambda b, c: (b, 0, 0)),
        ),
        scratch_shapes=[pltpu.VMEM((S * M_HEADS, M_DK, 2 * M_DV), F32), pltpu.VMEM((S, 8, LANES), F32)],
        compiler_params=_cparams(("arbitrary", "arbitrary")),
        name="mlstm_prompt",
    )(z3, z3, z3, z3, z3, bias_row, gain_row)


def _attn_prompt_kernel(relb_ref, sink_ref, q_ref, kp_ref, kc_ref, vp_ref, vc_ref, bucket_ref,
                        out_ref, bias_scr):
    B = ATT_BLOCK
    j = pl.program_id(1)

    @pl.when((pl.program_id(0) == 0) & (j == 0))
    def _():
        bucket = bucket_ref[...]
        for h in range(A_HEADS):
            acc = jnp.full((B, 2 * B), NEG, F32)
            for bk in range(NUM_BUCKETS):
                acc = jnp.where(bucket == bk, relb_ref[bk * A_HEADS + h], acc)
            bias_scr[h] = acc

    scale = A_HD ** -0.5
    s_iota = lax.broadcasted_iota(I32, (B, 2 * B), 1)
    first = jnp.where((s_iota < B) & (j == 0), NEG, 0.0)
    outs = []
    for h in range(A_HEADS):
        g = h // A_GROUP
        q_h = q_ref[:, h * A_HD:(h + 1) * A_HD].astype(BF16)
        k2 = jnp.concatenate([kp_ref[:, g * A_HD:(g + 1) * A_HD], kc_ref[:, g * A_HD:(g + 1) * A_HD]],
                             axis=0).astype(BF16)
        v2 = jnp.concatenate([vp_ref[:, g * A_HD:(g + 1) * A_HD], vc_ref[:, g * A_HD:(g + 1) * A_HD]],
                             axis=0).astype(BF16)
        logits = lax.dot_general(q_h, k2, (((1,), (1,)), ((), ())), preferred_element_type=F32)
        logits = logits * scale + bias_scr[h] + first
        sink = sink_ref[h]
        m = jnp.maximum(jnp.max(logits, axis=-1, keepdims=True), sink)
        p = jnp.exp(logits - m)
        den = jnp.sum(p, axis=-1, keepdims=True) + jnp.exp(sink - m)
        o = jnp.dot(p.astype(BF16), v2, preferred_element_type=F32) / den
        outs.append(o)
    out_ref[...] = jnp.concatenate(outs, axis=1).astype(out_ref.dtype)


def _attn_prompt(z, rel_bias, sinks, batch, seq):
    B = ATT_BLOCK
    nb = seq // B
    qi = np.arange(B)[:, None]
    si = np.arange(2 * B)[None, :]
    dist = qi + B - si
    bucket = np.where((dist >= 0) & (dist <= WINDOW), _t5_bucket_np(dist), -1).astype(np.int32)
    cur = lambda b, j, *_: b * nb + j
    prev = lambda b, j, *_: b * nb + jnp.maximum(j - 1, 0)
    grid_spec = pltpu.PrefetchScalarGridSpec(
        num_scalar_prefetch=2,
        grid=(batch, nb),
        in_specs=[
            pl.BlockSpec((B, 512), lambda b, j, *_: (cur(b, j), C_AQ // 512)),
            pl.BlockSpec((B, LANES), lambda b, j, *_: (prev(b, j), C_AK // LANES)),
            pl.BlockSpec((B, LANES), lambda b, j, *_: (cur(b, j), C_AK // LANES)),
            pl.BlockSpec((B, LANES), lambda b, j, *_: (prev(b, j), C_AV // LANES)),
            pl.BlockSpec((B, LANES), lambda b, j, *_: (cur(b, j), C_AV // LANES)),
            pl.BlockSpec((B, 2 * B), lambda b, j, *_: (0, 0)),
        ],
        out_specs=pl.BlockSpec((B, A_HEADS * A_HD), lambda b, j, *_: (cur(b, j), 0)),
        scratch_shapes=[pltpu.VMEM((A_HEADS, B, 2 * B), F32)],
    )
    return pl.pallas_call(
        _attn_prompt_kernel,
        out_shape=jax.ShapeDtypeStruct((batch * seq, A_HEADS * A_HD), BF16),
        grid_spec=grid_spec,
        compiler_params=_cparams(("arbitrary", "arbitrary")),
        name="attn_prompt",
    )(rel_bias.reshape(-1), sinks, z, z, z, z, z, jnp.asarray(bucket))


def _outproj_router_kernel(x_ref, mo_ref, ao_ref, wm_ref, wa_ref, g_ref, wr_ref, br_ref, cin_ref,
                           y_ref, yn_ref, eidx_ref, gate_ref, rank_ref, cout_ref, carry):
    T = x_ref.shape[0]
    i = pl.program_id(0)

    @pl.when(i == 0)
    def _():
        carry[...] = cin_ref[...]

    y = (x_ref[...] + jnp.dot(mo_ref[...], wm_ref[...], preferred_element_type=F32)
         + jnp.dot(ao_ref[...], wa_ref[...], preferred_element_type=F32))
    y_ref[...] = y
    yn = _rms(y, g_ref[...])
    yn_ref[...] = yn
    yh = yn.astype(BF16)
    yl = (yn - yh.astype(F32)).astype(BF16)
    hh = jnp.dot(yh, wr_ref[...], preferred_element_type=F32)
    lh = jnp.dot(yl, wr_ref[:, :LANES], preferred_element_type=F32)
    logits = hh[:, :LANES] + (hh[:, LANES:] + lh) + br_ref[...]
    lane = lax.broadcasted_iota(I32, (T, LANES), 1)
    lane_f = lane.astype(F32)
    vals, idxs, hots = [], [], []
    l = logits
    for _ in range(TOP_K):
        mx = jnp.max(l, axis=-1, keepdims=True)
        idx = jnp.min(jnp.where(l == mx, lane_f, float(LANES)), axis=-1, keepdims=True)
        hot = lane_f == idx
        l = jnp.where(hot, -jnp.inf, l)
        vals.append(mx)
        idxs.append(idx)
        hots.append(hot)
    es = [jnp.exp(v - vals[0]) for v in vals]
    tot = es[0] + es[1] + es[2] + es[3]
    sel = jnp.where(hots[0] | hots[1] | hots[2] | hots[3], 1.0, 0.0)
    row = lax.broadcasted_iota(I32, (T, T), 0)
    col = lax.broadcasted_iota(I32, (T, T), 1)
    strict = (col < row).astype(BF16)
    before = carry[...] + jnp.dot(strict, sel.astype(BF16), preferred_element_type=F32)
    eidx = jnp.zeros((T, LANES), I32)
    gate = jnp.zeros((T, LANES), F32)
    rank = jnp.zeros((T, LANES), I32)
    for k in range(TOP_K):
        r_k = jnp.sum(jnp.where(hots[k], before, 0.0), axis=-1, keepdims=True)
        eidx = jnp.where(lane == k, idxs[k].astype(I32), eidx)
        gate = jnp.where(lane == k, es[k] / tot, gate)
        rank = jnp.where(lane == k, r_k.astype(I32), rank)
    eidx_ref[...] = eidx
    gate_ref[...] = gate
    rank_ref[...] = rank
    carry[...] = carry[...] + jnp.sum(sel, axis=0, keepdims=True)
    cout_ref[...] = carry[...]


def _outproj_router(x2, m_out, a_out, w_m, w_a, norm_row, w_r, b_r, counts_in, tile):
    n = x2.shape[0]
    tok = lambda i: (i, 0)
    fix = lambda i: (0, 0)
    return pl.pallas_call(
        _outproj_router_kernel,
        out_shape=(
            jax.ShapeDtypeStruct((n, D_MODEL), F32),
            jax.ShapeDtypeStruct((n, D_MODEL), F32),
            jax.ShapeDtypeStruct((n, LANES), I32),
            jax.ShapeDtypeStruct((n, LANES), F32),
            jax.ShapeDtypeStruct((n, LANES), I32),
            jax.ShapeDtypeStruct((1, LANES), F32),
        ),
        grid=(n // tile,),
        in_specs=[
            pl.BlockSpec((tile, D_MODEL), tok),
            pl.BlockSpec((tile, 512), tok),
            pl.BlockSpec((tile, 512), tok),
            pl.BlockSpec((512, D_MODEL), fix),
            pl.BlockSpec((512, D_MODEL), fix),
            pl.BlockSpec((1, D_MODEL), fix),
            pl.BlockSpec((D_MODEL, 2 * LANES), fix),
            pl.BlockSpec((1, LANES), fix),
            pl.BlockSpec((1, LANES), fix),
        ],
        out_specs=(
            pl.BlockSpec((tile, D_MODEL), tok),
            pl.BlockSpec((tile, D_MODEL), tok),
            pl.BlockSpec((tile, LANES), tok),
            pl.BlockSpec((tile, LANES), tok),
            pl.BlockSpec((tile, LANES), tok),
            pl.BlockSpec((1, LANES), fix),
        ),
        scratch_shapes=[pltpu.VMEM((1, LANES), F32)],
        compiler_params=_cparams(("arbitrary",)),
        name="outproj_router",
    )(x2, m_out, a_out, w_m, w_a, norm_row, w_r, b_r, counts_in)


def _dispatch_kernel(dest_ref, tail_ref, yn_ref, *rest, first):
    xs_ref, zero_scr, sem = rest if first else rest[1:]
    T = yn_ref.shape[0]
    i = pl.program_id(0)

    if first:
        @pl.when(i == 0)
        def _():
            zero_scr[...] = jnp.zeros_like(zero_scr)

            def fill(e, carry):
                @pl.when(tail_ref[e] >= 0)
                def _():
                    pltpu.make_async_copy(zero_scr, xs_ref.at[pl.ds(pl.multiple_of(tail_ref[e], FFN_ROWS), FFN_ROWS), :], sem).start()
                return carry

            def drain(e, carry):
                @pl.when(tail_ref[e] >= 0)
                def _():
                    pltpu.make_async_copy(zero_scr, xs_ref.at[pl.ds(pl.multiple_of(tail_ref[e], FFN_ROWS), FFN_ROWS), :], sem).wait()
                return carry

            lax.fori_loop(0, N_EXPERTS, fill, 0)
            lax.fori_loop(0, N_EXPERTS, drain, 0)

    base = i * (T * TOP_K)

    def issue(tb, carry):
        for u in range(DMA_UNROLL):
            t = tb * DMA_UNROLL + u
            for k in range(TOP_K):
                d = dest_ref[base + t * TOP_K + k]
                pltpu.make_async_copy(yn_ref.at[pl.ds(t, 1), :], xs_ref.at[pl.ds(d, 1), :], sem).start()
        return carry

    lax.fori_loop(0, T // DMA_UNROLL, issue, 0)
    for k in range(TOP_K):
        pltpu.make_async_copy(yn_ref, xs_ref.at[pl.ds(0, T), :], sem).wait()


def _dispatch(dest_flat, tail, yn, xs, n_rows, tile):
    n = yn.shape[0]
    first = xs is None
    in_specs = [pl.BlockSpec((tile, D_MODEL), lambda i, *_: (i, 0))]
    operands = [dest_flat, tail, yn]
    if not first:
        in_specs.append(pl.BlockSpec(memory_space=pl.ANY))
        operands.append(xs)
    grid_spec = pltpu.PrefetchScalarGridSpec(
        num_scalar_prefetch=2,
        grid=(n // tile,),
        in_specs=in_specs,
        out_specs=pl.BlockSpec(memory_space=pl.ANY),
        scratch_shapes=[pltpu.VMEM((FFN_ROWS, D_MODEL), F32), pltpu.SemaphoreType.DMA],
    )
    return pl.pallas_call(
        functools.partial(_dispatch_kernel, first=first),
        out_shape=jax.ShapeDtypeStruct((n_rows, D_MODEL), F32),
        grid_spec=grid_spec,
        input_output_aliases={} if first else {3: 0},
        compiler_params=_cparams(("arbitrary",)),
        name="moe_dispatch_first" if first else "moe_dispatch_more",
    )(*operands)


def _ffn_kernel(be_ref, nused_ref, xs_ref, wg_ref, bg_ref, wu_ref, bu_ref, wd_ref, bd_ref, out_ref,
                wg_bf, wu_bf, wd_bf, h_scr):
    i = pl.program_id(0)

    @pl.when(i < nused_ref[0])
    def _():
        @pl.when((i == 0) | (be_ref[i] != be_ref[jnp.maximum(i - 1, 0)]))
        def _():
            for src, dst in ((wg_ref, wg_bf), (wu_ref, wu_bf), (wd_ref, wd_bf)):
                for r in range(0, src.shape[0], CAST_ROWS):
                    dst[r:r + CAST_ROWS, :] = src[r:r + CAST_ROWS, :].astype(BF16)

        x = xs_ref[...].astype(BF16)
        for c in range(D_FF // FFN_COLS):
            cs = slice(c * FFN_COLS, (c + 1) * FFN_COLS)
            g = jnp.dot(x, wg_bf[:, cs], preferred_element_type=F32) + bg_ref[:, cs]
            u = jnp.dot(x, wu_bf[:, cs], preferred_element_type=F32) + bu_ref[:, cs]
            g = jnp.minimum(g, SWIGLU_LIMIT)
            u = jnp.clip(u, -SWIGLU_LIMIT, SWIGLU_LIMIT)
            h_scr[:, cs] = ((u + 1.0) * (g * _sigmoid(SWIGLU_ALPHA * g))).astype(BF16)
        out_ref[...] = jnp.dot(h_scr[...], wd_bf[...], preferred_element_type=F32) + bd_ref[...]


def _ffn(block_e, nused, xs, wg, bg, wu, bu, wd, bd):
    nb = xs.shape[0] // FFN_ROWS
    blk = lambda i, be, nu: (jnp.minimum(i, nu[0] - 1), 0)
    wsel = lambda i, be, nu: (be[i], 0, 0)
    grid_spec = pltpu.PrefetchScalarGridSpec(
        num_scalar_prefetch=2,
        grid=(nb,),
        in_specs=[
            pl.BlockSpec((FFN_ROWS, D_MODEL), blk),
            pl.BlockSpec((None, D_MODEL, D_FF), wsel),
            pl.BlockSpec((None, 1, D_FF), wsel),
            pl.BlockSpec((None, D_MODEL, D_FF), wsel),
            pl.BlockSpec((None, 1, D_FF), wsel),
            pl.BlockSpec((None, D_FF, D_MODEL), wsel),
            pl.BlockSpec((None, 1, D_MODEL), wsel),
        ],
        out_specs=pl.BlockSpec((FFN_ROWS, D_MODEL), blk),
        scratch_shapes=[pltpu.VMEM((D_MODEL, D_FF), BF16), pltpu.VMEM((D_MODEL, D_FF), BF16),
                        pltpu.VMEM((D_FF, D_MODEL), BF16), pltpu.VMEM((FFN_ROWS, D_FF), BF16)],
    )
    return pl.pallas_call(
        _ffn_kernel,
        out_shape=jax.ShapeDtypeStruct(xs.shape, F32),
        grid_spec=grid_spec,
        compiler_params=_cparams(("arbitrary",)),
        name="moe_ffn",
    )(block_e, nused, xs, wg, bg, wu, bu, wd, bd)


def _combine_kernel(dest_ref, y_ref, gate_ref, fn_ref, ffn_ref, out_ref, buf, sem):
    T = y_ref.shape[0]
    i = pl.program_id(0)
    n = pl.num_programs(0)
    slot = i % 2

    def issue(tile, s):
        base = tile * (T * TOP_K)

        def body(tb, carry):
            for u in range(DMA_UNROLL):
                t = tb * DMA_UNROLL + u
                for k in range(TOP_K):
                    d = dest_ref[base + t * TOP_K + k]
                    pltpu.make_async_copy(ffn_ref.at[pl.ds(d, 1), :], buf.at[s, k, pl.ds(t, 1), :],
                                          sem.at[s]).start()
            return carry

        lax.fori_loop(0, T // DMA_UNROLL, body, 0)

    @pl.when(i == 0)
    def _():
        issue(0, 0)

    @pl.when(i + 1 < n)
    def _():
        issue(i + 1, 1 - slot)

    for k in range(TOP_K):
        pltpu.make_async_copy(ffn_ref.at[pl.ds(0, T), :], buf.at[slot, k], sem.at[slot]).wait()

    acc = y_ref[...]
    gate = gate_ref[...]
    for k in range(TOP_K):
        acc = acc + gate[:, k:k + 1] * buf[slot, k]
    out_ref[...] = _rms(acc, fn_ref[...])


def _combine(dest_flat, y, gate, fnorm_row, ffn_out, tile):
    n = y.shape[0]
    grid_spec = pltpu.PrefetchScalarGridSpec(
        num_scalar_prefetch=1,
        grid=(n // tile,),
        in_specs=[
            pl.BlockSpec((tile, D_MODEL), lambda i, *_: (i, 0)),
            pl.BlockSpec((tile, LANES), lambda i, *_: (i, 0)),
            pl.BlockSpec((1, D_MODEL), lambda i, *_: (0, 0)),
            pl.BlockSpec(memory_space=pl.ANY),
        ],
        out_specs=pl.BlockSpec((tile, D_MODEL), lambda i, *_: (i, 0)),
        scratch_shapes=[pltpu.VMEM((2, TOP_K, tile, D_MODEL), F32), pltpu.SemaphoreType.DMA((2,))],
    )
    return pl.pallas_call(
        _combine_kernel,
        out_shape=jax.ShapeDtypeStruct((n, D_MODEL), F32),
        grid_spec=grid_spec,
        compiler_params=_cparams(("arbitrary",)),
        name="moe_combine",
    )(dest_flat, y, gate, fnorm_row, ffn_out)


def _mlstm_step_kernel(q_ref, k_ref, v_ref, o_ref, gt_ref, bias_ref, gain_ref, c0_ref, n0_ref, m0_ref,
                       out_ref, c_ref, n_ref, m_ref):
    TB = SAMPLE_MLSTM_TB
    gb = gt_ref[...] + bias_ref[...]
    ls = _log_sigmoid(gb)
    lane = lax.broadcasted_iota(I32, (TB, LANES), 1)
    eye = (lax.broadcasted_iota(I32, (M_DK, M_DK), 0) == lax.broadcasted_iota(I32, (M_DK, M_DK), 1)).astype(F32)
    nt = (((1,), (1,)), ((), ()))
    m_all = jnp.zeros((TB, LANES), F32)
    for h in range(M_HEADS):
        i_pre = gb[:, h:h + 1]
        a = ls[:, M_HEADS + h:M_HEADS + h + 1] + m0_ref[:, h:h + 1]
        mt = jnp.maximum(a, i_pre)
        w_intra = jnp.exp(i_pre - mt)
        w_inter = jnp.exp(a - mt)
        q_h = q_ref[:, h * M_DK:(h + 1) * M_DK]
        k_h = k_ref[:, h * M_DK:(h + 1) * M_DK] * (M_DK ** -0.5)
        v_h = v_ref[:, h * M_DV:(h + 1) * M_DV]
        n0_h = n0_ref[:, h, :]
        s = jnp.sum(q_h * k_h, axis=-1, keepdims=True) * w_intra
        qn = w_inter * jnp.sum(q_h * n0_h, axis=-1, keepdims=True) + s
        den = jnp.maximum(jnp.abs(qn), jnp.exp(-mt))
        q_t = lax.dot_general(eye, q_h, nt, preferred_element_type=F32, precision=lax.Precision.HIGHEST)
        k_t = lax.dot_general(eye, k_h, nt, preferred_element_type=F32, precision=lax.Precision.HIGHEST)
        rows = []
        for b in range(TB):
            c0 = c0_ref[b, h]
            qc = jnp.sum(c0 * q_t[:, b:b + 1], axis=0, keepdims=True)
            v_b = v_h[b:b + 1, :]
            rows.append(w_inter[b:b + 1, :] * qc + s[b:b + 1, :] * v_b)
            c_ref[b, h] = w_inter[b:b + 1, :] * c0 + (w_intra[b:b + 1, :] * k_t[:, b:b + 1]) * v_b
        num = jnp.concatenate(rows, axis=0)
        hh = num / den
        hn = hh * lax.rsqrt(jnp.mean(hh * hh, axis=-1, keepdims=True) + EPS)
        hn = hn * gain_ref[:, h * M_DV:(h + 1) * M_DV]
        out_ref[:, h * M_DV:(h + 1) * M_DV] = (hn * _sigmoid(o_ref[:, h * M_DV:(h + 1) * M_DV])).astype(out_ref.dtype)
        n_ref[:, h * M_DK:(h + 1) * M_DK] = w_inter * n0_h + w_intra * k_h
        m_all = jnp.where(lane == h, mt, m_all)
    m_ref[...] = m_all


def _mlstm_step(zs, bias_row, gain_row, c0, n0, m0):
    TB = SAMPLE_MLSTM_TB
    nb = zs.shape[0]
    tok = lambda i: (i, 0)
    return pl.pallas_call(
        _mlstm_step_kernel,
        out_shape=(
            jax.ShapeDtypeStruct((nb, M_HEADS * M_DV), BF16),
            jax.ShapeDtypeStruct((nb, M_HEADS, M_DK, M_DV), F32),
            jax.ShapeDtypeStruct((nb, M_HEADS * M_DK), F32),
            jax.ShapeDtypeStruct((nb, LANES), F32),
        ),
        grid=(nb // TB,),
        in_specs=[
            pl.BlockSpec((TB, 256), lambda i: (i, C_MQ // 256)),
            pl.BlockSpec((TB, 256), lambda i: (i, C_MK // 256)),
            pl.BlockSpec((TB, 512), lambda i: (i, C_MV // 512)),
            pl.BlockSpec((TB, 512), lambda i: (i, C_MO // 512)),
            pl.BlockSpec((TB, LANES), lambda i: (i, C_GATE // LANES)),
            pl.BlockSpec((1, LANES), lambda i: (0, 0)),
            pl.BlockSpec((1, M_HEADS * M_DV), lambda i: (0, 0)),
            pl.BlockSpec((TB, M_HEADS, M_DK, M_DV), lambda i: (i, 0, 0, 0)),
            pl.BlockSpec((TB, M_HEADS, M_DK), lambda i: (i, 0, 0)),
            pl.BlockSpec((TB, M_HEADS), tok),
        ],
        out_specs=(
            pl.BlockSpec((TB, M_HEADS * M_DV), tok),
            pl.BlockSpec((TB, M_HEADS, M_DK, M_DV), lambda i: (i, 0, 0, 0)),
            pl.BlockSpec((TB, M_HEADS * M_DK), tok),
            pl.BlockSpec((TB, LANES), tok),
        ),
        compiler_params=_cparams(("arbitrary",)),
        name="mlstm_step",
    )(zs, zs, zs, zs, zs, bias_row, gain_row, c0, n0, m0)


def _attn_step_kernel(q_ref, kn_ref, vn_ref, ck_ref, cv_ref, bucket_ref, relt_ref, sink_ref,
                      out_ref, nk_ref, nv_ref, bias_scr):
    TB = SAMPLE_ATT_TB
    W = ck_ref.shape[1]

    @pl.when(pl.program_id(0) == 0)
    def _():
        bucket = jnp.broadcast_to(bucket_ref[...], (A_HEADS, W))
        acc = jnp.zeros((A_HEADS, W), F32)
        for bk in range(NUM_BUCKETS):
            acc = jnp.where(bucket == bk, relt_ref[:, bk:bk + 1], acc)
        bias_scr[...] = acc

    scale = A_HD ** -0.5
    nt = (((1,), (1,)), ((), ()))
    bias = bias_scr[...]
    bias_new = relt_ref[:, 0:1]
    sink = sink_ref[...]
    low = lax.broadcasted_iota(I32, (A_HEADS, 1), 0) < A_GROUP
    for b in range(TB):
        q = q_ref[b]
        qb = q.astype(BF16)
        kc = ck_ref[b]
        vc = cv_ref[b]
        kn = kn_ref[b:b + 1, :]
        vn = vn_ref[b:b + 1, :]
        l0 = lax.dot_general(qb, kc[:, :A_HD].astype(BF16), nt, preferred_element_type=F32)
        l1 = lax.dot_general(qb, kc[:, A_HD:].astype(BF16), nt, preferred_element_type=F32)
        logits = jnp.where(low, l0, l1) * scale + bias
        kn_h = jnp.where(low, kn[:, :A_HD], kn[:, A_HD:])
        vn_h = jnp.where(low, vn[:, :A_HD], vn[:, A_HD:])
        l_new = jnp.sum(q * kn_h, axis=-1, keepdims=True) * scale + bias_new
        m = jnp.maximum(jnp.maximum(jnp.max(logits, axis=-1, keepdims=True), l_new), sink)
        p = jnp.exp(logits - m)
        p_new = jnp.exp(l_new - m)
        den = jnp.sum(p, axis=-1, keepdims=True) + p_new + jnp.exp(sink - m)
        pb = p.astype(BF16)
        o0 = jnp.dot(pb, vc[:, :A_HD].astype(BF16), preferred_element_type=F32)
        o1 = jnp.dot(pb, vc[:, A_HD:].astype(BF16), preferred_element_type=F32)
        o = jnp.where(low, o0, o1) + p_new * vn_h
        out_ref[b] = o / den
        nk_ref[b, 0:W - 1, :] = ck_ref[b, 1:W, :]
        nk_ref[b, W - 1:W, :] = kn
        nv_ref[b, 0:W - 1, :] = cv_ref[b, 1:W, :]
        nv_ref[b, W - 1:W, :] = vn


def _attn_step(q3, k_new, v_new, ck, cv, rel_bias, sinks):
    TB = SAMPLE_ATT_TB
    nb, W = ck.shape[0], ck.shape[1]
    bucket = _t5_bucket_np(W - np.arange(W))[None, :].astype(np.int32)
    tok = lambda i: (i, 0)
    tok3 = lambda i: (i, 0, 0)
    fix = lambda i: (0, 0)
    return pl.pallas_call(
        _attn_step_kernel,
        out_shape=(
            jax.ShapeDtypeStruct((nb, A_HEADS, A_HD), F32),
            jax.ShapeDtypeStruct(ck.shape, F32),
            jax.ShapeDtypeStruct(cv.shape, F32),
        ),
        grid=(nb // TB,),
        in_specs=[
            pl.BlockSpec((TB, A_HEADS, A_HD), tok3),
            pl.BlockSpec((TB, LANES), lambda i: (i, C_AK // LANES)),
            pl.BlockSpec((TB, LANES), lambda i: (i, C_AV // LANES)),
            pl.BlockSpec((TB, W, A_KV * A_HD), tok3),
            pl.BlockSpec((TB, W, A_KV * A_HD), tok3),
            pl.BlockSpec((1, W), fix),
            pl.BlockSpec((A_HEADS, NUM_BUCKETS), fix),
            pl.BlockSpec((A_HEADS, 1), fix),
        ],
        out_specs=(
            pl.BlockSpec((TB, A_HEADS, A_HD), tok3),
            pl.BlockSpec((TB, W, A_KV * A_HD), tok3),
            pl.BlockSpec((TB, W, A_KV * A_HD), tok3),
        ),
        scratch_shapes=[pltpu.VMEM((A_HEADS, W), F32)],
        compiler_params=_cparams(("arbitrary",)),
        name="attn_step",
    )(q3, k_new, v_new, ck, cv, jnp.asarray(bucket), rel_bias.T, sinks.reshape(A_HEADS, 1))


def _reorder_w_in(w_in):
    o = 0
    parts = {}
    for name, width in (("mq", 256), ("mk", 256), ("mv", 512), ("mo", 512), ("mi", 4), ("mf", 4),
                        ("aq", 512), ("ak", 128), ("av", 128)):
        parts[name] = w_in[:, o:o + width]
        o += width
    pad = jnp.zeros((w_in.shape[0], LANES - 2 * M_HEADS), w_in.dtype)
    cols = [parts[n] for n in ("mq", "mk", "mv", "mo", "aq", "ak", "av", "mi", "mf")] + [pad]
    return jnp.concatenate(cols, axis=1).astype(BF16)


def _lane_row(v, fill=0.0):
    return jnp.concatenate([v.astype(F32), jnp.full((LANES - v.shape[0],), fill, F32)])[None, :]


def kernel(x_prompt, x_sample, state_C, state_n, state_m, cache_k, cache_v, rel_bias, norm1, w_in, b_if,
           m_gain, sinks, w_out, norm2, w_router, b_router, w_gate, b_gate, w_up, b_up, w_down, b_down,
           final_norm):
    assert norm1.shape[0] == 1, "single-layer trunk"
    batch, seq, _ = x_prompt.shape
    nsmp = x_sample.shape[0]
    n_p = batch * seq
    W = cache_k.shape[2]

    xp = x_prompt.reshape(n_p, D_MODEL)
    xs_ = x_sample.reshape(nsmp, D_MODEL)
    w_in_r = _reorder_w_in(w_in[0])
    n1 = norm1[0][None, :]
    n2 = norm2[0][None, :]
    fn = final_norm[None, :]
    bias_row = _lane_row(b_if[0])
    gain_row = m_gain[0][None, :]
    w_m = w_out[0][:M_HEADS * M_DV].astype(BF16)
    w_a = w_out[0][M_HEADS * M_DV:].astype(BF16)
    w_r32 = jnp.concatenate([w_router[0], jnp.zeros((D_MODEL, LANES - N_EXPERTS), F32)], axis=1)
    w_r_hi = w_r32.astype(BF16)
    w_r = jnp.concatenate([w_r_hi, (w_r32 - w_r_hi.astype(F32)).astype(BF16)], axis=1)
    b_r = _lane_row(b_router[0], NEG)

    zp = _inproj(xp, n1, w_in_r, TOK_TILE)
    zs = _inproj(xs_, n1, w_in_r, nsmp)
    m_out_p, p_c, p_nrep, p_mrep = _mlstm_prompt(zp, bias_row, gain_row, batch, seq)
    m_out_p = m_out_p.reshape(n_p, M_HEADS * M_DV)
    a_out_p = _attn_prompt(zp, rel_bias, sinks[0], batch, seq)
    m_out_s, s_c, s_n, s_mrep = _mlstm_step(zs, bias_row, gain_row, state_C[0], state_n[0], state_m[0])
    q3 = zs[:, C_AQ:C_AQ + A_HEADS * A_HD].reshape(nsmp, A_HEADS, A_HD)
    a3, s_k, s_v = _attn_step(q3, zs, zs, cache_k[0].reshape(nsmp, W, A_KV * A_HD),
                              cache_v[0].reshape(nsmp, W, A_KV * A_HD), rel_bias, sinks[0])
    a_out_s = a3.reshape(nsmp, A_HEADS * A_HD).astype(BF16)

    zero_counts = jnp.zeros((1, LANES), F32)
    y_p, yn_p, e_p, g_p, r_p, cnt_p = _outproj_router(xp, m_out_p, a_out_p, w_m, w_a, n2, w_r, b_r,
                                                      zero_counts, TOK_TILE)
    y_s, yn_s, e_s, g_s, r_s, cnt = _outproj_router(xs_, m_out_s, a_out_s, w_m, w_a, n2, w_r, b_r,
                                                    cnt_p, nsmp)

    counts = cnt[0, :N_EXPERTS].astype(I32)
    padded = (counts + FFN_ROWS - 1) // FFN_ROWS * FFN_ROWS
    pad_end = jnp.cumsum(padded)
    pad_start = pad_end - padded
    n_rows = ((n_p + nsmp) * TOP_K + N_EXPERTS * (FFN_ROWS - 1) + FFN_ROWS - 1) // FFN_ROWS * FFN_ROWS
    n_blocks = n_rows // FFN_ROWS
    nused = jnp.maximum(pad_end[-1] // FFN_ROWS, 1).astype(I32)
    blk_start = jnp.minimum(jnp.arange(n_blocks, dtype=I32), nused - 1) * FFN_ROWS
    block_e = jnp.minimum(jnp.sum((blk_start[:, None] >= pad_end[None, :]).astype(I32), axis=1), N_EXPERTS - 1)
    tail = jnp.where(counts % FFN_ROWS != 0, pad_end - FFN_ROWS, -1).astype(I32)
    def slots(e, r):
        hot = e[:, :TOP_K, None] == jnp.arange(N_EXPERTS, dtype=I32)
        return (jnp.sum(jnp.where(hot, pad_start, 0), axis=-1) + r[:, :TOP_K]).reshape(-1).astype(I32)

    dest_p = slots(e_p, r_p)
    dest_s = slots(e_s, r_s)

    xs_rows = _dispatch(dest_p, tail, yn_p, None, n_rows, DISPATCH_TILE)
    xs_rows = _dispatch(dest_s, tail, yn_s, xs_rows, n_rows, nsmp)
    ffn_out = _ffn(block_e, nused.reshape(1), xs_rows,
                   w_gate[0], b_gate[0][:, None, :], w_up[0], b_up[0][:, None, :],
                   w_down[0], b_down[0][:, None, :])
    out_p = _combine(dest_p, y_p, g_p, fn, ffn_out, COMBINE_TILE)
    out_s = _combine(dest_s, y_s, g_s, fn, ffn_out, COMBINE_TILE)

    kv_shape = (1, batch, WINDOW, A_KV, A_HD)
    zk = zp[:, C_AK:C_AK + A_KV * A_HD].reshape(batch, seq, A_KV * A_HD)[:, seq - WINDOW:]
    zv = zp[:, C_AV:C_AV + A_KV * A_HD].reshape(batch, seq, A_KV * A_HD)[:, seq - WINDOW:]
    return (
        out_p.reshape(batch, seq, D_MODEL),
        out_s.reshape(nsmp, 1, D_MODEL),
        p_c[None],
        p_nrep[None, :, :, :, 0],
        p_mrep[None, :, :M_HEADS, 0],
        zk.reshape(kv_shape),
        zv.reshape(kv_shape),
        s_c[None],
        s_n.reshape(1, nsmp, M_HEADS, M_DK),
        s_mrep[None, :, :M_HEADS],
        s_k.reshape(1, nsmp, W, A_KV, A_HD),
        s_v.reshape(1, nsmp, W, A_KV, A_HD),
    )
```

```python
import functools
import math

import numpy as np
import jax
import jax.numpy as jnp
from jax import lax
from jax.experimental import pallas as pl
from jax.experimental.pallas import tpu as pltpu

F32 = jnp.float32
BF16 = jnp.bfloat16
I32 = jnp.int32

D_MODEL = 1024
M_HEADS = 4
M_DK = 64
M_DV = 128
A_HEADS = 8
A_KV = 2
A_GROUP = A_HEADS // A_KV
A_HD = 64
WINDOW = 128
NUM_BUCKETS = 32
MAX_DISTANCE = 128
N_EXPERTS = 32
TOP_K = 4
D_FF = 1024
SWIGLU_LIMIT = 7.0
SWIGLU_ALPHA = 1.702
EPS = 1e-5

LANES = 128
NEG = -1e30
VMEM_LIMIT = 48 * 1024 * 1024

C_MQ, C_MK, C_MV, C_MO, C_AQ, C_AK, C_AV, C_GATE = 0, 256, 512, 1024, 1536, 2048, 2176, 2304
PROJ_W = 2432

MLSTM_CHUNK = 128
MLSTM_SEQS = 2
ATT_BLOCK = 128
TOK_TILE = 512
FFN_ROWS = 256
ROW_TILES = D_MODEL // LANES
FFN_COLS = 256
CAST_ROWS = 128
DISPATCH_TILE = 256
COMBINE_TILE = 128
DMA_UNROLL = 8
SAMPLE_MLSTM_TB = 16
SAMPLE_ATT_TB = 8


def _t5_bucket_np(dist):
    n = np.maximum(dist, 0)
    max_exact = NUM_BUCKETS // 2
    ratio = np.log(np.maximum(n, 1).astype(np.float32) / np.float32(max_exact)) / np.float32(
        math.log(MAX_DISTANCE / max_exact))
    large = max_exact + (ratio * np.float32(NUM_BUCKETS - max_exact)).astype(np.int32)
    large = np.minimum(large, NUM_BUCKETS - 1)
    return np.where(n < max_exact, n, large).astype(np.int32)


def _cparams(sem):
    return pltpu.CompilerParams(dimension_semantics=sem, vmem_limit_bytes=VMEM_LIMIT)


def _rms(x, g):
    return x * lax.rsqrt(jnp.mean(x * x, axis=-1, keepdims=True) + EPS) * g


def _log_sigmoid(x):
    return jnp.minimum(x, 0.0) - jnp.log(1.0 + jnp.exp(-jnp.abs(x)))


def _sigmoid(x):
    return 1.0 / (1.0 + jnp.exp(-x))


def _inproj_kernel(x_ref, g_ref, w_ref, z_ref):
    xn = _rms(x_ref[...], g_ref[...]).astype(BF16)
    z_ref[...] = jnp.dot(xn, w_ref[...], preferred_element_type=F32)


def _inproj(x2, norm_row, w_bf16, tile):
    n = x2.shape[0]
    return pl.pallas_call(
        _inproj_kernel,
        out_shape=jax.ShapeDtypeStruct((n, PROJ_W), F32),
        grid=(n // tile,),
        in_specs=[
            pl.BlockSpec((tile, D_MODEL), lambda i: (i, 0)),
            pl.BlockSpec((1, D_MODEL), lambda i: (0, 0)),
            pl.BlockSpec((D_MODEL, PROJ_W), lambda i: (0, 0)),
        ],
        out_specs=pl.BlockSpec((tile, PROJ_W), lambda i: (i, 0)),
        compiler_params=_cparams(("arbitrary",)),
        name="inproj",
    )(x2, norm_row, w_bf16)


def _mlstm_prompt_kernel(q_ref, k_ref, v_ref, o_ref, gt_ref, bias_ref, gain_ref,
                         out_ref, c_ref, n_ref, m_ref, s_scr, m_scr):
    L = MLSTM_CHUNK
    c = pl.program_id(1)

    @pl.when(c == 0)
    def _():
        s_scr[...] = jnp.zeros_like(s_scr)
        m_scr[...] = jnp.zeros_like(m_scr)

    row = lax.broadcasted_iota(I32, (L, L), 0)
    col = lax.broadcasted_iota(I32, (L, L), 1)
    causal = col <= row
    tril = causal.astype(F32)
    ones = jnp.ones((L, M_DV), BF16)

    for nb in range(MLSTM_SEQS):
        gb = gt_ref[nb] + bias_ref[...]
        ls = _log_sigmoid(gb)
        bcum = jnp.dot(tril, ls, preferred_element_type=F32, precision=lax.Precision.HIGHEST)
        gb_t = gb.T
        bcum_t = bcum.T
        k_t = (k_ref[nb] * (M_DK ** -0.5)).T
        for h in range(M_HEADS):
            sh = nb * M_HEADS + h
            b_col = bcum[:, M_HEADS + h:M_HEADS + h + 1]
            b_row = bcum_t[M_HEADS + h:M_HEADS + h + 1, :]
            i_row = gb_t[h:h + 1, :]
            m_prev = m_scr[nb, h:h + 1, 0:1]
            dmat = jnp.where(causal, b_col + (i_row - b_row), NEG)
            a_col = b_col + m_prev
            mt = jnp.maximum(a_col, jnp.max(dmat, axis=1, keepdims=True))
            w_intra = jnp.exp(dmat - mt)
            w_inter = jnp.exp(a_col - mt)
            q_h = q_ref[nb, :, h * M_DK:(h + 1) * M_DK].astype(BF16)
            kt_h = k_t[h * M_DK:(h + 1) * M_DK, :]
            qk = jnp.dot(q_h, kt_h.astype(BF16), preferred_element_type=F32)
            s_w = (qk * w_intra).astype(BF16)
            v_ext = jnp.concatenate([v_ref[nb, :, h * M_DV:(h + 1) * M_DV].astype(BF16), ones], axis=1)
            state = s_scr[sh]
            inter = jnp.dot(q_h, state.astype(BF16), preferred_element_type=F32)
            tot = w_inter * inter + jnp.dot(s_w, v_ext, preferred_element_type=F32)
            num = tot[:, :M_DV]
            qn = tot[:, M_DV:]
            den = jnp.maximum(jnp.abs(qn), jnp.exp(-mt))
            hh = num / den
            hn = hh * lax.rsqrt(jnp.mean(hh * hh, axis=-1, keepdims=True) + EPS)
            hn = hn * gain_ref[:, h * M_DV:(h + 1) * M_DV]
            out = hn * _sigmoid(o_ref[nb, :, h * M_DV:(h + 1) * M_DV])
            out_ref[nb, :, h * M_DV:(h + 1) * M_DV] = out.astype(out_ref.dtype)
            b_last = b_col[L - 1:L, :]
            m_new = mt[L - 1:L, :]
            g_prev = jnp.exp(b_last + m_prev - m_new)
            g_row = jnp.exp(b_last - b_row + i_row - m_new)
            kg_t = (kt_h * g_row).astype(BF16)
            s_scr[sh] = g_prev * state + jnp.dot(kg_t, v_ext, preferred_element_type=F32)
            m_scr[nb, h:h + 1, :] = jnp.broadcast_to(m_new, (1, LANES))

    @pl.when(c == pl.num_programs(1) - 1)
    def _():
        for nb in range(MLSTM_SEQS):
            for h in range(M_HEADS):
                st = s_scr[nb * M_HEADS + h]
                c_ref[nb, h] = st[:, :M_DV]
                n_ref[nb, h] = st[:, M_DV:]
        m_ref[...] = m_scr[...]


def _mlstm_prompt(z, bias_row, gain_row, batch, seq):
    L = MLSTM_CHUNK
    S = MLSTM_SEQS
    z3 = z.reshape(batch, seq, PROJ_W)
    return pl.pallas_call(
        _mlstm_prompt_kernel,
        out_shape=(
            jax.ShapeDtypeStruct((batch, seq, M_HEADS * M_DV), BF16),
            jax.ShapeDtypeStruct((batch, M_HEADS, M_DK, M_DV), F32),
            jax.ShapeDtypeStruct((batch, M_HEADS, M_DK, M_DV), F32),
            jax.ShapeDtypeStruct((batch, 8, LANES), F32),
        ),
        grid=(batch // S, seq // L),
        in_specs=[
            pl.BlockSpec((S, L, 256), lambda b, c: (b, c, C_MQ // 256)),
            pl.BlockSpec((S, L, 256), lambda b, c: (b, c, C_MK // 256)),
            pl.BlockSpec((S, L, 512), lambda b, c: (b, c, C_MV // 512)),
            pl.BlockSpec((S, L, 512), lambda b, c: (b, c, C_MO // 512)),
            pl.BlockSpec((S, L, LANES), lambda b, c: (b, c, C_GATE // LANES)),
            pl.BlockSpec((1, LANES), lambda b, c: (0, 0)),
            pl.BlockSpec((1, M_HEADS * M_DV), lambda b, c: (0, 0)),
        ],
        out_specs=(
            pl.BlockSpec((S, L, M_HEADS * M_DV), lambda b, c: (b, c, 0)),
            pl.BlockSpec((S, M_HEADS, M_DK, M_DV), lambda b, c: (b, 0, 0, 0)),
            pl.BlockSpec((S, M_HEADS, M_DK, M_DV), lambda b, c: (b, 0, 0, 0)),
            pl.BlockSpec((S, 8, LANES), lambda b, c: (b, 0, 0)),
        ),
        scratch_shapes=[pltpu.VMEM((S * M_HEADS, M_DK, 2 * M_DV), F32), pltpu.VMEM((S, 8, LANES), F32)],
        compiler_params=_cparams(("arbitrary", "arbitrary")),
        name="mlstm_prompt",
    )(z3, z3, z3, z3, z3, bias_row, gain_row)


def _attn_prompt_kernel(relb_ref, sink_ref, q_ref, kp_ref, kc_ref, vp_ref, vc_ref, bucket_ref,
                        out_ref, bias_scr):
    B = ATT_BLOCK
    j = pl.program_id(1)

    @pl.when((pl.program_id(0) == 0) & (j == 0))
    def _():
        bucket = bucket_ref[...]
        for h in range(A_HEADS):
            acc = jnp.full((B, 2 * B), NEG, F32)
            for bk in range(NUM_BUCKETS):
                acc = jnp.where(bucket == bk, relb_ref[bk * A_HEADS + h], acc)
            bias_scr[h] = acc

    scale = A_HD ** -0.5
    s_iota = lax.broadcasted_iota(I32, (B, 2 * B), 1)
    first = jnp.where((s_iota < B) & (j == 0), NEG, 0.0)
    outs = []
    for h in range(A_HEADS):
        g = h // A_GROUP
        q_h = q_ref[:, h * A_HD:(h + 1) * A_HD].astype(BF16)
        k2 = jnp.concatenate([kp_ref[:, g * A_HD:(g + 1) * A_HD], kc_ref[:, g * A_HD:(g + 1) * A_HD]],
                             axis=0).astype(BF16)
        v2 = jnp.concatenate([vp_ref[:, g * A_HD:(g + 1) * A_HD], vc_ref[:, g * A_HD:(g + 1) * A_HD]],
                             axis=0).astype(BF16)
        logits = lax.dot_general(q_h, k2, (((1,), (1,)), ((), ())), preferred_element_type=F32)
        logits = logits * scale + bias_scr[h] + first
        sink = sink_ref[h]
        m = jnp.maximum(jnp.max(logits, axis=-1, keepdims=True), sink)
        p = jnp.exp(logits - m)
        den = jnp.sum(p, axis=-1, keepdims=True) + jnp.exp(sink - m)
        o = jnp.dot(p.astype(BF16), v2, preferred_element_type=F32) / den
        outs.append(o)
    out_ref[...] = jnp.concatenate(outs, axis=1).astype(out_ref.dtype)


def _attn_prompt(z, rel_bias, sinks, batch, seq):
    B = ATT_BLOCK
    nb = seq // B
    qi = np.arange(B)[:, None]
    si = np.arange(2 * B)[None, :]
    dist = qi + B - si
    bucket = np.where((dist >= 0) & (dist <= WINDOW), _t5_bucket_np(dist), -1).astype(np.int32)
    cur = lambda b, j, *_: b * nb + j
    prev = lambda b, j, *_: b * nb + jnp.maximum(j - 1, 0)
    grid_spec = pltpu.PrefetchScalarGridSpec(
        num_scalar_prefetch=2,
        grid=(batch, nb),
        in_specs=[
            pl.BlockSpec((B, 512), lambda b, j, *_: (cur(b, j), C_AQ // 512)),
            pl.BlockSpec((B, LANES), lambda b, j, *_: (prev(b, j), C_AK // LANES)),
            pl.BlockSpec((B, LANES), lambda b, j, *_: (cur(b, j), C_AK // LANES)),
            pl.BlockSpec((B, LANES), lambda b, j, *_: (prev(b, j), C_AV // LANES)),
            pl.BlockSpec((B, LANES), lambda b, j, *_: (cur(b, j), C_AV // LANES)),
            pl.BlockSpec((B, 2 * B), lambda b, j, *_: (0, 0)),
        ],
        out_specs=pl.BlockSpec((B, A_HEADS * A_HD), lambda b, j, *_: (cur(b, j), 0)),
        scratch_shapes=[pltpu.VMEM((A_HEADS, B, 2 * B), F32)],
    )
    return pl.pallas_call(
        _attn_prompt_kernel,
        out_shape=jax.ShapeDtypeStruct((batch * seq, A_HEADS * A_HD), BF16),
        grid_spec=grid_spec,
        compiler_params=_cparams(("arbitrary", "arbitrary")),
        name="attn_prompt",
    )(rel_bias.reshape(-1), sinks, z, z, z, z, z, jnp.asarray(bucket))


def _outproj_router_kernel(x_ref, mo_ref, ao_ref, wm_ref, wa_ref, g_ref, wr_ref, br_ref, cin_ref,
                           y_ref, yn_ref, eidx_ref, gate_ref, rank_ref, cout_ref, carry):
    T = x_ref.shape[0]
    i = pl.program_id(0)

    @pl.when(i == 0)
    def _():
        carry[...] = cin_ref[...]

    y = (x_ref[...] + jnp.dot(mo_ref[...], wm_ref[...], preferred_element_type=F32)
         + jnp.dot(ao_ref[...], wa_ref[...], preferred_element_type=F32))
    y_ref[...] = y
    yn = _rms(y, g_ref[...])
    for s in range(ROW_TILES):
        yn_ref[pl.ds(s, T, stride=ROW_TILES), :] = yn[:, s * LANES:(s + 1) * LANES]
    yh = yn.astype(BF16)
    yl = (yn - yh.astype(F32)).astype(BF16)
    hh = jnp.dot(yh, wr_ref[...], preferred_element_type=F32)
    lh = jnp.dot(yl, wr_ref[:, :LANES], preferred_element_type=F32)
    logits = hh[:, :LANES] + (hh[:, LANES:] + lh) + br_ref[...]
    lane = lax.broadcasted_iota(I32, (T, LANES), 1)
    lane_f = lane.astype(F32)
    vals, idxs, hots = [], [], []
    l = logits
    for _ in range(TOP_K):
        mx = jnp.max(l, axis=-1, keepdims=True)
        idx = jnp.min(jnp.where(l == mx, lane_f, float(LANES)), axis=-1, keepdims=True)
        hot = lane_f == idx
        l = jnp.where(hot, -jnp.inf, l)
        vals.append(mx)
        idxs.append(idx)
        hots.append(hot)
    es = [jnp.exp(v - vals[0]) for v in vals]
    tot = es[0] + es[1] + es[2] + es[3]
    sel = jnp.where(hots[0] | hots[1] | hots[2] | hots[3], 1.0, 0.0)
    row = lax.broadcasted_iota(I32, (T, T), 0)
    col = lax.broadcasted_iota(I32, (T, T), 1)
    strict = (col < row).astype(BF16)
    before = carry[...] + jnp.dot(strict, sel.astype(BF16), preferred_element_type=F32)
    eidx = jnp.zeros((T, LANES), I32)
    gate = jnp.zeros((T, LANES), F32)
    rank = jnp.zeros((T, LANES), I32)
    for k in range(TOP_K):
        r_k = jnp.sum(jnp.where(hots[k], before, 0.0), axis=-1, keepdims=True)
        eidx = jnp.where(lane == k, idxs[k].astype(I32), eidx)
        gate = jnp.where(lane == k, es[k] / tot, gate)
        rank = jnp.where(lane == k, r_k.astype(I32), rank)
    eidx_ref[...] = eidx
    gate_ref[...] = gate
    rank_ref[...] = rank
    carry[...] = carry[...] + jnp.sum(sel, axis=0, keepdims=True)
    cout_ref[...] = carry[...]


def _outproj_router(x2, m_out, a_out, w_m, w_a, norm_row, w_r, b_r, counts_in, tile):
    n = x2.shape[0]
    tok = lambda i: (i, 0)
    fix = lambda i: (0, 0)
    return pl.pallas_call(
        _outproj_router_kernel,
        out_shape=(
            jax.ShapeDtypeStruct((n, D_MODEL), F32),
            jax.ShapeDtypeStruct((n * ROW_TILES, LANES), F32),
            jax.ShapeDtypeStruct((n, LANES), I32),
            jax.ShapeDtypeStruct((n, LANES), F32),
            jax.ShapeDtypeStruct((n, LANES), I32),
            jax.ShapeDtypeStruct((1, LANES), F32),
        ),
        grid=(n // tile,),
        in_specs=[
            pl.BlockSpec((tile, D_MODEL), tok),
            pl.BlockSpec((tile, 512), tok),
            pl.BlockSpec((tile, 512), tok),
            pl.BlockSpec((512, D_MODEL), fix),
            pl.BlockSpec((512, D_MODEL), fix),
            pl.BlockSpec((1, D_MODEL), fix),
            pl.BlockSpec((D_MODEL, 2 * LANES), fix),
            pl.BlockSpec((1, LANES), fix),
            pl.BlockSpec((1, LANES), fix),
        ],
        out_specs=(
            pl.BlockSpec((tile, D_MODEL), tok),
            pl.BlockSpec((tile * ROW_TILES, LANES), tok),
            pl.BlockSpec((tile, LANES), tok),
            pl.BlockSpec((tile, LANES), tok),
            pl.BlockSpec((tile, LANES), tok),
            pl.BlockSpec((1, LANES), fix),
        ),
        scratch_shapes=[pltpu.VMEM((1, LANES), F32)],
        compiler_params=_cparams(("arbitrary",)),
        name="outproj_router",
    )(x2, m_out, a_out, w_m, w_a, norm_row, w_r, b_r, counts_in)


def _dispatch_kernel(dest_ref, tail_ref, yn_ref, *rest, first):
    xs_ref, zero_scr, sem = rest if first else rest[1:]
    T = yn_ref.shape[0]
    i = pl.program_id(0)

    if first:
        @pl.when(i == 0)
        def _():
            zero_scr[...] = jnp.zeros_like(zero_scr)

            def fill(e, carry):
                @pl.when(tail_ref[e] >= 0)
                def _():
                    pltpu.make_async_copy(zero_scr, xs_ref.at[pl.ds(tail_ref[e], FFN_ROWS)], sem).start()
                return carry

            def drain(e, carry):
                @pl.when(tail_ref[e] >= 0)
                def _():
                    pltpu.make_async_copy(zero_scr, xs_ref.at[pl.ds(tail_ref[e], FFN_ROWS)], sem).wait()
                return carry

            lax.fori_loop(0, N_EXPERTS, fill, 0)
            lax.fori_loop(0, N_EXPERTS, drain, 0)

    base = i * (T * TOP_K)

    def issue(tb, carry):
        for u in range(DMA_UNROLL):
            t = tb * DMA_UNROLL + u
            for k in range(TOP_K):
                d = dest_ref[base + t * TOP_K + k]
                pltpu.make_async_copy(yn_ref.at[t], xs_ref.at[d], sem).start()
        return carry

    lax.fori_loop(0, T // DMA_UNROLL, issue, 0)
    for k in range(TOP_K):
        pltpu.make_async_copy(yn_ref, xs_ref.at[pl.ds(0, T)], sem).wait()


def _dispatch(dest_flat, tail, yn, xs, n_rows, tile):
    n = yn.shape[0]
    first = xs is None
    in_specs = [pl.BlockSpec((tile, ROW_TILES, LANES), lambda i, *_: (i, 0, 0))]
    operands = [dest_flat, tail, yn]
    if not first:
        in_specs.append(pl.BlockSpec(memory_space=pl.ANY))
        operands.append(xs)
    grid_spec = pltpu.PrefetchScalarGridSpec(
        num_scalar_prefetch=2,
        grid=(n // tile,),
        in_specs=in_specs,
        out_specs=pl.BlockSpec(memory_space=pl.ANY),
        scratch_shapes=[pltpu.VMEM((FFN_ROWS, ROW_TILES, LANES), F32), pltpu.SemaphoreType.DMA],
    )
    return pl.pallas_call(
        functools.partial(_dispatch_kernel, first=first),
        out_shape=jax.ShapeDtypeStruct((n_rows, ROW_TILES, LANES), F32),
        grid_spec=grid_spec,
        input_output_aliases={} if first else {3: 0},
        compiler_params=_cparams(("arbitrary",)),
        name="moe_dispatch_first" if first else "moe_dispatch_more",
    )(*operands)


def _ffn_kernel(be_ref, nused_ref, xs_ref, wg_ref, bg_ref, wu_ref, bu_ref, wd_ref, bd_ref, out_ref,
                wg_bf, wu_bf, wd_bf, h_scr):
    i = pl.program_id(0)

    @pl.when(i < nused_ref[0])
    def _():
        @pl.when((i == 0) | (be_ref[i] != be_ref[jnp.maximum(i - 1, 0)]))
        def _():
            for src, dst in ((wg_ref, wg_bf), (wu_ref, wu_bf), (wd_ref, wd_bf)):
                for r in range(0, src.shape[0], CAST_ROWS):
                    dst[r:r + CAST_ROWS, :] = src[r:r + CAST_ROWS, :].astype(BF16)

        x = jnp.concatenate([xs_ref[pl.ds(s, FFN_ROWS, stride=ROW_TILES), :].astype(BF16)
                             for s in range(ROW_TILES)], axis=1)
        for c in range(D_FF // FFN_COLS):
            cs = slice(c * FFN_COLS, (c + 1) * FFN_COLS)
            g = jnp.dot(x, wg_bf[:, cs], preferred_element_type=F32) + bg_ref[:, cs]
            u = jnp.dot(x, wu_bf[:, cs], preferred_element_type=F32) + bu_ref[:, cs]
            g = jnp.minimum(g, SWIGLU_LIMIT)
            u = jnp.clip(u, -SWIGLU_LIMIT, SWIGLU_LIMIT)
            h_scr[:, cs] = ((u + 1.0) * (g * _sigmoid(SWIGLU_ALPHA * g))).astype(BF16)
        out = jnp.dot(h_scr[...], wd_bf[...], preferred_element_type=F32) + bd_ref[...]
        for s in range(ROW_TILES):
            out_ref[pl.ds(s, FFN_ROWS, stride=ROW_TILES), :] = out[:, s * LANES:(s + 1) * LANES]


def _ffn(block_e, nused, xs, wg, bg, wu, bu, wd, bd):
    nb = xs.shape[0] // (FFN_ROWS * ROW_TILES)
    blk = lambda i, be, nu: (jnp.minimum(i, nu[0] - 1), 0)
    wsel = lambda i, be, nu: (be[i], 0, 0)
    grid_spec = pltpu.PrefetchScalarGridSpec(
        num_scalar_prefetch=2,
        grid=(nb,),
        in_specs=[
            pl.BlockSpec((FFN_ROWS * ROW_TILES, LANES), blk),
            pl.BlockSpec((None, D_MODEL, D_FF), wsel),
            pl.BlockSpec((None, 1, D_FF), wsel),
            pl.BlockSpec((None, D_MODEL, D_FF), wsel),
            pl.BlockSpec((None, 1, D_FF), wsel),
            pl.BlockSpec((None, D_FF, D_MODEL), wsel),
            pl.BlockSpec((None, 1, D_MODEL), wsel),
        ],
        out_specs=pl.BlockSpec((FFN_ROWS * ROW_TILES, LANES), blk),
        scratch_shapes=[pltpu.VMEM((D_MODEL, D_FF), BF16), pltpu.VMEM((D_MODEL, D_FF), BF16),
                        pltpu.VMEM((D_FF, D_MODEL), BF16), pltpu.VMEM((FFN_ROWS, D_FF), BF16)],
    )
    return pl.pallas_call(
        _ffn_kernel,
        out_shape=jax.ShapeDtypeStruct(xs.shape, F32),
        grid_spec=grid_spec,
        compiler_params=_cparams(("arbitrary",)),
        name="moe_ffn",
    )(block_e, nused, xs, wg, bg, wu, bu, wd, bd)


def _combine_kernel(dest_ref, y_ref, gate_ref, fn_ref, ffn_ref, out_ref, buf, sem):
    T = y_ref.shape[0]
    i = pl.program_id(0)
    n = pl.num_programs(0)
    slot = i % 2

    def issue(tile, s):
        base = tile * (T * TOP_K)

        def body(tb, carry):
            for u in range(DMA_UNROLL):
                t = tb * DMA_UNROLL + u
                for k in range(TOP_K):
                    d = dest_ref[base + t * TOP_K + k]
                    pltpu.make_async_copy(ffn_ref.at[d], buf.at[s, k, pl.ds(t * ROW_TILES, ROW_TILES)],
                                          sem.at[s]).start()
            return carry

        lax.fori_loop(0, T // DMA_UNROLL, body, 0)

    @pl.when(i == 0)
    def _():
        issue(0, 0)

    @pl.when(i + 1 < n)
    def _():
        issue(i + 1, 1 - slot)

    for k in range(TOP_K):
        pltpu.make_async_copy(buf.at[1 - slot, k], buf.at[slot, k], sem.at[slot]).wait()

    acc = y_ref[...]
    gate = gate_ref[...]
    for k in range(TOP_K):
        rows = jnp.concatenate([buf[slot, k, pl.ds(s, T, stride=ROW_TILES), :] for s in range(ROW_TILES)],
                               axis=1)
        acc = acc + gate[:, k:k + 1] * rows
    out_ref[...] = _rms(acc, fn_ref[...])


def _combine(dest_flat, y, gate, fnorm_row, ffn_out, tile):
    n = y.shape[0]
    grid_spec = pltpu.PrefetchScalarGridSpec(
        num_scalar_prefetch=1,
        grid=(n // tile,),
        in_specs=[
            pl.BlockSpec((tile, D_MODEL), lambda i, *_: (i, 0)),
            pl.BlockSpec((tile, LANES), lambda i, *_: (i, 0)),
            pl.BlockSpec((1, D_MODEL), lambda i, *_: (0, 0)),
            pl.BlockSpec(memory_space=pl.ANY),
        ],
        out_specs=pl.BlockSpec((tile, D_MODEL), lambda i, *_: (i, 0)),
        scratch_shapes=[pltpu.VMEM((2, TOP_K, tile * ROW_TILES, LANES), F32), pltpu.SemaphoreType.DMA((2,))],
    )
    return pl.pallas_call(
        _combine_kernel,
        out_shape=jax.ShapeDtypeStruct((n, D_MODEL), F32),
        grid_spec=grid_spec,
        compiler_params=_cparams(("arbitrary",)),
        name="moe_combine",
    )(dest_flat, y, gate, fnorm_row, ffn_out)


def _mlstm_step_kernel(q_ref, k_ref, v_ref, o_ref, gt_ref, bias_ref, gain_ref, c0_ref, n0_ref, m0_ref,
                       out_ref, c_ref, n_ref, m_ref):
    TB = SAMPLE_MLSTM_TB
    gb = gt_ref[...] + bias_ref[...]
    ls = _log_sigmoid(gb)
    lane = lax.broadcasted_iota(I32, (TB, LANES), 1)
    eye = (lax.broadcasted_iota(I32, (M_DK, M_DK), 0) == lax.broadcasted_iota(I32, (M_DK, M_DK), 1)).astype(F32)
    nt = (((1,), (1,)), ((), ()))
    m_all = jnp.zeros((TB, LANES), F32)
    for h in range(M_HEADS):
        i_pre = gb[:, h:h + 1]
        a = ls[:, M_HEADS + h:M_HEADS + h + 1] + m0_ref[:, h:h + 1]
        mt = jnp.maximum(a, i_pre)
        w_intra = jnp.exp(i_pre - mt)
        w_inter = jnp.exp(a - mt)
        q_h = q_ref[:, h * M_DK:(h + 1) * M_DK]
        k_h = k_ref[:, h * M_DK:(h + 1) * M_DK] * (M_DK ** -0.5)
        v_h = v_ref[:, h * M_DV:(h + 1) * M_DV]
        n0_h = n0_ref[:, h, :]
        s = jnp.sum(q_h * k_h, axis=-1, keepdims=True) * w_intra
        qn = w_inter * jnp.sum(q_h * n0_h, axis=-1, keepdims=True) + s
        den = jnp.maximum(jnp.abs(qn), jnp.exp(-mt))
        q_t = lax.dot_general(eye, q_h, nt, preferred_element_type=F32, precision=lax.Precision.HIGHEST)
        k_t = lax.dot_general(eye, k_h, nt, preferred_element_type=F32, precision=lax.Precision.HIGHEST)
        rows = []
        for b in range(TB):
            c0 = c0_ref[b, h]
            qc = jnp.sum(c0 * q_t[:, b:b + 1], axis=0, keepdims=True)
            v_b = v_h[b:b + 1, :]
            rows.append(w_inter[b:b + 1, :] * qc + s[b:b + 1, :] * v_b)
            c_ref[b, h] = w_inter[b:b + 1, :] * c0 + (w_intra[b:b + 1, :] * k_t[:, b:b + 1]) * v_b
        num = jnp.concatenate(rows, axis=0)
        hh = num / den
        hn = hh * lax.rsqrt(jnp.mean(hh * hh, axis=-1, keepdims=True) + EPS)
        hn = hn * gain_ref[:, h * M_DV:(h + 1) * M_DV]
        out_ref[:, h * M_DV:(h + 1) * M_DV] = (hn * _sigmoid(o_ref[:, h * M_DV:(h + 1) * M_DV])).astype(out_ref.dtype)
        n_ref[:, h * M_DK:(h + 1) * M_DK] = w_inter * n0_h + w_intra * k_h
        m_all = jnp.where(lane == h, mt, m_all)
    m_ref[...] = m_all


def _mlstm_step(zs, bias_row, gain_row, c0, n0, m0):
    TB = SAMPLE_MLSTM_TB
    nb = zs.shape[0]
    tok = lambda i: (i, 0)
    return pl.pallas_call(
        _mlstm_step_kernel,
        out_shape=(
            jax.ShapeDtypeStruct((nb, M_HEADS * M_DV), BF16),
            jax.ShapeDtypeStruct((nb, M_HEADS, M_DK, M_DV), F32),
            jax.ShapeDtypeStruct((nb, M_HEADS * M_DK), F32),
            jax.ShapeDtypeStruct((nb, LANES), F32),
        ),
        grid=(nb // TB,),
        in_specs=[
            pl.BlockSpec((TB, 256), lambda i: (i, C_MQ // 256)),
            pl.BlockSpec((TB, 256), lambda i: (i, C_MK // 256)),
            pl.BlockSpec((TB, 512), lambda i: (i, C_MV // 512)),
            pl.BlockSpec((TB, 512), lambda i: (i, C_MO // 512)),
            pl.BlockSpec((TB, LANES), lambda i: (i, C_GATE // LANES)),
            pl.BlockSpec((1, LANES), lambda i: (0, 0)),
            pl.BlockSpec((1, M_HEADS * M_DV), lambda i: (0, 0)),
            pl.BlockSpec((TB, M_HEADS, M_DK, M_DV), lambda i: (i, 0, 0, 0)),
            pl.BlockSpec((TB, M_HEADS, M_DK), lambda i: (i, 0, 0)),
            pl.BlockSpec((TB, M_HEADS), tok),
        ],
        out_specs=(
            pl.BlockSpec((TB, M_HEADS * M_DV), tok),
            pl.BlockSpec((TB, M_HEADS, M_DK, M_DV), lambda i: (i, 0, 0, 0)),
            pl.BlockSpec((TB, M_HEADS * M_DK), tok),
            pl.BlockSpec((TB, LANES), tok),
        ),
        compiler_params=_cparams(("arbitrary",)),
        name="mlstm_step",
    )(zs, zs, zs, zs, zs, bias_row, gain_row, c0, n0, m0)


def _attn_step_kernel(q_ref, kn_ref, vn_ref, ck_ref, cv_ref, bucket_ref, relt_ref, sink_ref,
                      out_ref, nk_ref, nv_ref, bias_scr):
    TB = SAMPLE_ATT_TB
    W = ck_ref.shape[1]

    @pl.when(pl.program_id(0) == 0)
    def _():
        bucket = jnp.broadcast_to(bucket_ref[...], (A_HEADS, W))
        acc = jnp.zeros((A_HEADS, W), F32)
        for bk in range(NUM_BUCKETS):
            acc = jnp.where(bucket == bk, relt_ref[:, bk:bk + 1], acc)
        bias_scr[...] = acc

    scale = A_HD ** -0.5
    nt = (((1,), (1,)), ((), ()))
    bias = bias_scr[...]
    bias_new = relt_ref[:, 0:1]
    sink = sink_ref[...]
    low = lax.broadcasted_iota(I32, (A_HEADS, 1), 0) < A_GROUP
    for b in range(TB):
        q = q_ref[b]
        qb = q.astype(BF16)
        kc = ck_ref[b]
        vc = cv_ref[b]
        kn = kn_ref[b:b + 1, :]
        vn = vn_ref[b:b + 1, :]
        l0 = lax.dot_general(qb, kc[:, :A_HD].astype(BF16), nt, preferred_element_type=F32)
        l1 = lax.dot_general(qb, kc[:, A_HD:].astype(BF16), nt, preferred_element_type=F32)
        logits = jnp.where(low, l0, l1) * scale + bias
        kn_h = jnp.where(low, kn[:, :A_HD], kn[:, A_HD:])
        vn_h = jnp.where(low, vn[:, :A_HD], vn[:, A_HD:])
        l_new = jnp.sum(q * kn_h, axis=-1, keepdims=True) * scale + bias_new
        m = jnp.maximum(jnp.maximum(jnp.max(logits, axis=-1, keepdims=True), l_new), sink)
        p = jnp.exp(logits - m)
        p_new = jnp.exp(l_new - m)
        den = jnp.sum(p, axis=-1, keepdims=True) + p_new + jnp.exp(sink - m)
        pb = p.astype(BF16)
        o0 = jnp.dot(pb, vc[:, :A_HD].astype(BF16), preferred_element_type=F32)
        o1 = jnp.dot(pb, vc[:, A_HD:].astype(BF16), preferred_element_type=F32)
        o = jnp.where(low, o0, o1) + p_new * vn_h
        out_ref[b] = o / den
        nk_ref[b, 0:W - 1, :] = ck_ref[b, 1:W, :]
        nk_ref[b, W - 1:W, :] = kn
        nv_ref[b, 0:W - 1, :] = cv_ref[b, 1:W, :]
        nv_ref[b, W - 1:W, :] = vn


def _attn_step(q3, k_new, v_new, ck, cv, rel_bias, sinks):
    TB = SAMPLE_ATT_TB
    nb, W = ck.shape[0], ck.shape[1]
    bucket = _t5_bucket_np(W - np.arange(W))[None, :].astype(np.int32)
    tok = lambda i: (i, 0)
    tok3 = lambda i: (i, 0, 0)
    fix = lambda i: (0, 0)
    return pl.pallas_call(
        _attn_step_kernel,
        out_shape=(
            jax.ShapeDtypeStruct((nb, A_HEADS, A_HD), F32),
            jax.ShapeDtypeStruct(ck.shape, F32),
            jax.ShapeDtypeStruct(cv.shape, F32),
        ),
        grid=(nb // TB,),
        in_specs=[
            pl.BlockSpec((TB, A_HEADS, A_HD), tok3),
            pl.BlockSpec((TB, LANES), lambda i: (i, C_AK // LANES)),
            pl.BlockSpec((TB, LANES), lambda i: (i, C_AV // LANES)),
            pl.BlockSpec((TB, W, A_KV * A_HD), tok3),
            pl.BlockSpec((TB, W, A_KV * A_HD), tok3),
            pl.BlockSpec((1, W), fix),
            pl.BlockSpec((A_HEADS, NUM_BUCKETS), fix),
            pl.BlockSpec((A_HEADS, 1), fix),
        ],
        out_specs=(
            pl.BlockSpec((TB, A_HEADS, A_HD), tok3),
            pl.BlockSpec((TB, W, A_KV * A_HD), tok3),
            pl.BlockSpec((TB, W, A_KV * A_HD), tok3),
        ),
        scratch_shapes=[pltpu.VMEM((A_HEADS, W), F32)],
        compiler_params=_cparams(("arbitrary",)),
        name="attn_step",
    )(q3, k_new, v_new, ck, cv, jnp.asarray(bucket), rel_bias.T, sinks.reshape(A_HEADS, 1))


def _reorder_w_in(w_in):
    o = 0
    parts = {}
    for name, width in (("mq", 256), ("mk", 256), ("mv", 512), ("mo", 512), ("mi", 4), ("mf", 4),
                        ("aq", 512), ("ak", 128), ("av", 128)):
        parts[name] = w_in[:, o:o + width]
        o += width
    pad = jnp.zeros((w_in.shape[0], LANES - 2 * M_HEADS), w_in.dtype)
    cols = [parts[n] for n in ("mq", "mk", "mv", "mo", "aq", "ak", "av", "mi", "mf")] + [pad]
    return jnp.concatenate(cols, axis=1).astype(BF16)


def _lane_row(v, fill=0.0):
    return jnp.concatenate([v.astype(F32), jnp.full((LANES - v.shape[0],), fill, F32)])[None, :]


def kernel(x_prompt, x_sample, state_C, state_n, state_m, cache_k, cache_v, rel_bias, norm1, w_in, b_if,
           m_gain, sinks, w_out, norm2, w_router, b_router, w_gate, b_gate, w_up, b_up, w_down, b_down,
           final_norm):
    assert norm1.shape[0] == 1, "single-layer trunk"
    batch, seq, _ = x_prompt.shape
    nsmp = x_sample.shape[0]
    n_p = batch * seq
    W = cache_k.shape[2]

    xp = x_prompt.reshape(n_p, D_MODEL)
    xs_ = x_sample.reshape(nsmp, D_MODEL)
    w_in_r = _reorder_w_in(w_in[0])
    n1 = norm1[0][None, :]
    n2 = norm2[0][None, :]
    fn = final_norm[None, :]
    bias_row = _lane_row(b_if[0])
    gain_row = m_gain[0][None, :]
    w_m = w_out[0][:M_HEADS * M_DV].astype(BF16)
    w_a = w_out[0][M_HEADS * M_DV:].astype(BF16)
    w_r32 = jnp.concatenate([w_router[0], jnp.zeros((D_MODEL, LANES - N_EXPERTS), F32)], axis=1)
    w_r_hi = w_r32.astype(BF16)
    w_r = jnp.concatenate([w_r_hi, (w_r32 - w_r_hi.astype(F32)).astype(BF16)], axis=1)
    b_r = _lane_row(b_router[0], NEG)

    zp = _inproj(xp, n1, w_in_r, TOK_TILE)
    zs = _inproj(xs_, n1, w_in_r, nsmp)
    m_out_p, p_c, p_nrep, p_mrep = _mlstm_prompt(zp, bias_row, gain_row, batch, seq)
    m_out_p = m_out_p.reshape(n_p, M_HEADS * M_DV)
    a_out_p = _attn_prompt(zp, rel_bias, sinks[0], batch, seq)
    m_out_s, s_c, s_n, s_mrep = _mlstm_step(zs, bias_row, gain_row, state_C[0], state_n[0], state_m[0])
    q3 = zs[:, C_AQ:C_AQ + A_HEADS * A_HD].reshape(nsmp, A_HEADS, A_HD)
    a3, s_k, s_v = _attn_step(q3, zs, zs, cache_k[0].reshape(nsmp, W, A_KV * A_HD),
                              cache_v[0].reshape(nsmp, W, A_KV * A_HD), rel_bias, sinks[0])
    a_out_s = a3.reshape(nsmp, A_HEADS * A_HD).astype(BF16)

    zero_counts = jnp.zeros((1, LANES), F32)
    y_p, yn_p, e_p, g_p, r_p, cnt_p = _outproj_router(xp, m_out_p, a_out_p, w_m, w_a, n2, w_r, b_r,
                                                      zero_counts, TOK_TILE)
    y_s, yn_s, e_s, g_s, r_s, cnt = _outproj_router(xs_, m_out_s, a_out_s, w_m, w_a, n2, w_r, b_r,
                                                    cnt_p, nsmp)

    counts = cnt[0, :N_EXPERTS].astype(I32)
    padded = (counts + FFN_ROWS - 1) // FFN_ROWS * FFN_ROWS
    pad_end = jnp.cumsum(padded)
    pad_start = pad_end - padded
    n_rows = ((n_p + nsmp) * TOP_K + N_EXPERTS * (FFN_ROWS - 1) + FFN_ROWS - 1) // FFN_ROWS * FFN_ROWS
    n_blocks = n_rows // FFN_ROWS
    nused = jnp.maximum(pad_end[-1] // FFN_ROWS, 1).astype(I32)
    blk_start = jnp.minimum(jnp.arange(n_blocks, dtype=I32), nused - 1) * FFN_ROWS
    block_e = jnp.minimum(jnp.sum((blk_start[:, None] >= pad_end[None, :]).astype(I32), axis=1), N_EXPERTS - 1)
    tail = jnp.where(counts % FFN_ROWS != 0, pad_end - FFN_ROWS, -1).astype(I32)
    def slots(e, r):
        hot = e[:, :TOP_K, None] == jnp.arange(N_EXPERTS, dtype=I32)
        return (jnp.sum(jnp.where(hot, pad_start, 0), axis=-1) + r[:, :TOP_K]).reshape(-1).astype(I32)

    dest_p = slots(e_p, r_p)
    dest_s = slots(e_s, r_s)

    as_tiles = lambda a: a.reshape(-1, ROW_TILES, LANES)
    xs_rows = _dispatch(dest_p, tail, as_tiles(yn_p), None, n_rows, DISPATCH_TILE)
    xs_rows = _dispatch(dest_s, tail, as_tiles(yn_s), xs_rows, n_rows, nsmp)
    ffn_out = _ffn(block_e, nused.reshape(1), xs_rows.reshape(-1, LANES),
                   w_gate[0], b_gate[0][:, None, :], w_up[0], b_up[0][:, None, :],
                   w_down[0], b_down[0][:, None, :])
    out_p = _combine(dest_p, y_p, g_p, fn, as_tiles(ffn_out), COMBINE_TILE)
    out_s = _combine(dest_s, y_s, g_s, fn, as_tiles(ffn_out), COMBINE_TILE)

    kv_shape = (1, batch, WINDOW, A_KV, A_HD)
    zk = zp[:, C_AK:C_AK + A_KV * A_HD].reshape(batch, seq, A_KV * A_HD)[:, seq - WINDOW:]
    zv = zp[:, C_AV:C_AV + A_KV * A_HD].reshape(batch, seq, A_KV * A_HD)[:, seq - WINDOW:]
    return (
        out_p.reshape(batch, seq, D_MODEL),
        out_s.reshape(nsmp, 1, D_MODEL),
        p_c[None],
        p_nrep[None, :, :, :, 0],
        p_mrep[None, :, :M_HEADS, 0],
        zk.reshape(kv_shape),
        zv.reshape(kv_shape),
        s_c[None],
        s_n.reshape(1, nsmp, M_HEADS, M_DK),
        s_mrep[None, :, :M_HEADS],
        s_k.reshape(1, nsmp, W, A_KV, A_HD),
        s_v.reshape(1, nsmp, W, A_KV, A_HD),
    )
```

```python
import functools
import math

import numpy as np
import jax
import jax.numpy as jnp
from jax import lax
from jax.experimental import pallas as pl
from jax.experimental.pallas import tpu as pltpu

F32 = jnp.float32
BF16 = jnp.bfloat16
I32 = jnp.int32

D_MODEL = 1024
M_HEADS = 4
M_DK = 64
M_DV = 128
A_HEADS = 8
A_KV = 2
A_GROUP = A_HEADS // A_KV
A_HD = 64
WINDOW = 128
NUM_BUCKETS = 32
MAX_DISTANCE = 128
N_EXPERTS = 32
TOP_K = 4
D_FF = 1024
SWIGLU_LIMIT = 7.0
SWIGLU_ALPHA = 1.702
EPS = 1e-5

LANES = 128
NEG = -1e30
VMEM_LIMIT = 48 * 1024 * 1024

C_MQ, C_MK, C_MV, C_MO, C_AQ, C_AK, C_AV, C_GATE = 0, 256, 512, 1024, 1536, 2048, 2176, 2304
PROJ_W = 2432

MLSTM_CHUNK = 128
MLSTM_SEQS = 2
ATT_BLOCK = 128
TOK_TILE = 512
FFN_ROWS = 512
FFN_SUB = 256
ROW_TILES = D_MODEL // LANES
FFN_COLS = 256
CAST_ROWS = 128
DISPATCH_TILE = 256
COMBINE_TILE = 128
DMA_UNROLL = 8
SAMPLE_MLSTM_TB = 16
SAMPLE_ATT_TB = 8


def _t5_bucket_np(dist):
    n = np.maximum(dist, 0)
    max_exact = NUM_BUCKETS // 2
    ratio = np.log(np.maximum(n, 1).astype(np.float32) / np.float32(max_exact)) / np.float32(
        math.log(MAX_DISTANCE / max_exact))
    large = max_exact + (ratio * np.float32(NUM_BUCKETS - max_exact)).astype(np.int32)
    large = np.minimum(large, NUM_BUCKETS - 1)
    return np.where(n < max_exact, n, large).astype(np.int32)


def _cparams(sem):
    return pltpu.CompilerParams(dimension_semantics=sem, vmem_limit_bytes=VMEM_LIMIT)


def _rms(x, g):
    return x * lax.rsqrt(jnp.mean(x * x, axis=-1, keepdims=True) + EPS) * g


def _log_sigmoid(x):
    return jnp.minimum(x, 0.0) - jnp.log(1.0 + jnp.exp(-jnp.abs(x)))


def _sigmoid(x):
    return 1.0 / (1.0 + jnp.exp(-x))


def _inproj_kernel(x_ref, g_ref, w_ref, z_ref):
    xn = _rms(x_ref[...], g_ref[...]).astype(BF16)
    z_ref[...] = jnp.dot(xn, w_ref[...], preferred_element_type=F32)


def _inproj(x2, norm_row, w_bf16, tile):
    n = x2.shape[0]
    return pl.pallas_call(
        _inproj_kernel,
        out_shape=jax.ShapeDtypeStruct((n, PROJ_W), F32),
        grid=(n // tile,),
        in_specs=[
            pl.BlockSpec((tile, D_MODEL), lambda i: (i, 0)),
            pl.BlockSpec((1, D_MODEL), lambda i: (0, 0)),
            pl.BlockSpec((D_MODEL, PROJ_W), lambda i: (0, 0)),
        ],
        out_specs=pl.BlockSpec((tile, PROJ_W), lambda i: (i, 0)),
        compiler_params=_cparams(("arbitrary",)),
        name="inproj",
    )(x2, norm_row, w_bf16)


def _mlstm_prompt_kernel(q_ref, k_ref, v_ref, o_ref, gt_ref, bias_ref, gain_ref,
                         out_ref, c_ref, n_ref, m_ref, s_scr, m_scr):
    L = MLSTM_CHUNK
    c = pl.program_id(1)

    @pl.when(c == 0)
    def _():
        s_scr[...] = jnp.zeros_like(s_scr)
        m_scr[...] = jnp.zeros_like(m_scr)

    row = lax.broadcasted_iota(I32, (L, L), 0)
    col = lax.broadcasted_iota(I32, (L, L), 1)
    causal = col <= row
    tril = causal.astype(F32)
    ones = jnp.ones((L, M_DV), BF16)

    for nb in range(MLSTM_SEQS):
        gb = gt_ref[nb] + bias_ref[...]
        ls = _log_sigmoid(gb)
        bcum = jnp.dot(tril, ls, preferred_element_type=F32, precision=lax.Precision.HIGHEST)
        gb_t = gb.T
        bcum_t = bcum.T
        k_t = (k_ref[nb] * (M_DK ** -0.5)).T
        for h in range(M_HEADS):
            sh = nb * M_HEADS + h
            b_col = bcum[:, M_HEADS + h:M_HEADS + h + 1]
            b_row = bcum_t[M_HEADS + h:M_HEADS + h + 1, :]
            i_row = gb_t[h:h + 1, :]
            m_prev = m_scr[nb, h:h + 1, 0:1]
            dmat = jnp.where(causal, b_col + (i_row - b_row), NEG)
            a_col = b_col + m_prev
            mt = jnp.maximum(a_col, jnp.max(dmat, axis=1, keepdims=True))
            w_intra = jnp.exp(dmat - mt)
            w_inter = jnp.exp(a_col - mt)
            q_h = q_ref[nb, :, h * M_DK:(h + 1) * M_DK].astype(BF16)
            kt_h = k_t[h * M_DK:(h + 1) * M_DK, :]
            qk = jnp.dot(q_h, kt_h.astype(BF16), preferred_element_type=F32)
            s_w = (qk * w_intra).astype(BF16)
            v_ext = jnp.concatenate([v_ref[nb, :, h * M_DV:(h + 1) * M_DV].astype(BF16), ones], axis=1)
            state = s_scr[sh]
            inter = jnp.dot(q_h, state.astype(BF16), preferred_element_type=F32)
            tot = w_inter * inter + jnp.dot(s_w, v_ext, preferred_element_type=F32)
            num = tot[:, :M_DV]
            qn = tot[:, M_DV:]
            den = jnp.maximum(jnp.abs(qn), jnp.exp(-mt))
            hh = num / den
            hn = hh * lax.rsqrt(jnp.mean(hh * hh, axis=-1, keepdims=True) + EPS)
            hn = hn * gain_ref[:, h * M_DV:(h + 1) * M_DV]
            out = hn * _sigmoid(o_ref[nb, :, h * M_DV:(h + 1) * M_DV])
            out_ref[nb, :, h * M_DV:(h + 1) * M_DV] = out.astype(out_ref.dtype)
            b_last = b_col[L - 1:L, :]
            m_new = mt[L - 1:L, :]
            g_prev = jnp.exp(b_last + m_prev - m_new)
            g_row = jnp.exp(b_last - b_row + i_row - m_new)
            kg_t = (kt_h * g_row).astype(BF16)
            s_scr[sh] = g_prev * state + jnp.dot(kg_t, v_ext, preferred_element_type=F32)
            m_scr[nb, h:h + 1, :] = jnp.broadcast_to(m_new, (1, LANES))

    @pl.when(c == pl.num_programs(1) - 1)
    def _():
        for nb in range(MLSTM_SEQS):
            for h in range(M_HEADS):
                st = s_scr[nb * M_HEADS + h]
                c_ref[nb, h] = st[:, :M_DV]
                n_ref[nb, h] = st[:, M_DV:]
        m_ref[...] = m_scr[...]


def _mlstm_prompt(z, bias_row, gain_row, batch, seq):
    L = MLSTM_CHUNK
    S = MLSTM_SEQS
    z3 = z.reshape(batch, seq, PROJ_W)
    return pl.pallas_call(
        _mlstm_prompt_kernel,
        out_shape=(
            jax.ShapeDtypeStruct((batch, seq, M_HEADS * M_DV), BF16),
            jax.ShapeDtypeStruct((batch, M_HEADS, M_DK, M_DV), F32),
            jax.ShapeDtypeStruct((batch, M_HEADS, M_DK, M_DV), F32),
            jax.ShapeDtypeStruct((batch, 8, LANES), F32),
        ),
        grid=(batch // S, seq // L),
        in_specs=[
            pl.BlockSpec((S, L, 256), lambda b, c: (b, c, C_MQ // 256)),
            pl.BlockSpec((S, L, 256), lambda b, c: (b, c, C_MK // 256)),
            pl.BlockSpec((S, L, 512), lambda b, c: (b, c, C_MV // 512)),
            pl.BlockSpec((S, L, 512), lambda b, c: (b, c, C_MO // 512)),
            pl.BlockSpec((S, L, LANES), lambda b, c: (b, c, C_GATE // LANES)),
            pl.BlockSpec((1, LANES), lambda b, c: (0, 0)),
            pl.BlockSpec((1, M_HEADS * M_DV), lambda b, c: (0, 0)),
        ],
        out_specs=(
            pl.BlockSpec((S, L, M_HEADS * M_DV), lambda b, c: (b, c, 0)),
            pl.BlockSpec((S, M_HEADS, M_DK, M_DV), lambda b, c: (b, 0, 0, 0)),
            pl.BlockSpec((S, M_HEADS, M_DK, M_DV), lambda b, c: (b, 0, 0, 0)),
            pl.BlockSpec((S, 8, LANES), lambda b, c: (b, 0, 0)),
        ),
        scratch_shapes=[pltpu.VMEM((S * M_HEADS, M_DK, 2 * M_DV), F32), pltpu.VMEM((S, 8, LANES), F32)],
        compiler_params=_cparams(("arbitrary", "arbitrary")),
        name="mlstm_prompt",
    )(z3, z3, z3, z3, z3, bias_row, gain_row)


def _attn_prompt_kernel(relb_ref, sink_ref, q_ref, kp_ref, kc_ref, vp_ref, vc_ref, bucket_ref,
                        out_ref, bias_scr):
    B = ATT_BLOCK
    j = pl.program_id(1)

    @pl.when((pl.program_id(0) == 0) & (j == 0))
    def _():
        bucket = bucket_ref[...]
        for h in range(A_HEADS):
            acc = jnp.full((B, 2 * B), NEG, F32)
            for bk in range(NUM_BUCKETS):
                acc = jnp.where(bucket == bk, relb_ref[bk * A_HEADS + h], acc)
            bias_scr[h] = acc

    scale = A_HD ** -0.5
    s_iota = lax.broadcasted_iota(I32, (B, 2 * B), 1)
    first = jnp.where((s_iota < B) & (j == 0), NEG, 0.0)
    outs = []
    for h in range(A_HEADS):
        g = h // A_GROUP
        q_h = q_ref[:, h * A_HD:(h + 1) * A_HD].astype(BF16)
        k2 = jnp.concatenate([kp_ref[:, g * A_HD:(g + 1) * A_HD], kc_ref[:, g * A_HD:(g + 1) * A_HD]],
                             axis=0).astype(BF16)
        v2 = jnp.concatenate([vp_ref[:, g * A_HD:(g + 1) * A_HD], vc_ref[:, g * A_HD:(g + 1) * A_HD]],
                             axis=0).astype(BF16)
        logits = lax.dot_general(q_h, k2, (((1,), (1,)), ((), ())), preferred_element_type=F32)
        logits = logits * scale + bias_scr[h] + first
        sink = sink_ref[h]
        m = jnp.maximum(jnp.max(logits, axis=-1, keepdims=True), sink)
        p = jnp.exp(logits - m)
        den = jnp.sum(p, axis=-1, keepdims=True) + jnp.exp(sink - m)
        o = jnp.dot(p.astype(BF16), v2, preferred_element_type=F32) / den
        outs.append(o)
    out_ref[...] = jnp.concatenate(outs, axis=1).astype(out_ref.dtype)


def _attn_prompt(z, rel_bias, sinks, batch, seq):
    B = ATT_BLOCK
    nb = seq // B
    qi = np.arange(B)[:, None]
    si = np.arange(2 * B)[None, :]
    dist = qi + B - si
    bucket = np.where((dist >= 0) & (dist <= WINDOW), _t5_bucket_np(dist), -1).astype(np.int32)
    cur = lambda b, j, *_: b * nb + j
    prev = lambda b, j, *_: b * nb + jnp.maximum(j - 1, 0)
    grid_spec = pltpu.PrefetchScalarGridSpec(
        num_scalar_prefetch=2,
        grid=(batch, nb),
        in_specs=[
            pl.BlockSpec((B, 512), lambda b, j, *_: (cur(b, j), C_AQ // 512)),
            pl.BlockSpec((B, LANES), lambda b, j, *_: (prev(b, j), C_AK // LANES)),
            pl.BlockSpec((B, LANES), lambda b, j, *_: (cur(b, j), C_AK // LANES)),
            pl.BlockSpec((B, LANES), lambda b, j, *_: (prev(b, j), C_AV // LANES)),
            pl.BlockSpec((B, LANES), lambda b, j, *_: (cur(b, j), C_AV // LANES)),
            pl.BlockSpec((B, 2 * B), lambda b, j, *_: (0, 0)),
        ],
        out_specs=pl.BlockSpec((B, A_HEADS * A_HD), lambda b, j, *_: (cur(b, j), 0)),
        scratch_shapes=[pltpu.VMEM((A_HEADS, B, 2 * B), F32)],
    )
    return pl.pallas_call(
        _attn_prompt_kernel,
        out_shape=jax.ShapeDtypeStruct((batch * seq, A_HEADS * A_HD), BF16),
        grid_spec=grid_spec,
        compiler_params=_cparams(("arbitrary", "arbitrary")),
        name="attn_prompt",
    )(rel_bias.reshape(-1), sinks, z, z, z, z, z, jnp.asarray(bucket))


def _outproj_router_kernel(x_ref, mo_ref, ao_ref, wm_ref, wa_ref, g_ref, wr_ref, br_ref, cin_ref,
                           y_ref, yn_ref, eidx_ref, gate_ref, rank_ref, cout_ref, carry):
    T = x_ref.shape[0]
    i = pl.program_id(0)

    @pl.when(i == 0)
    def _():
        carry[...] = cin_ref[...]

    y = (x_ref[...] + jnp.dot(mo_ref[...], wm_ref[...], preferred_element_type=F32)
         + jnp.dot(ao_ref[...], wa_ref[...], preferred_element_type=F32))
    y_ref[...] = y
    yn = _rms(y, g_ref[...])
    for s in range(ROW_TILES):
        yn_ref[pl.ds(s, T, stride=ROW_TILES), :] = yn[:, s * LANES:(s + 1) * LANES]
    yh = yn.astype(BF16)
    yl = (yn - yh.astype(F32)).astype(BF16)
    hh = jnp.dot(yh, wr_ref[...], preferred_element_type=F32)
    lh = jnp.dot(yl, wr_ref[:, :LANES], preferred_element_type=F32)
    logits = hh[:, :LANES] + (hh[:, LANES:] + lh) + br_ref[...]
    lane = lax.broadcasted_iota(I32, (T, LANES), 1)
    lane_f = lane.astype(F32)
    vals, idxs, hots = [], [], []
    l = logits
    for _ in range(TOP_K):
        mx = jnp.max(l, axis=-1, keepdims=True)
        idx = jnp.min(jnp.where(l == mx, lane_f, float(LANES)), axis=-1, keepdims=True)
        hot = lane_f == idx
        l = jnp.where(hot, -jnp.inf, l)
        vals.append(mx)
        idxs.append(idx)
        hots.append(hot)
    es = [jnp.exp(v - vals[0]) for v in vals]
    tot = es[0] + es[1] + es[2] + es[3]
    sel = jnp.where(hots[0] | hots[1] | hots[2] | hots[3], 1.0, 0.0)
    row = lax.broadcasted_iota(I32, (T, T), 0)
    col = lax.broadcasted_iota(I32, (T, T), 1)
    strict = (col < row).astype(BF16)
    before = carry[...] + jnp.dot(strict, sel.astype(BF16), preferred_element_type=F32)
    eidx = jnp.zeros((T, LANES), I32)
    gate = jnp.zeros((T, LANES), F32)
    rank = jnp.zeros((T, LANES), I32)
    for k in range(TOP_K):
        r_k = jnp.sum(jnp.where(hots[k], before, 0.0), axis=-1, keepdims=True)
        eidx = jnp.where(lane == k, idxs[k].astype(I32), eidx)
        gate = jnp.where(lane == k, es[k] / tot, gate)
        rank = jnp.where(lane == k, r_k.astype(I32), rank)
    eidx_ref[...] = eidx
    gate_ref[...] = gate
    rank_ref[...] = rank
    carry[...] = carry[...] + jnp.sum(sel, axis=0, keepdims=True)
    cout_ref[...] = carry[...]


def _outproj_router(x2, m_out, a_out, w_m, w_a, norm_row, w_r, b_r, counts_in, tile):
    n = x2.shape[0]
    tok = lambda i: (i, 0)
    fix = lambda i: (0, 0)
    return pl.pallas_call(
        _outproj_router_kernel,
        out_shape=(
            jax.ShapeDtypeStruct((n, D_MODEL), F32),
            jax.ShapeDtypeStruct((n * ROW_TILES, LANES), F32),
            jax.ShapeDtypeStruct((n, LANES), I32),
            jax.ShapeDtypeStruct((n, LANES), F32),
            jax.ShapeDtypeStruct((n, LANES), I32),
            jax.ShapeDtypeStruct((1, LANES), F32),
        ),
        grid=(n // tile,),
        in_specs=[
            pl.BlockSpec((tile, D_MODEL), tok),
            pl.BlockSpec((tile, 512), tok),
            pl.BlockSpec((tile, 512), tok),
            pl.BlockSpec((512, D_MODEL), fix),
            pl.BlockSpec((512, D_MODEL), fix),
            pl.BlockSpec((1, D_MODEL), fix),
            pl.BlockSpec((D_MODEL, 2 * LANES), fix),
            pl.BlockSpec((1, LANES), fix),
            pl.BlockSpec((1, LANES), fix),
        ],
        out_specs=(
            pl.BlockSpec((tile, D_MODEL), tok),
            pl.BlockSpec((tile * ROW_TILES, LANES), tok),
            pl.BlockSpec((tile, LANES), tok),
            pl.BlockSpec((tile, LANES), tok),
            pl.BlockSpec((tile, LANES), tok),
            pl.BlockSpec((1, LANES), fix),
        ),
        scratch_shapes=[pltpu.VMEM((1, LANES), F32)],
        compiler_params=_cparams(("arbitrary",)),
        name="outproj_router",
    )(x2, m_out, a_out, w_m, w_a, norm_row, w_r, b_r, counts_in)


def _dispatch_kernel(dest_ref, tail_ref, yn_ref, *rest, first):
    xs_ref, zero_scr, sem = rest if first else rest[1:]
    T = yn_ref.shape[0]
    i = pl.program_id(0)

    if first:
        @pl.when(i == 0)
        def _():
            zero_scr[...] = jnp.zeros_like(zero_scr)

            def fill(e, carry):
                @pl.when(tail_ref[e] >= 0)
                def _():
                    pltpu.make_async_copy(zero_scr, xs_ref.at[pl.ds(tail_ref[e], FFN_ROWS)], sem).start()
                return carry

            def drain(e, carry):
                @pl.when(tail_ref[e] >= 0)
                def _():
                    pltpu.make_async_copy(zero_scr, xs_ref.at[pl.ds(tail_ref[e], FFN_ROWS)], sem).wait()
                return carry

            lax.fori_loop(0, N_EXPERTS, fill, 0)
            lax.fori_loop(0, N_EXPERTS, drain, 0)

    base = i * (T * TOP_K)

    def issue(tb, carry):
        for u in range(DMA_UNROLL):
            t = tb * DMA_UNROLL + u
            for k in range(TOP_K):
                d = dest_ref[base + t * TOP_K + k]
                pltpu.make_async_copy(yn_ref.at[t], xs_ref.at[d], sem).start(priority=k % 2)
        return carry

    lax.fori_loop(0, T // DMA_UNROLL, issue, 0)
    for k in range(TOP_K):
        pltpu.make_async_copy(yn_ref, xs_ref.at[pl.ds(0, T)], sem).wait()


def _dispatch(dest_flat, tail, yn, xs, n_rows, tile):
    n = yn.shape[0]
    first = xs is None
    in_specs = [pl.BlockSpec((tile, ROW_TILES, LANES), lambda i, *_: (i, 0, 0))]
    operands = [dest_flat, tail, yn]
    if not first:
        in_specs.append(pl.BlockSpec(memory_space=pl.ANY))
        operands.append(xs)
    grid_spec = pltpu.PrefetchScalarGridSpec(
        num_scalar_prefetch=2,
        grid=(n // tile,),
        in_specs=in_specs,
        out_specs=pl.BlockSpec(memory_space=pl.ANY),
        scratch_shapes=[pltpu.VMEM((FFN_ROWS, ROW_TILES, LANES), F32), pltpu.SemaphoreType.DMA],
    )
    return pl.pallas_call(
        functools.partial(_dispatch_kernel, first=first),
        out_shape=jax.ShapeDtypeStruct((n_rows, ROW_TILES, LANES), F32),
        grid_spec=grid_spec,
        input_output_aliases={} if first else {3: 0},
        compiler_params=_cparams(("arbitrary",)),
        name="moe_dispatch_first" if first else "moe_dispatch_more",
    )(*operands)


def _ffn_kernel(be_ref, nused_ref, valid_ref, xs_ref, wg_ref, bg_ref, wu_ref, bu_ref, wd_ref, bd_ref, out_ref,
                wg_bf, wu_bf, wd_bf, h_scr):
    i = pl.program_id(0)

    @pl.when(i < nused_ref[0])
    def _():
        @pl.when((i == 0) | (be_ref[i] != be_ref[jnp.maximum(i - 1, 0)]))
        def _():
            for src, dst in ((wg_ref, wg_bf), (wu_ref, wu_bf), (wd_ref, wd_bf)):
                for r in range(0, src.shape[0], CAST_ROWS):
                    dst[r:r + CAST_ROWS, :] = src[r:r + CAST_ROWS, :].astype(BF16)

        for sb in range(FFN_ROWS // FFN_SUB):
            @pl.when(valid_ref[i] > sb * FFN_SUB)
            def _():
                r0 = sb * FFN_SUB * ROW_TILES
                x = jnp.concatenate([xs_ref[pl.ds(r0 + s, FFN_SUB, stride=ROW_TILES), :].astype(BF16)
                                     for s in range(ROW_TILES)], axis=1)
                for c in range(D_FF // FFN_COLS):
                    cs = slice(c * FFN_COLS, (c + 1) * FFN_COLS)
                    g = jnp.dot(x, wg_bf[:, cs], preferred_element_type=F32) + bg_ref[:, cs]
                    u = jnp.dot(x, wu_bf[:, cs], preferred_element_type=F32) + bu_ref[:, cs]
                    g = jnp.minimum(g, SWIGLU_LIMIT)
                    u = jnp.clip(u, -SWIGLU_LIMIT, SWIGLU_LIMIT)
                    h_scr[:, cs] = ((u + 1.0) * (g * _sigmoid(SWIGLU_ALPHA * g))).astype(BF16)
                out = jnp.dot(h_scr[...], wd_bf[...], preferred_element_type=F32) + bd_ref[...]
                for s in range(ROW_TILES):
                    out_ref[pl.ds(r0 + s, FFN_SUB, stride=ROW_TILES), :] = out[:, s * LANES:(s + 1) * LANES]


def _ffn(block_e, nused, valid, xs, wg, bg, wu, bu, wd, bd):
    nb = xs.shape[0] // (FFN_ROWS * ROW_TILES)
    blk = lambda i, be, nu, va: (jnp.minimum(i, nu[0] - 1), 0)
    wsel = lambda i, be, nu, va: (be[i], 0, 0)
    grid_spec = pltpu.PrefetchScalarGridSpec(
        num_scalar_prefetch=3,
        grid=(nb,),
        in_specs=[
            pl.BlockSpec((FFN_ROWS * ROW_TILES, LANES), blk),
            pl.BlockSpec((None, D_MODEL, D_FF), wsel),
            pl.BlockSpec((None, 1, D_FF), wsel),
            pl.BlockSpec((None, D_MODEL, D_FF), wsel),
            pl.BlockSpec((None, 1, D_FF), wsel),
            pl.BlockSpec((None, D_FF, D_MODEL), wsel),
            pl.BlockSpec((None, 1, D_MODEL), wsel),
        ],
        out_specs=pl.BlockSpec((FFN_ROWS * ROW_TILES, LANES), blk),
        scratch_shapes=[pltpu.VMEM((D_MODEL, D_FF), BF16), pltpu.VMEM((D_MODEL, D_FF), BF16),
                        pltpu.VMEM((D_FF, D_MODEL), BF16), pltpu.VMEM((FFN_SUB, D_FF), BF16)],
    )
    return pl.pallas_call(
        _ffn_kernel,
        out_shape=jax.ShapeDtypeStruct(xs.shape, F32),
        grid_spec=grid_spec,
        compiler_params=_cparams(("arbitrary",)),
        name="moe_ffn",
    )(block_e, nused, valid, xs, wg, bg, wu, bu, wd, bd)


def _combine_kernel(dest_ref, y_ref, gate_ref, fn_ref, ffn_ref, out_ref, buf, sem):
    T = y_ref.shape[0]
    i = pl.program_id(0)
    n = pl.num_programs(0)
    slot = i % 2

    def issue(tile, s):
        base = tile * (T * TOP_K)

        def body(tb, carry):
            for u in range(DMA_UNROLL):
                t = tb * DMA_UNROLL + u
                for k in range(TOP_K):
                    d = dest_ref[base + t * TOP_K + k]
                    pltpu.make_async_copy(ffn_ref.at[d], buf.at[s, k, pl.ds(t * ROW_TILES, ROW_TILES)],
                                          sem.at[s]).start(priority=k % 2)
            return carry

        lax.fori_loop(0, T // DMA_UNROLL, body, 0)

    @pl.when(i == 0)
    def _():
        issue(0, 0)

    @pl.when(i + 1 < n)
    def _():
        issue(i + 1, 1 - slot)

    for k in range(TOP_K):
        pltpu.make_async_copy(buf.at[1 - slot, k], buf.at[slot, k], sem.at[slot]).wait()

    acc = y_ref[...]
    gate = gate_ref[...]
    for k in range(TOP_K):
        rows = jnp.concatenate([buf[slot, k, pl.ds(s, T, stride=ROW_TILES), :] for s in range(ROW_TILES)],
                               axis=1)
        acc = acc + gate[:, k:k + 1] * rows
    out_ref[...] = _rms(acc, fn_ref[...])


def _combine(dest_flat, y, gate, fnorm_row, ffn_out, tile):
    n = y.shape[0]
    grid_spec = pltpu.PrefetchScalarGridSpec(
        num_scalar_prefetch=1,
        grid=(n // tile,),
        in_specs=[
            pl.BlockSpec((tile, D_MODEL), lambda i, *_: (i, 0)),
            pl.BlockSpec((tile, LANES), lambda i, *_: (i, 0)),
            pl.BlockSpec((1, D_MODEL), lambda i, *_: (0, 0)),
            pl.BlockSpec(memory_space=pl.ANY),
        ],
        out_specs=pl.BlockSpec((tile, D_MODEL), lambda i, *_: (i, 0)),
        scratch_shapes=[pltpu.VMEM((2, TOP_K, tile * ROW_TILES, LANES), F32), pltpu.SemaphoreType.DMA((2,))],
    )
    return pl.pallas_call(
        _combine_kernel,
        out_shape=jax.ShapeDtypeStruct((n, D_MODEL), F32),
        grid_spec=grid_spec,
        compiler_params=_cparams(("arbitrary",)),
        name="moe_combine",
    )(dest_flat, y, gate, fnorm_row, ffn_out)


def _mlstm_step_kernel(q_ref, k_ref, v_ref, o_ref, gt_ref, bias_ref, gain_ref, c0_ref, n0_ref, m0_ref,
                       out_ref, c_ref, n_ref, m_ref):
    TB = SAMPLE_MLSTM_TB
    gb = gt_ref[...] + bias_ref[...]
    ls = _log_sigmoid(gb)
    lane = lax.broadcasted_iota(I32, (TB, LANES), 1)
    eye = (lax.broadcasted_iota(I32, (M_DK, M_DK), 0) == lax.broadcasted_iota(I32, (M_DK, M_DK), 1)).astype(F32)
    nt = (((1,), (1,)), ((), ()))
    m_all = jnp.zeros((TB, LANES), F32)
    for h in range(M_HEADS):
        i_pre = gb[:, h:h + 1]
        a = ls[:, M_HEADS + h:M_HEADS + h + 1] + m0_ref[:, h:h + 1]
        mt = jnp.maximum(a, i_pre)
        w_intra = jnp.exp(i_pre - mt)
        w_inter = jnp.exp(a - mt)
        q_h = q_ref[:, h * M_DK:(h + 1) * M_DK]
        k_h = k_ref[:, h * M_DK:(h + 1) * M_DK] * (M_DK ** -0.5)
        v_h = v_ref[:, h * M_DV:(h + 1) * M_DV]
        n0_h = n0_ref[:, h, :]
        s = jnp.sum(q_h * k_h, axis=-1, keepdims=True) * w_intra
        qn = w_inter * jnp.sum(q_h * n0_h, axis=-1, keepdims=True) + s
        den = jnp.maximum(jnp.abs(qn), jnp.exp(-mt))
        q_t = lax.dot_general(eye, q_h, nt, preferred_element_type=F32, precision=lax.Precision.HIGHEST)
        k_t = lax.dot_general(eye, k_h, nt, preferred_element_type=F32, precision=lax.Precision.HIGHEST)
        rows = []
        for b in range(TB):
            c0 = c0_ref[b, h]
            qc = jnp.sum(c0 * q_t[:, b:b + 1], axis=0, keepdims=True)
            v_b = v_h[b:b + 1, :]
            rows.append(w_inter[b:b + 1, :] * qc + s[b:b + 1, :] * v_b)
            c_ref[b, h] = w_inter[b:b + 1, :] * c0 + (w_intra[b:b + 1, :] * k_t[:, b:b + 1]) * v_b
        num = jnp.concatenate(rows, axis=0)
        hh = num / den
        hn = hh * lax.rsqrt(jnp.mean(hh * hh, axis=-1, keepdims=True) + EPS)
        hn = hn * gain_ref[:, h * M_DV:(h + 1) * M_DV]
        out_ref[:, h * M_DV:(h + 1) * M_DV] = (hn * _sigmoid(o_ref[:, h * M_DV:(h + 1) * M_DV])).astype(out_ref.dtype)
        n_ref[:, h * M_DK:(h + 1) * M_DK] = w_inter * n0_h + w_intra * k_h
        m_all = jnp.where(lane == h, mt, m_all)
    m_ref[...] = m_all


def _mlstm_step(zs, bias_row, gain_row, c0, n0, m0):
    TB = SAMPLE_MLSTM_TB
    nb = zs.shape[0]
    tok = lambda i: (i, 0)
    return pl.pallas_call(
        _mlstm_step_kernel,
        out_shape=(
            jax.ShapeDtypeStruct((nb, M_HEADS * M_DV), BF16),
            jax.ShapeDtypeStruct((nb, M_HEADS, M_DK, M_DV), F32),
            jax.ShapeDtypeStruct((nb, M_HEADS * M_DK), F32),
            jax.ShapeDtypeStruct((nb, LANES), F32),
        ),
        grid=(nb // TB,),
        in_specs=[
            pl.BlockSpec((TB, 256), lambda i: (i, C_MQ // 256)),
            pl.BlockSpec((TB, 256), lambda i: (i, C_MK // 256)),
            pl.BlockSpec((TB, 512), lambda i: (i, C_MV // 512)),
            pl.BlockSpec((TB, 512), lambda i: (i, C_MO // 512)),
            pl.BlockSpec((TB, LANES), lambda i: (i, C_GATE // LANES)),
            pl.BlockSpec((1, LANES), lambda i: (0, 0)),
            pl.BlockSpec((1, M_HEADS * M_DV), lambda i: (0, 0)),
            pl.BlockSpec((TB, M_HEADS, M_DK, M_DV), lambda i: (i, 0, 0, 0)),
            pl.BlockSpec((TB, M_HEADS, M_DK), lambda i: (i, 0, 0)),
            pl.BlockSpec((TB, M_HEADS), tok),
        ],
        out_specs=(
            pl.BlockSpec((TB, M_HEADS * M_DV), tok),
            pl.BlockSpec((TB, M_HEADS, M_DK, M_DV), lambda i: (i, 0, 0, 0)),
            pl.BlockSpec((TB, M_HEADS * M_DK), tok),
            pl.BlockSpec((TB, LANES), tok),
        ),
        compiler_params=_cparams(("arbitrary",)),
        name="mlstm_step",
    )(zs, zs, zs, zs, zs, bias_row, gain_row, c0, n0, m0)


def _attn_step_kernel(q_ref, kn_ref, vn_ref, ck_ref, cv_ref, bucket_ref, relt_ref, sink_ref,
                      out_ref, nk_ref, nv_ref, bias_scr):
    TB = SAMPLE_ATT_TB
    W = ck_ref.shape[1]

    @pl.when(pl.program_id(0) == 0)
    def _():
        bucket = jnp.broadcast_to(bucket_ref[...], (A_HEADS, W))
        acc = jnp.zeros((A_HEADS, W), F32)
        for bk in range(NUM_BUCKETS):
            acc = jnp.where(bucket == bk, relt_ref[:, bk:bk + 1], acc)
        bias_scr[...] = acc

    scale = A_HD ** -0.5
    nt = (((1,), (1,)), ((), ()))
    bias = bias_scr[...]
    bias_new = relt_ref[:, 0:1]
    sink = sink_ref[...]
    low = lax.broadcasted_iota(I32, (A_HEADS, 1), 0) < A_GROUP
    for b in range(TB):
        q = q_ref[b]
        qb = q.astype(BF16)
        kc = ck_ref[b]
        vc = cv_ref[b]
        kn = kn_ref[b:b + 1, :]
        vn = vn_ref[b:b + 1, :]
        l0 = lax.dot_general(qb, kc[:, :A_HD].astype(BF16), nt, preferred_element_type=F32)
        l1 = lax.dot_general(qb, kc[:, A_HD:].astype(BF16), nt, preferred_element_type=F32)
        logits = jnp.where(low, l0, l1) * scale + bias
        kn_h = jnp.where(low, kn[:, :A_HD], kn[:, A_HD:])
        vn_h = jnp.where(low, vn[:, :A_HD], vn[:, A_HD:])
        l_new = jnp.sum(q * kn_h, axis=-1, keepdims=True) * scale + bias_new
        m = jnp.maximum(jnp.maximum(jnp.max(logits, axis=-1, keepdims=True), l_new), sink)
        p = jnp.exp(logits - m)
        p_new = jnp.exp(l_new - m)
        den = jnp.sum(p, axis=-1, keepdims=True) + p_new + jnp.exp(sink - m)
        pb = p.astype(BF16)
        o0 = jnp.dot(pb, vc[:, :A_HD].astype(BF16), preferred_element_type=F32)
        o1 = jnp.dot(pb, vc[:, A_HD:].astype(BF16), preferred_element_type=F32)
        o = jnp.where(low, o0, o1) + p_new * vn_h
        out_ref[b] = o / den
        nk_ref[b, 0:W - 1, :] = ck_ref[b, 1:W, :]
        nk_ref[b, W - 1:W, :] = kn
        nv_ref[b, 0:W - 1, :] = cv_ref[b, 1:W, :]
        nv_ref[b, W - 1:W, :] = vn


def _attn_step(q3, k_new, v_new, ck, cv, rel_bias, sinks):
    TB = SAMPLE_ATT_TB
    nb, W = ck.shape[0], ck.shape[1]
    bucket = _t5_bucket_np(W - np.arange(W))[None, :].astype(np.int32)
    tok = lambda i: (i, 0)
    tok3 = lambda i: (i, 0, 0)
    fix = lambda i: (0, 0)
    return pl.pallas_call(
        _attn_step_kernel,
        out_shape=(
            jax.ShapeDtypeStruct((nb, A_HEADS, A_HD), F32),
            jax.ShapeDtypeStruct(ck.shape, F32),
            jax.ShapeDtypeStruct(cv.shape, F32),
        ),
        grid=(nb // TB,),
        in_specs=[
            pl.BlockSpec((TB, A_HEADS, A_HD), tok3),
            pl.BlockSpec((TB, LANES), lambda i: (i, C_AK // LANES)),
            pl.BlockSpec((TB, LANES), lambda i: (i, C_AV // LANES)),
            pl.BlockSpec((TB, W, A_KV * A_HD), tok3),
            pl.BlockSpec((TB, W, A_KV * A_HD), tok3),
            pl.BlockSpec((1, W), fix),
            pl.BlockSpec((A_HEADS, NUM_BUCKETS), fix),
            pl.BlockSpec((A_HEADS, 1), fix),
        ],
        out_specs=(
            pl.BlockSpec((TB, A_HEADS, A_HD), tok3),
            pl.BlockSpec((TB, W, A_KV * A_HD), tok3),
            pl.BlockSpec((TB, W, A_KV * A_HD), tok3),
        ),
        scratch_shapes=[pltpu.VMEM((A_HEADS, W), F32)],
        compiler_params=_cparams(("arbitrary",)),
        name="attn_step",
    )(q3, k_new, v_new, ck, cv, jnp.asarray(bucket), rel_bias.T, sinks.reshape(A_HEADS, 1))


def _reorder_w_in(w_in):
    o = 0
    parts = {}
    for name, width in (("mq", 256), ("mk", 256), ("mv", 512), ("mo", 512), ("mi", 4), ("mf", 4),
                        ("aq", 512), ("ak", 128), ("av", 128)):
        parts[name] = w_in[:, o:o + width]
        o += width
    pad = jnp.zeros((w_in.shape[0], LANES - 2 * M_HEADS), w_in.dtype)
    cols = [parts[n] for n in ("mq", "mk", "mv", "mo", "aq", "ak", "av", "mi", "mf")] + [pad]
    return jnp.concatenate(cols, axis=1).astype(BF16)


def _lane_row(v, fill=0.0):
    return jnp.concatenate([v.astype(F32), jnp.full((LANES - v.shape[0],), fill, F32)])[None, :]


def kernel(x_prompt, x_sample, state_C, state_n, state_m, cache_k, cache_v, rel_bias, norm1, w_in, b_if,
           m_gain, sinks, w_out, norm2, w_router, b_router, w_gate, b_gate, w_up, b_up, w_down, b_down,
           final_norm):
    assert norm1.shape[0] == 1, "single-layer trunk"
    batch, seq, _ = x_prompt.shape
    nsmp = x_sample.shape[0]
    n_p = batch * seq
    W = cache_k.shape[2]

    xp = x_prompt.reshape(n_p, D_MODEL)
    xs_ = x_sample.reshape(nsmp, D_MODEL)
    w_in_r = _reorder_w_in(w_in[0])
    n1 = norm1[0][None, :]
    n2 = norm2[0][None, :]
    fn = final_norm[None, :]
    bias_row = _lane_row(b_if[0])
    gain_row = m_gain[0][None, :]
    w_m = w_out[0][:M_HEADS * M_DV].astype(BF16)
    w_a = w_out[0][M_HEADS * M_DV:].astype(BF16)
    w_r32 = jnp.concatenate([w_router[0], jnp.zeros((D_MODEL, LANES - N_EXPERTS), F32)], axis=1)
    w_r_hi = w_r32.astype(BF16)
    w_r = jnp.concatenate([w_r_hi, (w_r32 - w_r_hi.astype(F32)).astype(BF16)], axis=1)
    b_r = _lane_row(b_router[0], NEG)

    zp = _inproj(xp, n1, w_in_r, TOK_TILE)
    zs = _inproj(xs_, n1, w_in_r, nsmp)
    m_out_p, p_c, p_nrep, p_mrep = _mlstm_prompt(zp, bias_row, gain_row, batch, seq)
    m_out_p = m_out_p.reshape(n_p, M_HEADS * M_DV)
    a_out_p = _attn_prompt(zp, rel_bias, sinks[0], batch, seq)
    m_out_s, s_c, s_n, s_mrep = _mlstm_step(zs, bias_row, gain_row, state_C[0], state_n[0], state_m[0])
    q3 = zs[:, C_AQ:C_AQ + A_HEADS * A_HD].reshape(nsmp, A_HEADS, A_HD)
    a3, s_k, s_v = _attn_step(q3, zs, zs, cache_k[0].reshape(nsmp, W, A_KV * A_HD),
                              cache_v[0].reshape(nsmp, W, A_KV * A_HD), rel_bias, sinks[0])
    a_out_s = a3.reshape(nsmp, A_HEADS * A_HD).astype(BF16)

    zero_counts = jnp.zeros((1, LANES), F32)
    y_p, yn_p, e_p, g_p, r_p, cnt_p = _outproj_router(xp, m_out_p, a_out_p, w_m, w_a, n2, w_r, b_r,
                                                      zero_counts, TOK_TILE)
    y_s, yn_s, e_s, g_s, r_s, cnt = _outproj_router(xs_, m_out_s, a_out_s, w_m, w_a, n2, w_r, b_r,
                                                    cnt_p, nsmp)

    counts = cnt[0, :N_EXPERTS].astype(I32)
    padded = (counts + FFN_ROWS - 1) // FFN_ROWS * FFN_ROWS
    pad_end = jnp.cumsum(padded)
    pad_start = pad_end - padded
    n_rows = ((n_p + nsmp) * TOP_K + N_EXPERTS * (FFN_ROWS - 1) + FFN_ROWS - 1) // FFN_ROWS * FFN_ROWS
    n_blocks = n_rows // FFN_ROWS
    nused = jnp.maximum(pad_end[-1] // FFN_ROWS, 1).astype(I32)
    blk_start = jnp.minimum(jnp.arange(n_blocks, dtype=I32), nused - 1) * FFN_ROWS
    block_e = jnp.minimum(jnp.sum((blk_start[:, None] >= pad_end[None, :]).astype(I32), axis=1), N_EXPERTS - 1)
    tail = jnp.where(counts % FFN_ROWS != 0, pad_end - FFN_ROWS, -1).astype(I32)
    blk_valid = jnp.clip((pad_start + counts)[block_e] - blk_start, 0, FFN_ROWS).astype(I32)
    def slots(e, r):
        hot = e[:, :TOP_K, None] == jnp.arange(N_EXPERTS, dtype=I32)
        return (jnp.sum(jnp.where(hot, pad_start, 0), axis=-1) + r[:, :TOP_K]).reshape(-1).astype(I32)

    dest_p = slots(e_p, r_p)
    dest_s = slots(e_s, r_s)

    as_tiles = lambda a: a.reshape(-1, ROW_TILES, LANES)
    xs_rows = _dispatch(dest_p, tail, as_tiles(yn_p), None, n_rows, DISPATCH_TILE)
    xs_rows = _dispatch(dest_s, tail, as_tiles(yn_s), xs_rows, n_rows, nsmp)
    ffn_out = _ffn(block_e, nused.reshape(1), blk_valid, xs_rows.reshape(-1, LANES),
                   w_gate[0], b_gate[0][:, None, :], w_up[0], b_up[0][:, None, :],
                   w_down[0], b_down[0][:, None, :])
    out_p = _combine(dest_p, y_p, g_p, fn, as_tiles(ffn_out), COMBINE_TILE)
    out_s = _combine(dest_s, y_s, g_s, fn, as_tiles(ffn_out), COMBINE_TILE)

    kv_shape = (1, batch, WINDOW, A_KV, A_HD)
    zk = zp[:, C_AK:C_AK + A_KV * A_HD].reshape(batch, seq, A_KV * A_HD)[:, seq - WINDOW:]
    zv = zp[:, C_AV:C_AV + A_KV * A_HD].reshape(batch, seq, A_KV * A_HD)[:, seq - WINDOW:]
    return (
        out_p.reshape(batch, seq, D_MODEL),
        out_s.reshape(nsmp, 1, D_MODEL),
        p_c[None],
        p_nrep[None, :, :, :, 0],
        p_mrep[None, :, :M_HEADS, 0],
        zk.reshape(kv_shape),
        zv.reshape(kv_shape),
        s_c[None],
        s_n.reshape(1, nsmp, M_HEADS, M_DK),
        s_mrep[None, :, :M_HEADS],
        s_k.reshape(1, nsmp, W, A_KV, A_HD),
        s_v.reshape(1, nsmp, W, A_KV, A_HD),
    )
```

```python
import functools
import math

import numpy as np
import jax
import jax.numpy as jnp
from jax import lax
from jax.experimental import pallas as pl
from jax.experimental.pallas import tpu as pltpu

F32 = jnp.float32
BF16 = jnp.bfloat16
I32 = jnp.int32

D_MODEL = 1024
M_HEADS = 4
M_DK = 64
M_DV = 128
A_HEADS = 8
A_KV = 2
A_GROUP = A_HEADS // A_KV
A_HD = 64
WINDOW = 128
NUM_BUCKETS = 32
MAX_DISTANCE = 128
N_EXPERTS = 32
TOP_K = 4
D_FF = 1024
SWIGLU_LIMIT = 7.0
SWIGLU_ALPHA = 1.702
EPS = 1e-5

LANES = 128
NEG = -1e30
VMEM_LIMIT = 48 * 1024 * 1024

C_MQ, C_MK, C_MV, C_MO, C_AQ, C_AK, C_AV, C_GATE = 0, 256, 512, 1024, 1536, 2048, 2176, 2304
PROJ_W = 2432

MLSTM_CHUNK = 128
MLSTM_SEQS = 4
ATT_BLOCK = 128
TOK_TILE = 512
FFN_ROWS = 512
FFN_SUB = 256
ROW_TILES = D_MODEL // LANES
FFN_COLS = 256
CAST_ROWS = 128
DISPATCH_TILE = 256
COMBINE_TILE = 128
DMA_UNROLL = 8
SAMPLE_MLSTM_TB = 16
SAMPLE_ATT_TB = 8


def _t5_bucket_np(dist):
    n = np.maximum(dist, 0)
    max_exact = NUM_BUCKETS // 2
    ratio = np.log(np.maximum(n, 1).astype(np.float32) / np.float32(max_exact)) / np.float32(
        math.log(MAX_DISTANCE / max_exact))
    large = max_exact + (ratio * np.float32(NUM_BUCKETS - max_exact)).astype(np.int32)
    large = np.minimum(large, NUM_BUCKETS - 1)
    return np.where(n < max_exact, n, large).astype(np.int32)


def _cparams(sem):
    return pltpu.CompilerParams(dimension_semantics=sem, vmem_limit_bytes=VMEM_LIMIT)


def _rms(x, g):
    return x * lax.rsqrt(jnp.mean(x * x, axis=-1, keepdims=True) + EPS) * g


def _log_sigmoid(x):
    return jnp.minimum(x, 0.0) - jnp.log(1.0 + jnp.exp(-jnp.abs(x)))


def _sigmoid(x):
    return 1.0 / (1.0 + jnp.exp(-x))


def _inproj_kernel(x_ref, g_ref, w_ref, z_ref):
    xn = _rms(x_ref[...], g_ref[...]).astype(BF16)
    z_ref[...] = jnp.dot(xn, w_ref[...], preferred_element_type=F32)


def _inproj(x2, norm_row, w_bf16, tile):
    n = x2.shape[0]
    return pl.pallas_call(
        _inproj_kernel,
        out_shape=jax.ShapeDtypeStruct((n, PROJ_W), F32),
        grid=(n // tile,),
        in_specs=[
            pl.BlockSpec((tile, D_MODEL), lambda i: (i, 0)),
            pl.BlockSpec((1, D_MODEL), lambda i: (0, 0)),
            pl.BlockSpec((D_MODEL, PROJ_W), lambda i: (0, 0)),
        ],
        out_specs=pl.BlockSpec((tile, PROJ_W), lambda i: (i, 0)),
        compiler_params=_cparams(("arbitrary",)),
        name="inproj",
    )(x2, norm_row, w_bf16)


def _mlstm_prompt_kernel(q_ref, k_ref, v_ref, o_ref, gt_ref, bias_ref, gain_ref,
                         out_ref, c_ref, n_ref, m_ref, s_scr, m_scr):
    L = MLSTM_CHUNK
    c = pl.program_id(1)

    @pl.when(c == 0)
    def _():
        s_scr[...] = jnp.zeros_like(s_scr)
        m_scr[...] = jnp.zeros_like(m_scr)

    row = lax.broadcasted_iota(I32, (L, L), 0)
    col = lax.broadcasted_iota(I32, (L, L), 1)
    causal = col <= row
    tril = causal.astype(F32)
    ones = jnp.ones((L, M_DV), BF16)

    for nb in range(MLSTM_SEQS):
        gb = gt_ref[nb] + bias_ref[...]
        ls = _log_sigmoid(gb)
        bcum = jnp.dot(tril, ls, preferred_element_type=F32, precision=lax.Precision.HIGHEST)
        gb_t = gb.T
        bcum_t = bcum.T
        k_t = (k_ref[nb] * (M_DK ** -0.5)).T
        for h in range(M_HEADS):
            sh = nb * M_HEADS + h
            b_col = bcum[:, M_HEADS + h:M_HEADS + h + 1]
            b_row = bcum_t[M_HEADS + h:M_HEADS + h + 1, :]
            i_row = gb_t[h:h + 1, :]
            m_prev = m_scr[nb, h:h + 1, 0:1]
            dmat = jnp.where(causal, b_col + (i_row - b_row), NEG)
            a_col = b_col + m_prev
            mt = jnp.maximum(a_col, jnp.max(dmat, axis=1, keepdims=True))
            w_intra = jnp.exp(dmat - mt)
            w_inter = jnp.exp(a_col - mt)
            q_h = q_ref[nb, :, h * M_DK:(h + 1) * M_DK].astype(BF16)
            kt_h = k_t[h * M_DK:(h + 1) * M_DK, :]
            qk = jnp.dot(q_h, kt_h.astype(BF16), preferred_element_type=F32)
            s_w = (qk * w_intra).astype(BF16)
            v_ext = jnp.concatenate([v_ref[nb, :, h * M_DV:(h + 1) * M_DV].astype(BF16), ones], axis=1)
            state = s_scr[sh]
            inter = jnp.dot(q_h, state.astype(BF16), preferred_element_type=F32)
            tot = w_inter * inter + jnp.dot(s_w, v_ext, preferred_element_type=F32)
            num = tot[:, :M_DV]
            qn = tot[:, M_DV:]
            den = jnp.maximum(jnp.abs(qn), jnp.exp(-mt))
            hh = num / den
            hn = hh * lax.rsqrt(jnp.mean(hh * hh, axis=-1, keepdims=True) + EPS)
            hn = hn * gain_ref[:, h * M_DV:(h + 1) * M_DV]
            out = hn * _sigmoid(o_ref[nb, :, h * M_DV:(h + 1) * M_DV])
            out_ref[nb, :, h * M_DV:(h + 1) * M_DV] = out.astype(out_ref.dtype)
            b_last = b_col[L - 1:L, :]
            m_new = mt[L - 1:L, :]
            g_prev = jnp.exp(b_last + m_prev - m_new)
            g_row = jnp.exp(b_last - b_row + i_row - m_new)
            kg_t = (kt_h * g_row).astype(BF16)
            s_scr[sh] = g_prev * state + jnp.dot(kg_t, v_ext, preferred_element_type=F32)
            m_scr[nb, h:h + 1, :] = jnp.broadcast_to(m_new, (1, LANES))

    @pl.when(c == pl.num_programs(1) - 1)
    def _():
        for nb in range(MLSTM_SEQS):
            for h in range(M_HEADS):
                st = s_scr[nb * M_HEADS + h]
                c_ref[nb, h] = st[:, :M_DV]
                n_ref[nb, h] = st[:, M_DV:]
        m_ref[...] = m_scr[...]


def _mlstm_prompt(z, bias_row, gain_row, batch, seq):
    L = MLSTM_CHUNK
    S = MLSTM_SEQS
    z3 = z.reshape(batch, seq, PROJ_W)
    return pl.pallas_call(
        _mlstm_prompt_kernel,
        out_shape=(
            jax.ShapeDtypeStruct((batch, seq, M_HEADS * M_DV), BF16),
            jax.ShapeDtypeStruct((batch, M_HEADS, M_DK, M_DV), F32),
            jax.ShapeDtypeStruct((batch, M_HEADS, M_DK, M_DV), F32),
            jax.ShapeDtypeStruct((batch, 8, LANES), F32),
        ),
        grid=(batch // S, seq // L),
        in_specs=[
            pl.BlockSpec((S, L, 256), lambda b, c: (b, c, C_MQ // 256)),
            pl.BlockSpec((S, L, 256), lambda b, c: (b, c, C_MK // 256)),
            pl.BlockSpec((S, L, 512), lambda b, c: (b, c, C_MV // 512)),
            pl.BlockSpec((S, L, 512), lambda b, c: (b, c, C_MO // 512)),
            pl.BlockSpec((S, L, LANES), lambda b, c: (b, c, C_GATE // LANES)),
            pl.BlockSpec((1, LANES), lambda b, c: (0, 0)),
            pl.BlockSpec((1, M_HEADS * M_DV), lambda b, c: (0, 0)),
        ],
        out_specs=(
            pl.BlockSpec((S, L, M_HEADS * M_DV), lambda b, c: (b, c, 0)),
            pl.BlockSpec((S, M_HEADS, M_DK, M_DV), lambda b, c: (b, 0, 0, 0)),
            pl.BlockSpec((S, M_HEADS, M_DK, M_DV), lambda b, c: (b, 0, 0, 0)),
            pl.BlockSpec((S, 8, LANES), lambda b, c: (b, 0, 0)),
        ),
        scratch_shapes=[pltpu.VMEM((S * M_HEADS, M_DK, 2 * M_DV), F32), pltpu.VMEM((S, 8, LANES), F32)],
        compiler_params=_cparams(("arbitrary", "arbitrary")),
        name="mlstm_prompt",
    )(z3, z3, z3, z3, z3, bias_row, gain_row)


def _attn_prompt_kernel(relb_ref, sink_ref, q_ref, kp_ref, kc_ref, vp_ref, vc_ref, bucket_ref,
                        out_ref, bias_scr):
    B = ATT_BLOCK
    j = pl.program_id(1)

    @pl.when((pl.program_id(0) == 0) & (j == 0))
    def _():
        bucket = bucket_ref[...]
        for h in range(A_HEADS):
            acc = jnp.full((B, 2 * B), NEG, F32)
            for bk in range(NUM_BUCKETS):
                acc = jnp.where(bucket == bk, relb_ref[bk * A_HEADS + h], acc)
            bias_scr[h] = acc

    scale = A_HD ** -0.5
    s_iota = lax.broadcasted_iota(I32, (B, 2 * B), 1)
    first = jnp.where((s_iota < B) & (j == 0), NEG, 0.0)
    outs = []
    for h in range(A_HEADS):
        g = h // A_GROUP
        q_h = q_ref[:, h * A_HD:(h + 1) * A_HD].astype(BF16)
        k2 = jnp.concatenate([kp_ref[:, g * A_HD:(g + 1) * A_HD], kc_ref[:, g * A_HD:(g + 1) * A_HD]],
                             axis=0).astype(BF16)
        v2 = jnp.concatenate([vp_ref[:, g * A_HD:(g + 1) * A_HD], vc_ref[:, g * A_HD:(g + 1) * A_HD]],
                             axis=0).astype(BF16)
        logits = lax.dot_general(q_h, k2, (((1,), (1,)), ((), ())), preferred_element_type=F32)
        logits = logits * scale + bias_scr[h] + first
        sink = sink_ref[h]
        m = jnp.maximum(jnp.max(logits, axis=-1, keepdims=True), sink)
        p = jnp.exp(logits - m)
        den = jnp.sum(p, axis=-1, keepdims=True) + jnp.exp(sink - m)
        o = jnp.dot(p.astype(BF16), v2, preferred_element_type=F32) / den
        outs.append(o)
    out_ref[...] = jnp.concatenate(outs, axis=1).astype(out_ref.dtype)


def _attn_prompt(z, rel_bias, sinks, batch, seq):
    B = ATT_BLOCK
    nb = seq // B
    qi = np.arange(B)[:, None]
    si = np.arange(2 * B)[None, :]
    dist = qi + B - si
    bucket = np.where((dist >= 0) & (dist <= WINDOW), _t5_bucket_np(dist), -1).astype(np.int32)
    cur = lambda b, j, *_: b * nb + j
    prev = lambda b, j, *_: b * nb + jnp.maximum(j - 1, 0)
    grid_spec = pltpu.PrefetchScalarGridSpec(
        num_scalar_prefetch=2,
        grid=(batch, nb),
        in_specs=[
            pl.BlockSpec((B, 512), lambda b, j, *_: (cur(b, j), C_AQ // 512)),
            pl.BlockSpec((B, LANES), lambda b, j, *_: (prev(b, j), C_AK // LANES)),
            pl.BlockSpec((B, LANES), lambda b, j, *_: (cur(b, j), C_AK // LANES)),
            pl.BlockSpec((B, LANES), lambda b, j, *_: (prev(b, j), C_AV // LANES)),
            pl.BlockSpec((B, LANES), lambda b, j, *_: (cur(b, j), C_AV // LANES)),
            pl.BlockSpec((B, 2 * B), lambda b, j, *_: (0, 0)),
        ],
        out_specs=pl.BlockSpec((B, A_HEADS * A_HD), lambda b, j, *_: (cur(b, j), 0)),
        scratch_shapes=[pltpu.VMEM((A_HEADS, B, 2 * B), F32)],
    )
    return pl.pallas_call(
        _attn_prompt_kernel,
        out_shape=jax.ShapeDtypeStruct((batch * seq, A_HEADS * A_HD), BF16),
        grid_spec=grid_spec,
        compiler_params=_cparams(("arbitrary", "arbitrary")),
        name="attn_prompt",
    )(rel_bias.reshape(-1), sinks, z, z, z, z, z, jnp.asarray(bucket))


def _outproj_router_kernel(x_ref, mo_ref, ao_ref, wm_ref, wa_ref, g_ref, wr_ref, br_ref, cin_ref,
                           y_ref, yn_ref, eidx_ref, gate_ref, rank_ref, cout_ref, carry):
    T = x_ref.shape[0]
    i = pl.program_id(0)

    @pl.when(i == 0)
    def _():
        carry[...] = cin_ref[...]

    y = (x_ref[...] + jnp.dot(mo_ref[...], wm_ref[...], preferred_element_type=F32)
         + jnp.dot(ao_ref[...], wa_ref[...], preferred_element_type=F32))
    y_ref[...] = y
    yn = _rms(y, g_ref[...])
    for s in range(ROW_TILES):
        yn_ref[pl.ds(s, T, stride=ROW_TILES), :] = yn[:, s * LANES:(s + 1) * LANES]
    yh = yn.astype(BF16)
    yl = (yn - yh.astype(F32)).astype(BF16)
    hh = jnp.dot(yh, wr_ref[...], preferred_element_type=F32)
    lh = jnp.dot(yl, wr_ref[:, :LANES], preferred_element_type=F32)
    logits = hh[:, :LANES] + (hh[:, LANES:] + lh) + br_ref[...]
    lane = lax.broadcasted_iota(I32, (T, LANES), 1)
    lane_f = lane.astype(F32)
    vals, idxs, hots = [], [], []
    l = logits
    for _ in range(TOP_K):
        mx = jnp.max(l, axis=-1, keepdims=True)
        idx = jnp.min(jnp.where(l == mx, lane_f, float(LANES)), axis=-1, keepdims=True)
        hot = lane_f == idx
        l = jnp.where(hot, -jnp.inf, l)
        vals.append(mx)
        idxs.append(idx)
        hots.append(hot)
    es = [jnp.exp(v - vals[0]) for v in vals]
    tot = es[0] + es[1] + es[2] + es[3]
    sel = jnp.where(hots[0] | hots[1] | hots[2] | hots[3], 1.0, 0.0)
    row = lax.broadcasted_iota(I32, (T, T), 0)
    col = lax.broadcasted_iota(I32, (T, T), 1)
    strict = (col < row).astype(BF16)
    before = carry[...] + jnp.dot(strict, sel.astype(BF16), preferred_element_type=F32)
    eidx = jnp.zeros((T, LANES), I32)
    gate = jnp.zeros((T, LANES), F32)
    rank = jnp.zeros((T, LANES), I32)
    for k in range(TOP_K):
        r_k = jnp.sum(jnp.where(hots[k], before, 0.0), axis=-1, keepdims=True)
        eidx = jnp.where(lane == k, idxs[k].astype(I32), eidx)
        gate = jnp.where(lane == k, es[k] / tot, gate)
        rank = jnp.where(lane == k, r_k.astype(I32), rank)
    eidx_ref[...] = eidx
    gate_ref[...] = gate
    rank_ref[...] = rank
    carry[...] = carry[...] + jnp.sum(sel, axis=0, keepdims=True)
    cout_ref[...] = carry[...]


def _outproj_router(x2, m_out, a_out, w_m, w_a, norm_row, w_r, b_r, counts_in, tile):
    n = x2.shape[0]
    tok = lambda i: (i, 0)
    fix = lambda i: (0, 0)
    return pl.pallas_call(
        _outproj_router_kernel,
        out_shape=(
            jax.ShapeDtypeStruct((n, D_MODEL), F32),
            jax.ShapeDtypeStruct((n * ROW_TILES, LANES), F32),
            jax.ShapeDtypeStruct((n, LANES), I32),
            jax.ShapeDtypeStruct((n, LANES), F32),
            jax.ShapeDtypeStruct((n, LANES), I32),
            jax.ShapeDtypeStruct((1, LANES), F32),
        ),
        grid=(n // tile,),
        in_specs=[
            pl.BlockSpec((tile, D_MODEL), tok),
            pl.BlockSpec((tile, 512), tok),
            pl.BlockSpec((tile, 512), tok),
            pl.BlockSpec((512, D_MODEL), fix),
            pl.BlockSpec((512, D_MODEL), fix),
            pl.BlockSpec((1, D_MODEL), fix),
            pl.BlockSpec((D_MODEL, 2 * LANES), fix),
            pl.BlockSpec((1, LANES), fix),
            pl.BlockSpec((1, LANES), fix),
        ],
        out_specs=(
            pl.BlockSpec((tile, D_MODEL), tok),
            pl.BlockSpec((tile * ROW_TILES, LANES), tok),
            pl.BlockSpec((tile, LANES), tok),
            pl.BlockSpec((tile, LANES), tok),
            pl.BlockSpec((tile, LANES), tok),
            pl.BlockSpec((1, LANES), fix),
        ),
        scratch_shapes=[pltpu.VMEM((1, LANES), F32)],
        compiler_params=_cparams(("arbitrary",)),
        name="outproj_router",
    )(x2, m_out, a_out, w_m, w_a, norm_row, w_r, b_r, counts_in)


def _dispatch_kernel(dest_ref, tail_ref, yn_ref, *rest, first):
    xs_ref, zero_scr, sem = rest if first else rest[1:]
    T = yn_ref.shape[0]
    i = pl.program_id(0)

    if first:
        @pl.when(i == 0)
        def _():
            zero_scr[...] = jnp.zeros_like(zero_scr)

            def fill(e, carry):
                @pl.when(tail_ref[e] >= 0)
                def _():
                    pltpu.make_async_copy(zero_scr, xs_ref.at[pl.ds(tail_ref[e], FFN_ROWS)], sem).start()
                return carry

            def drain(e, carry):
                @pl.when(tail_ref[e] >= 0)
                def _():
                    pltpu.make_async_copy(zero_scr, xs_ref.at[pl.ds(tail_ref[e], FFN_ROWS)], sem).wait()
                return carry

            lax.fori_loop(0, N_EXPERTS, fill, 0)
            lax.fori_loop(0, N_EXPERTS, drain, 0)

    base = i * (T * TOP_K)

    def issue(tb, carry):
        for u in range(DMA_UNROLL):
            t = tb * DMA_UNROLL + u
            for k in range(TOP_K):
                d = dest_ref[base + t * TOP_K + k]
                pltpu.make_async_copy(yn_ref.at[t], xs_ref.at[d], sem).start(priority=k % 2)
        return carry

    lax.fori_loop(0, T // DMA_UNROLL, issue, 0)
    for k in range(TOP_K):
        pltpu.make_async_copy(yn_ref, xs_ref.at[pl.ds(0, T)], sem).wait()


def _dispatch(dest_flat, tail, yn, xs, n_rows, tile):
    n = yn.shape[0]
    first = xs is None
    in_specs = [pl.BlockSpec((tile, ROW_TILES, LANES), lambda i, *_: (i, 0, 0))]
    operands = [dest_flat, tail, yn]
    if not first:
        in_specs.append(pl.BlockSpec(memory_space=pl.ANY))
        operands.append(xs)
    grid_spec = pltpu.PrefetchScalarGridSpec(
        num_scalar_prefetch=2,
        grid=(n // tile,),
        in_specs=in_specs,
        out_specs=pl.BlockSpec(memory_space=pl.ANY),
        scratch_shapes=[pltpu.VMEM((FFN_ROWS, ROW_TILES, LANES), F32), pltpu.SemaphoreType.DMA],
    )
    return pl.pallas_call(
        functools.partial(_dispatch_kernel, first=first),
        out_shape=jax.ShapeDtypeStruct((n_rows, ROW_TILES, LANES), F32),
        grid_spec=grid_spec,
        input_output_aliases={} if first else {3: 0},
        compiler_params=_cparams(("arbitrary",)),
        name="moe_dispatch_first" if first else "moe_dispatch_more",
    )(*operands)


def _ffn_kernel(be_ref, nused_ref, valid_ref, nxt_ref, slot_ref, xs_ref, wg_hbm, bg_ref, wu_hbm, bu_ref,
                wd_hbm, bd_ref, out_ref, wbuf, wg_bf, wu_bf, wd_bf, h_scr, sem):
    i = pl.program_id(0)
    w_hbm = (wg_hbm, wu_hbm, wd_hbm)
    w_bf = (wg_bf, wu_bf, wd_bf)

    def weight_copies(e, slot):
        return [pltpu.make_async_copy(w_hbm[j].at[e], wbuf.at[slot, j], sem.at[slot, j]) for j in range(3)]

    @pl.when(i < nused_ref[0])
    def _():
        @pl.when((i == 0) | (be_ref[i] != be_ref[jnp.maximum(i - 1, 0)]))
        def _():
            slot = slot_ref[i]

            @pl.when(i == 0)
            def _():
                for cp in weight_copies(be_ref[0], 0):
                    cp.start()

            for cp in weight_copies(be_ref[i], slot):
                cp.wait()

            @pl.when(nxt_ref[i] >= 0)
            def _():
                for cp in weight_copies(nxt_ref[i], 1 - slot):
                    cp.start()

            for j in range(3):
                for r in range(0, D_MODEL, CAST_ROWS):
                    w_bf[j][r:r + CAST_ROWS, :] = wbuf[slot, j, r:r + CAST_ROWS, :].astype(BF16)

        for sb in range(FFN_ROWS // FFN_SUB):
            @pl.when(valid_ref[i] > sb * FFN_SUB)
            def _():
                r0 = sb * FFN_SUB * ROW_TILES
                x = jnp.concatenate([xs_ref[pl.ds(r0 + s, FFN_SUB, stride=ROW_TILES), :].astype(BF16)
                                     for s in range(ROW_TILES)], axis=1)
                for c in range(D_FF // FFN_COLS):
                    cs = slice(c * FFN_COLS, (c + 1) * FFN_COLS)
                    g = jnp.dot(x, wg_bf[:, cs], preferred_element_type=F32) + bg_ref[:, cs]
                    u = jnp.dot(x, wu_bf[:, cs], preferred_element_type=F32) + bu_ref[:, cs]
                    g = jnp.minimum(g, SWIGLU_LIMIT)
                    u = jnp.clip(u, -SWIGLU_LIMIT, SWIGLU_LIMIT)
                    h_scr[:, cs] = ((u + 1.0) * (g * _sigmoid(SWIGLU_ALPHA * g))).astype(BF16)
                out = jnp.dot(h_scr[...], wd_bf[...], preferred_element_type=F32) + bd_ref[...]
                for s in range(ROW_TILES):
                    out_ref[pl.ds(r0 + s, FFN_SUB, stride=ROW_TILES), :] = out[:, s * LANES:(s + 1) * LANES]


def _ffn(block_e, nused, valid, nxt, slot, xs, wg, bg, wu, bu, wd, bd):
    nb = xs.shape[0] // (FFN_ROWS * ROW_TILES)
    blk = lambda i, be, nu, *_: (jnp.minimum(i, nu[0] - 1), 0)
    bsel = lambda i, be, *_: (be[i], 0, 0)
    hbm = pl.BlockSpec(memory_space=pl.ANY)
    grid_spec = pltpu.PrefetchScalarGridSpec(
        num_scalar_prefetch=5,
        grid=(nb,),
        in_specs=[
            pl.BlockSpec((FFN_ROWS * ROW_TILES, LANES), blk),
            hbm,
            pl.BlockSpec((None, 1, D_FF), bsel),
            hbm,
            pl.BlockSpec((None, 1, D_FF), bsel),
            hbm,
            pl.BlockSpec((None, 1, D_MODEL), bsel),
        ],
        out_specs=pl.BlockSpec((FFN_ROWS * ROW_TILES, LANES), blk),
        scratch_shapes=[pltpu.VMEM((2, 3, D_MODEL, D_FF), F32),
                        pltpu.VMEM((D_MODEL, D_FF), BF16), pltpu.VMEM((D_MODEL, D_FF), BF16),
                        pltpu.VMEM((D_FF, D_MODEL), BF16), pltpu.VMEM((FFN_SUB, D_FF), BF16),
                        pltpu.SemaphoreType.DMA((2, 3))],
    )
    return pl.pallas_call(
        _ffn_kernel,
        out_shape=jax.ShapeDtypeStruct(xs.shape, F32),
        grid_spec=grid_spec,
        compiler_params=_cparams(("arbitrary",)),
        name="moe_ffn",
    )(block_e, nused, valid, nxt, slot, xs, wg, bg, wu, bu, wd, bd)


def _combine_kernel(dest_ref, y_ref, gate_ref, fn_ref, ffn_ref, out_ref, buf, sem):
    T = y_ref.shape[0]
    i = pl.program_id(0)
    n = pl.num_programs(0)
    slot = i % 2

    def issue(tile, s):
        base = tile * (T * TOP_K)

        def body(tb, carry):
            for u in range(DMA_UNROLL):
                t = tb * DMA_UNROLL + u
                for k in range(TOP_K):
                    d = dest_ref[base + t * TOP_K + k]
                    pltpu.make_async_copy(ffn_ref.at[d], buf.at[s, k, pl.ds(t * ROW_TILES, ROW_TILES)],
                                          sem.at[s]).start(priority=k % 2)
            return carry

        lax.fori_loop(0, T // DMA_UNROLL, body, 0)

    @pl.when(i == 0)
    def _():
        issue(0, 0)

    @pl.when(i + 1 < n)
    def _():
        issue(i + 1, 1 - slot)

    for k in range(TOP_K):
        pltpu.make_async_copy(buf.at[1 - slot, k], buf.at[slot, k], sem.at[slot]).wait()

    acc = y_ref[...]
    gate = gate_ref[...]
    for k in range(TOP_K):
        rows = jnp.concatenate([buf[slot, k, pl.ds(s, T, stride=ROW_TILES), :] for s in range(ROW_TILES)],
                               axis=1)
        acc = acc + gate[:, k:k + 1] * rows
    out_ref[...] = _rms(acc, fn_ref[...])


def _combine(dest_flat, y, gate, fnorm_row, ffn_out, tile):
    n = y.shape[0]
    grid_spec = pltpu.PrefetchScalarGridSpec(
        num_scalar_prefetch=1,
        grid=(n // tile,),
        in_specs=[
            pl.BlockSpec((tile, D_MODEL), lambda i, *_: (i, 0)),
            pl.BlockSpec((tile, LANES), lambda i, *_: (i, 0)),
            pl.BlockSpec((1, D_MODEL), lambda i, *_: (0, 0)),
            pl.BlockSpec(memory_space=pl.ANY),
        ],
        out_specs=pl.BlockSpec((tile, D_MODEL), lambda i, *_: (i, 0)),
        scratch_shapes=[pltpu.VMEM((2, TOP_K, tile * ROW_TILES, LANES), F32), pltpu.SemaphoreType.DMA((2,))],
    )
    return pl.pallas_call(
        _combine_kernel,
        out_shape=jax.ShapeDtypeStruct((n, D_MODEL), F32),
        grid_spec=grid_spec,
        compiler_params=_cparams(("arbitrary",)),
        name="moe_combine",
    )(dest_flat, y, gate, fnorm_row, ffn_out)


def _mlstm_step_kernel(q_ref, k_ref, v_ref, o_ref, gt_ref, bias_ref, gain_ref, c0_ref, n0_ref, m0_ref,
                       out_ref, c_ref, n_ref, m_ref):
    TB = SAMPLE_MLSTM_TB
    gb = gt_ref[...] + bias_ref[...]
    ls = _log_sigmoid(gb)
    lane = lax.broadcasted_iota(I32, (TB, LANES), 1)
    eye = (lax.broadcasted_iota(I32, (M_DK, M_DK), 0) == lax.broadcasted_iota(I32, (M_DK, M_DK), 1)).astype(F32)
    nt = (((1,), (1,)), ((), ()))
    m_all = jnp.zeros((TB, LANES), F32)
    for h in range(M_HEADS):
        i_pre = gb[:, h:h + 1]
        a = ls[:, M_HEADS + h:M_HEADS + h + 1] + m0_ref[:, h:h + 1]
        mt = jnp.maximum(a, i_pre)
        w_intra = jnp.exp(i_pre - mt)
        w_inter = jnp.exp(a - mt)
        q_h = q_ref[:, h * M_DK:(h + 1) * M_DK]
        k_h = k_ref[:, h * M_DK:(h + 1) * M_DK] * (M_DK ** -0.5)
        v_h = v_ref[:, h * M_DV:(h + 1) * M_DV]
        n0_h = n0_ref[:, h, :]
        s = jnp.sum(q_h * k_h, axis=-1, keepdims=True) * w_intra
        qn = w_inter * jnp.sum(q_h * n0_h, axis=-1, keepdims=True) + s
        den = jnp.maximum(jnp.abs(qn), jnp.exp(-mt))
        q_t = lax.dot_general(eye, q_h, nt, preferred_element_type=F32, precision=lax.Precision.HIGHEST)
        k_t = lax.dot_general(eye, k_h, nt, preferred_element_type=F32, precision=lax.Precision.HIGHEST)
        rows = []
        for b in range(TB):
            c0 = c0_ref[b, h]
            qc = jnp.sum(c0 * q_t[:, b:b + 1], axis=0, keepdims=True)
            v_b = v_h[b:b + 1, :]
            rows.append(w_inter[b:b + 1, :] * qc + s[b:b + 1, :] * v_b)
            c_ref[b, h] = w_inter[b:b + 1, :] * c0 + (w_intra[b:b + 1, :] * k_t[:, b:b + 1]) * v_b
        num = jnp.concatenate(rows, axis=0)
        hh = num / den
        hn = hh * lax.rsqrt(jnp.mean(hh * hh, axis=-1, keepdims=True) + EPS)
        hn = hn * gain_ref[:, h * M_DV:(h + 1) * M_DV]
        out_ref[:, h * M_DV:(h + 1) * M_DV] = (hn * _sigmoid(o_ref[:, h * M_DV:(h + 1) * M_DV])).astype(out_ref.dtype)
        n_ref[:, h * M_DK:(h + 1) * M_DK] = w_inter * n0_h + w_intra * k_h
        m_all = jnp.where(lane == h, mt, m_all)
    m_ref[...] = m_all


def _mlstm_step(zs, bias_row, gain_row, c0, n0, m0):
    TB = SAMPLE_MLSTM_TB
    nb = zs.shape[0]
    tok = lambda i: (i, 0)
    return pl.pallas_call(
        _mlstm_step_kernel,
        out_shape=(
            jax.ShapeDtypeStruct((nb, M_HEADS * M_DV), BF16),
            jax.ShapeDtypeStruct((nb, M_HEADS, M_DK, M_DV), F32),
            jax.ShapeDtypeStruct((nb, M_HEADS * M_DK), F32),
            jax.ShapeDtypeStruct((nb, LANES), F32),
        ),
        grid=(nb // TB,),
        in_specs=[
            pl.BlockSpec((TB, 256), lambda i: (i, C_MQ // 256)),
            pl.BlockSpec((TB, 256), lambda i: (i, C_MK // 256)),
            pl.BlockSpec((TB, 512), lambda i: (i, C_MV // 512)),
            pl.BlockSpec((TB, 512), lambda i: (i, C_MO // 512)),
            pl.BlockSpec((TB, LANES), lambda i: (i, C_GATE // LANES)),
            pl.BlockSpec((1, LANES), lambda i: (0, 0)),
            pl.BlockSpec((1, M_HEADS * M_DV), lambda i: (0, 0)),
            pl.BlockSpec((TB, M_HEADS, M_DK, M_DV), lambda i: (i, 0, 0, 0)),
            pl.BlockSpec((TB, M_HEADS, M_DK), lambda i: (i, 0, 0)),
            pl.BlockSpec((TB, M_HEADS), tok),
        ],
        out_specs=(
            pl.BlockSpec((TB, M_HEADS * M_DV), tok),
            pl.BlockSpec((TB, M_HEADS, M_DK, M_DV), lambda i: (i, 0, 0, 0)),
            pl.BlockSpec((TB, M_HEADS * M_DK), tok),
            pl.BlockSpec((TB, LANES), tok),
        ),
        compiler_params=_cparams(("arbitrary",)),
        name="mlstm_step",
    )(zs, zs, zs, zs, zs, bias_row, gain_row, c0, n0, m0)


def _attn_step_kernel(q_ref, kn_ref, vn_ref, ck_ref, cv_ref, bucket_ref, relt_ref, sink_ref,
                      out_ref, nk_ref, nv_ref, bias_scr):
    TB = SAMPLE_ATT_TB
    W = ck_ref.shape[1]

    @pl.when(pl.program_id(0) == 0)
    def _():
        bucket = jnp.broadcast_to(bucket_ref[...], (A_HEADS, W))
        acc = jnp.zeros((A_HEADS, W), F32)
        for bk in range(NUM_BUCKETS):
            acc = jnp.where(bucket == bk, relt_ref[:, bk:bk + 1], acc)
        bias_scr[...] = acc

    scale = A_HD ** -0.5
    nt = (((1,), (1,)), ((), ()))
    bias = bias_scr[...]
    bias_new = relt_ref[:, 0:1]
    sink = sink_ref[...]
    low = lax.broadcasted_iota(I32, (A_HEADS, 1), 0) < A_GROUP
    for b in range(TB):
        q = q_ref[b]
        qb = q.astype(BF16)
        kc = ck_ref[b]
        vc = cv_ref[b]
        kn = kn_ref[b:b + 1, :]
        vn = vn_ref[b:b + 1, :]
        l0 = lax.dot_general(qb, kc[:, :A_HD].astype(BF16), nt, preferred_element_type=F32)
        l1 = lax.dot_general(qb, kc[:, A_HD:].astype(BF16), nt, preferred_element_type=F32)
        logits = jnp.where(low, l0, l1) * scale + bias
        kn_h = jnp.where(low, kn[:, :A_HD], kn[:, A_HD:])
        vn_h = jnp.where(low, vn[:, :A_HD], vn[:, A_HD:])
        l_new = jnp.sum(q * kn_h, axis=-1, keepdims=True) * scale + bias_new
        m = jnp.maximum(jnp.maximum(jnp.max(logits, axis=-1, keepdims=True), l_new), sink)
        p = jnp.exp(logits - m)
        p_new = jnp.exp(l_new - m)
        den = jnp.sum(p, axis=-1, keepdims=True) + p_new + jnp.exp(sink - m)
        pb = p.astype(BF16)
        o0 = jnp.dot(pb, vc[:, :A_HD].astype(BF16), preferred_element_type=F32)
        o1 = jnp.dot(pb, vc[:, A_HD:].astype(BF16), preferred_element_type=F32)
        o = jnp.where(low, o0, o1) + p_new * vn_h
        out_ref[b] = o / den
        nk_ref[b, 0:W - 1, :] = ck_ref[b, 1:W, :]
        nk_ref[b, W - 1:W, :] = kn
        nv_ref[b, 0:W - 1, :] = cv_ref[b, 1:W, :]
        nv_ref[b, W - 1:W, :] = vn


def _attn_step(q3, k_new, v_new, ck, cv, rel_bias, sinks):
    TB = SAMPLE_ATT_TB
    nb, W = ck.shape[0], ck.shape[1]
    bucket = _t5_bucket_np(W - np.arange(W))[None, :].astype(np.int32)
    tok = lambda i: (i, 0)
    tok3 = lambda i: (i, 0, 0)
    fix = lambda i: (0, 0)
    return pl.pallas_call(
        _attn_step_kernel,
        out_shape=(
            jax.ShapeDtypeStruct((nb, A_HEADS, A_HD), F32),
            jax.ShapeDtypeStruct(ck.shape, F32),
            jax.ShapeDtypeStruct(cv.shape, F32),
        ),
        grid=(nb // TB,),
        in_specs=[
            pl.BlockSpec((TB, A_HEADS, A_HD), tok3),
            pl.BlockSpec((TB, LANES), lambda i: (i, C_AK // LANES)),
            pl.BlockSpec((TB, LANES), lambda i: (i, C_AV // LANES)),
            pl.BlockSpec((TB, W, A_KV * A_HD), tok3),
            pl.BlockSpec((TB, W, A_KV * A_HD), tok3),
            pl.BlockSpec((1, W), fix),
            pl.BlockSpec((A_HEADS, NUM_BUCKETS), fix),
            pl.BlockSpec((A_HEADS, 1), fix),
        ],
        out_specs=(
            pl.BlockSpec((TB, A_HEADS, A_HD), tok3),
            pl.BlockSpec((TB, W, A_KV * A_HD), tok3),
            pl.BlockSpec((TB, W, A_KV * A_HD), tok3),
        ),
        scratch_shapes=[pltpu.VMEM((A_HEADS, W), F32)],
        compiler_params=_cparams(("arbitrary",)),
        name="attn_step",
    )(q3, k_new, v_new, ck, cv, jnp.asarray(bucket), rel_bias.T, sinks.reshape(A_HEADS, 1))


def _reorder_w_in(w_in):
    o = 0
    parts = {}
    for name, width in (("mq", 256), ("mk", 256), ("mv", 512), ("mo", 512), ("mi", 4), ("mf", 4),
                        ("aq", 512), ("ak", 128), ("av", 128)):
        parts[name] = w_in[:, o:o + width]
        o += width
    pad = jnp.zeros((w_in.shape[0], LANES - 2 * M_HEADS), w_in.dtype)
    cols = [parts[n] for n in ("mq", "mk", "mv", "mo", "aq", "ak", "av", "mi", "mf")] + [pad]
    return jnp.concatenate(cols, axis=1).astype(BF16)


def _lane_row(v, fill=0.0):
    return jnp.concatenate([v.astype(F32), jnp.full((LANES - v.shape[0],), fill, F32)])[None, :]


def kernel(x_prompt, x_sample, state_C, state_n, state_m, cache_k, cache_v, rel_bias, norm1, w_in, b_if,
           m_gain, sinks, w_out, norm2, w_router, b_router, w_gate, b_gate, w_up, b_up, w_down, b_down,
           final_norm):
    assert norm1.shape[0] == 1, "single-layer trunk"
    batch, seq, _ = x_prompt.shape
    nsmp = x_sample.shape[0]
    n_p = batch * seq
    W = cache_k.shape[2]

    xp = x_prompt.reshape(n_p, D_MODEL)
    xs_ = x_sample.reshape(nsmp, D_MODEL)
    w_in_r = _reorder_w_in(w_in[0])
    n1 = norm1[0][None, :]
    n2 = norm2[0][None, :]
    fn = final_norm[None, :]
    bias_row = _lane_row(b_if[0])
    gain_row = m_gain[0][None, :]
    w_m = w_out[0][:M_HEADS * M_DV].astype(BF16)
    w_a = w_out[0][M_HEADS * M_DV:].astype(BF16)
    w_r32 = jnp.concatenate([w_router[0], jnp.zeros((D_MODEL, LANES - N_EXPERTS), F32)], axis=1)
    w_r_hi = w_r32.astype(BF16)
    w_r = jnp.concatenate([w_r_hi, (w_r32 - w_r_hi.astype(F32)).astype(BF16)], axis=1)
    b_r = _lane_row(b_router[0], NEG)

    zp = _inproj(xp, n1, w_in_r, TOK_TILE)
    zs = _inproj(xs_, n1, w_in_r, nsmp)
    m_out_p, p_c, p_nrep, p_mrep = _mlstm_prompt(zp, bias_row, gain_row, batch, seq)
    m_out_p = m_out_p.reshape(n_p, M_HEADS * M_DV)
    a_out_p = _attn_prompt(zp, rel_bias, sinks[0], batch, seq)
    m_out_s, s_c, s_n, s_mrep = _mlstm_step(zs, bias_row, gain_row, state_C[0], state_n[0], state_m[0])
    q3 = zs[:, C_AQ:C_AQ + A_HEADS * A_HD].reshape(nsmp, A_HEADS, A_HD)
    a3, s_k, s_v = _attn_step(q3, zs, zs, cache_k[0].reshape(nsmp, W, A_KV * A_HD),
                              cache_v[0].reshape(nsmp, W, A_KV * A_HD), rel_bias, sinks[0])
    a_out_s = a3.reshape(nsmp, A_HEADS * A_HD).astype(BF16)

    zero_counts = jnp.zeros((1, LANES), F32)
    y_p, yn_p, e_p, g_p, r_p, cnt_p = _outproj_router(xp, m_out_p, a_out_p, w_m, w_a, n2, w_r, b_r,
                                                      zero_counts, TOK_TILE)
    y_s, yn_s, e_s, g_s, r_s, cnt = _outproj_router(xs_, m_out_s, a_out_s, w_m, w_a, n2, w_r, b_r,
                                                    cnt_p, nsmp)

    counts = cnt[0, :N_EXPERTS].astype(I32)
    padded = (counts + FFN_ROWS - 1) // FFN_ROWS * FFN_ROWS
    pad_end = jnp.cumsum(padded)
    pad_start = pad_end - padded
    n_rows = ((n_p + nsmp) * TOP_K + N_EXPERTS * (FFN_ROWS - 1) + FFN_ROWS - 1) // FFN_ROWS * FFN_ROWS
    n_blocks = n_rows // FFN_ROWS
    nused = jnp.maximum(pad_end[-1] // FFN_ROWS, 1).astype(I32)
    blk_start = jnp.minimum(jnp.arange(n_blocks, dtype=I32), nused - 1) * FFN_ROWS
    block_e = jnp.minimum(jnp.sum((blk_start[:, None] >= pad_end[None, :]).astype(I32), axis=1), N_EXPERTS - 1)
    tail = jnp.where(counts % FFN_ROWS != 0, pad_end - FFN_ROWS, -1).astype(I32)
    blk_valid = jnp.clip((pad_start + counts)[block_e] - blk_start, 0, FFN_ROWS).astype(I32)
    def slots(e, r):
        hot = e[:, :TOP_K, None] == jnp.arange(N_EXPERTS, dtype=I32)
        return (jnp.sum(jnp.where(hot, pad_start, 0), axis=-1) + r[:, :TOP_K]).reshape(-1).astype(I32)

    dest_p = slots(e_p, r_p)
    dest_s = slots(e_s, r_s)

    as_tiles = lambda a: a.reshape(-1, ROW_TILES, LANES)
    xs_rows = _dispatch(dest_p, tail, as_tiles(yn_p), None, n_rows, DISPATCH_TILE)
    xs_rows = _dispatch(dest_s, tail, as_tiles(yn_s), xs_rows, n_rows, nsmp)
    blk_first = jnp.concatenate([jnp.ones((1,), bool), block_e[1:] != block_e[:-1]])
    blk_slot = ((jnp.cumsum(blk_first.astype(I32)) - 1) % 2).astype(I32)
    ex = jnp.arange(N_EXPERTS, dtype=I32)
    later = jnp.where((ex[None, :] > ex[:, None]) & (counts[None, :] > 0), ex[None, :], N_EXPERTS)
    next_e = jnp.min(later, axis=1)
    blk_next = jnp.where(next_e < N_EXPERTS, next_e, -1)[block_e].astype(I32)
    ffn_out = _ffn(block_e, nused.reshape(1), blk_valid, blk_next, blk_slot, xs_rows.reshape(-1, LANES),
                   w_gate[0], b_gate[0][:, None, :], w_up[0], b_up[0][:, None, :],
                   w_down[0], b_down[0][:, None, :])
    out_p = _combine(dest_p, y_p, g_p, fn, as_tiles(ffn_out), COMBINE_TILE)
    out_s = _combine(dest_s, y_s, g_s, fn, as_tiles(ffn_out), COMBINE_TILE)

    kv_shape = (1, batch, WINDOW, A_KV, A_HD)
    zk = zp[:, C_AK:C_AK + A_KV * A_HD].reshape(batch, seq, A_KV * A_HD)[:, seq - WINDOW:]
    zv = zp[:, C_AV:C_AV + A_KV * A_HD].reshape(batch, seq, A_KV * A_HD)[:, seq - WINDOW:]
    return (
        out_p.reshape(batch, seq, D_MODEL),
        out_s.reshape(nsmp, 1, D_MODEL),
        p_c[None],
        p_nrep[None, :, :, :, 0],
        p_mrep[None, :, :M_HEADS, 0],
        zk.reshape(kv_shape),
        zv.reshape(kv_shape),
        s_c[None],
        s_n.reshape(1, nsmp, M_HEADS, M_DK),
        s_mrep[None, :, :M_HEADS],
        s_k.reshape(1, nsmp, W, A_KV, A_HD),
        s_v.reshape(1, nsmp, W, A_KV, A_HD),
    )
```

```python
import functools
import math

import numpy as np
import jax
import jax.numpy as jnp
from jax import lax
from jax.experimental import pallas as pl
from jax.experimental.pallas import tpu as pltpu

F32 = jnp.float32
BF16 = jnp.bfloat16
I32 = jnp.int32

D_MODEL = 1024
M_HEADS = 4
M_DK = 64
M_DV = 128
A_HEADS = 8
A_KV = 2
A_GROUP = A_HEADS // A_KV
A_HD = 64
WINDOW = 128
NUM_BUCKETS = 32
MAX_DISTANCE = 128
N_EXPERTS = 32
TOP_K = 4
D_FF = 1024
SWIGLU_LIMIT = 7.0
SWIGLU_ALPHA = 1.702
EPS = 1e-5

LANES = 128
NEG = -1e30
VMEM_LIMIT = 52 * 1024 * 1024

C_MQ, C_MK, C_MV, C_MO, C_AQ, C_AK, C_AV, C_GATE = 0, 256, 512, 1024, 1536, 2048, 2176, 2304
PROJ_W = 2432

MLSTM_CHUNK = 128
MLSTM_SEQS = 4
ATT_BLOCK = 128
TOK_TILE = 512
FFN_ROWS = 1024
FFN_SUB = 256
ROW_TILES = D_MODEL // LANES
FFN_COLS = 256
CAST_ROWS = 128
DISPATCH_TILE = 512
COMBINE_TILE = 256
DMA_UNROLL = 8
SAMPLE_MLSTM_TB = 16
SAMPLE_ATT_TB = 8


def _t5_bucket_np(dist):
    n = np.maximum(dist, 0)
    max_exact = NUM_BUCKETS // 2
    ratio = np.log(np.maximum(n, 1).astype(np.float32) / np.float32(max_exact)) / np.float32(
        math.log(MAX_DISTANCE / max_exact))
    large = max_exact + (ratio * np.float32(NUM_BUCKETS - max_exact)).astype(np.int32)
    large = np.minimum(large, NUM_BUCKETS - 1)
    return np.where(n < max_exact, n, large).astype(np.int32)


def _cparams(sem):
    return pltpu.CompilerParams(dimension_semantics=sem, vmem_limit_bytes=VMEM_LIMIT)


def _rms(x, g):
    return x * lax.rsqrt(jnp.mean(x * x, axis=-1, keepdims=True) + EPS) * g


def _log_sigmoid(x):
    return jnp.minimum(x, 0.0) - jnp.log(1.0 + jnp.exp(-jnp.abs(x)))


def _sigmoid(x):
    return 1.0 / (1.0 + jnp.exp(-x))


def _inproj_kernel(x_ref, g_ref, w_ref, z_ref):
    xn = _rms(x_ref[...], g_ref[...]).astype(BF16)
    z_ref[...] = jnp.dot(xn, w_ref[...], preferred_element_type=F32)


def _inproj(x2, norm_row, w_bf16, tile):
    n = x2.shape[0]
    return pl.pallas_call(
        _inproj_kernel,
        out_shape=jax.ShapeDtypeStruct((n, PROJ_W), F32),
        grid=(n // tile,),
        in_specs=[
            pl.BlockSpec((tile, D_MODEL), lambda i: (i, 0)),
            pl.BlockSpec((1, D_MODEL), lambda i: (0, 0)),
            pl.BlockSpec((D_MODEL, PROJ_W), lambda i: (0, 0)),
        ],
        out_specs=pl.BlockSpec((tile, PROJ_W), lambda i: (i, 0)),
        compiler_params=_cparams(("arbitrary",)),
        name="inproj",
    )(x2, norm_row, w_bf16)


def _mlstm_prompt_kernel(q_ref, k_ref, v_ref, o_ref, gt_ref, bias_ref, gain_ref,
                         out_ref, c_ref, n_ref, m_ref, s_scr, m_scr):
    L = MLSTM_CHUNK
    c = pl.program_id(1)

    @pl.when(c == 0)
    def _():
        s_scr[...] = jnp.zeros_like(s_scr)
        m_scr[...] = jnp.zeros_like(m_scr)

    row = lax.broadcasted_iota(I32, (L, L), 0)
    col = lax.broadcasted_iota(I32, (L, L), 1)
    causal = col <= row
    tril = causal.astype(F32)
    ones = jnp.ones((L, M_DV), BF16)

    for nb in range(MLSTM_SEQS):
        gb = gt_ref[nb] + bias_ref[...]
        ls = _log_sigmoid(gb)
        bcum = jnp.dot(tril, ls, preferred_element_type=F32, precision=lax.Precision.HIGHEST)
        gb_t = gb.T
        bcum_t = bcum.T
        k_t = (k_ref[nb] * (M_DK ** -0.5)).T
        for h in range(M_HEADS):
            sh = nb * M_HEADS + h
            b_col = bcum[:, M_HEADS + h:M_HEADS + h + 1]
            b_row = bcum_t[M_HEADS + h:M_HEADS + h + 1, :]
            i_row = gb_t[h:h + 1, :]
            m_prev = m_scr[nb, h:h + 1, 0:1]
            dmat = jnp.where(causal, b_col + (i_row - b_row), NEG)
            a_col = b_col + m_prev
            mt = jnp.maximum(a_col, jnp.max(dmat, axis=1, keepdims=True))
            w_intra = jnp.exp(dmat - mt)
            w_inter = jnp.exp(a_col - mt)
            q_h = q_ref[nb, :, h * M_DK:(h + 1) * M_DK].astype(BF16)
            kt_h = k_t[h * M_DK:(h + 1) * M_DK, :]
            qk = jnp.dot(q_h, kt_h.astype(BF16), preferred_element_type=F32)
            s_w = (qk * w_intra).astype(BF16)
            v_ext = jnp.concatenate([v_ref[nb, :, h * M_DV:(h + 1) * M_DV].astype(BF16), ones], axis=1)
            state = s_scr[sh]
            inter = jnp.dot(q_h, state.astype(BF16), preferred_element_type=F32)
            tot = w_inter * inter + jnp.dot(s_w, v_ext, preferred_element_type=F32)
            num = tot[:, :M_DV]
            qn = tot[:, M_DV:]
            den = jnp.maximum(jnp.abs(qn), jnp.exp(-mt))
            hh = num / den
            hn = hh * lax.rsqrt(jnp.mean(hh * hh, axis=-1, keepdims=True) + EPS)
            hn = hn * gain_ref[:, h * M_DV:(h + 1) * M_DV]
            out = hn * _sigmoid(o_ref[nb, :, h * M_DV:(h + 1) * M_DV])
            out_ref[nb, :, h * M_DV:(h + 1) * M_DV] = out.astype(out_ref.dtype)
            b_last = b_col[L - 1:L, :]
            m_new = mt[L - 1:L, :]
            g_prev = jnp.exp(b_last + m_prev - m_new)
            g_row = jnp.exp(b_last - b_row + i_row - m_new)
            kg_t = (kt_h * g_row).astype(BF16)
            s_scr[sh] = g_prev * state + jnp.dot(kg_t, v_ext, preferred_element_type=F32)
            m_scr[nb, h:h + 1, :] = jnp.broadcast_to(m_new, (1, LANES))

    @pl.when(c == pl.num_programs(1) - 1)
    def _():
        for nb in range(MLSTM_SEQS):
            for h in range(M_HEADS):
                st = s_scr[nb * M_HEADS + h]
                c_ref[nb, h] = st[:, :M_DV]
                n_ref[nb, h] = st[:, M_DV:]
        m_ref[...] = m_scr[...]


def _mlstm_prompt(z, bias_row, gain_row, batch, seq):
    L = MLSTM_CHUNK
    S = MLSTM_SEQS
    z3 = z.reshape(batch, seq, PROJ_W)
    return pl.pallas_call(
        _mlstm_prompt_kernel,
        out_shape=(
            jax.ShapeDtypeStruct((batch, seq, M_HEADS * M_DV), BF16),
            jax.ShapeDtypeStruct((batch, M_HEADS, M_DK, M_DV), F32),
            jax.ShapeDtypeStruct((batch, M_HEADS, M_DK, M_DV), F32),
            jax.ShapeDtypeStruct((batch, 8, LANES), F32),
        ),
        grid=(batch // S, seq // L),
        in_specs=[
            pl.BlockSpec((S, L, 256), lambda b, c: (b, c, C_MQ // 256)),
            pl.BlockSpec((S, L, 256), lambda b, c: (b, c, C_MK // 256)),
            pl.BlockSpec((S, L, 512), lambda b, c: (b, c, C_MV // 512)),
            pl.BlockSpec((S, L, 512), lambda b, c: (b, c, C_MO // 512)),
            pl.BlockSpec((S, L, LANES), lambda b, c: (b, c, C_GATE // LANES)),
            pl.BlockSpec((1, LANES), lambda b, c: (0, 0)),
            pl.BlockSpec((1, M_HEADS * M_DV), lambda b, c: (0, 0)),
        ],
        out_specs=(
            pl.BlockSpec((S, L, M_HEADS * M_DV), lambda b, c: (b, c, 0)),
            pl.BlockSpec((S, M_HEADS, M_DK, M_DV), lambda b, c: (b, 0, 0, 0)),
            pl.BlockSpec((S, M_HEADS, M_DK, M_DV), lambda b, c: (b, 0, 0, 0)),
            pl.BlockSpec((S, 8, LANES), lambda b, c: (b, 0, 0)),
        ),
        scratch_shapes=[pltpu.VMEM((S * M_HEADS, M_DK, 2 * M_DV), F32), pltpu.VMEM((S, 8, LANES), F32)],
        compiler_params=_cparams(("arbitrary", "arbitrary")),
        name="mlstm_prompt",
    )(z3, z3, z3, z3, z3, bias_row, gain_row)


def _attn_prompt_kernel(relb_ref, sink_ref, q_ref, kp_ref, kc_ref, vp_ref, vc_ref, bucket_ref,
                        out_ref, bias_scr):
    B = ATT_BLOCK
    j = pl.program_id(1)

    @pl.when((pl.program_id(0) == 0) & (j == 0))
    def _():
        bucket = bucket_ref[...]
        for h in range(A_HEADS):
            acc = jnp.full((B, 2 * B), NEG, F32)
            for bk in range(NUM_BUCKETS):
                acc = jnp.where(bucket == bk, relb_ref[bk * A_HEADS + h], acc)
            bias_scr[h] = acc

    scale = A_HD ** -0.5
    s_iota = lax.broadcasted_iota(I32, (B, 2 * B), 1)
    first = jnp.where((s_iota < B) & (j == 0), NEG, 0.0)
    outs = []
    for h in range(A_HEADS):
        g = h // A_GROUP
        q_h = q_ref[:, h * A_HD:(h + 1) * A_HD].astype(BF16)
        k2 = jnp.concatenate([kp_ref[:, g * A_HD:(g + 1) * A_HD], kc_ref[:, g * A_HD:(g + 1) * A_HD]],
                             axis=0).astype(BF16)
        v2 = jnp.concatenate([vp_ref[:, g * A_HD:(g + 1) * A_HD], vc_ref[:, g * A_HD:(g + 1) * A_HD]],
                             axis=0).astype(BF16)
        logits = lax.dot_general(q_h, k2, (((1,), (1,)), ((), ())), preferred_element_type=F32)
        logits = logits * scale + bias_scr[h] + first
        sink = sink_ref[h]
        m = jnp.maximum(jnp.max(logits, axis=-1, keepdims=True), sink)
        p = jnp.exp(logits - m)
        den = jnp.sum(p, axis=-1, keepdims=True) + jnp.exp(sink - m)
        o = jnp.dot(p.astype(BF16), v2, preferred_element_type=F32) / den
        outs.append(o)
    out_ref[...] = jnp.concatenate(outs, axis=1).astype(out_ref.dtype)


def _attn_prompt(z, rel_bias, sinks, batch, seq):
    B = ATT_BLOCK
    nb = seq // B
    qi = np.arange(B)[:, None]
    si = np.arange(2 * B)[None, :]
    dist = qi + B - si
    bucket = np.where((dist >= 0) & (dist <= WINDOW), _t5_bucket_np(dist), -1).astype(np.int32)
    cur = lambda b, j, *_: b * nb + j
    prev = lambda b, j, *_: b * nb + jnp.maximum(j - 1, 0)
    grid_spec = pltpu.PrefetchScalarGridSpec(
        num_scalar_prefetch=2,
        grid=(batch, nb),
        in_specs=[
            pl.BlockSpec((B, 512), lambda b, j, *_: (cur(b, j), C_AQ // 512)),
            pl.BlockSpec((B, LANES), lambda b, j, *_: (prev(b, j), C_AK // LANES)),
            pl.BlockSpec((B, LANES), lambda b, j, *_: (cur(b, j), C_AK // LANES)),
            pl.BlockSpec((B, LANES), lambda b, j, *_: (prev(b, j), C_AV // LANES)),
            pl.BlockSpec((B, LANES), lambda b, j, *_: (cur(b, j), C_AV // LANES)),
            pl.BlockSpec((B, 2 * B), lambda b, j, *_: (0, 0)),
        ],
        out_specs=pl.BlockSpec((B, A_HEADS * A_HD), lambda b, j, *_: (cur(b, j), 0)),
        scratch_shapes=[pltpu.VMEM((A_HEADS, B, 2 * B), F32)],
    )
    return pl.pallas_call(
        _attn_prompt_kernel,
        out_shape=jax.ShapeDtypeStruct((batch * seq, A_HEADS * A_HD), BF16),
        grid_spec=grid_spec,
        compiler_params=_cparams(("arbitrary", "arbitrary")),
        name="attn_prompt",
    )(rel_bias.reshape(-1), sinks, z, z, z, z, z, jnp.asarray(bucket))


def _outproj_router_kernel(x_ref, mo_ref, ao_ref, wm_ref, wa_ref, g_ref, wr_ref, br_ref, cin_ref,
                           y_ref, yn_ref, eidx_ref, gate_ref, rank_ref, cout_ref, carry):
    T = x_ref.shape[0]
    i = pl.program_id(0)

    @pl.when(i == 0)
    def _():
        carry[...] = cin_ref[...]

    y = (x_ref[...] + jnp.dot(mo_ref[...], wm_ref[...], preferred_element_type=F32)
         + jnp.dot(ao_ref[...], wa_ref[...], preferred_element_type=F32))
    y_ref[...] = y
    yn = _rms(y, g_ref[...])
    for s in range(ROW_TILES):
        yn_ref[pl.ds(s, T, stride=ROW_TILES), :] = yn[:, s * LANES:(s + 1) * LANES]
    yh = yn.astype(BF16)
    yl = (yn - yh.astype(F32)).astype(BF16)
    hh = jnp.dot(yh, wr_ref[...], preferred_element_type=F32)
    lh = jnp.dot(yl, wr_ref[:, :LANES], preferred_element_type=F32)
    logits = hh[:, :LANES] + (hh[:, LANES:] + lh) + br_ref[...]
    lane = lax.broadcasted_iota(I32, (T, LANES), 1)
    lane_f = lane.astype(F32)
    vals, idxs, hots = [], [], []
    l = logits
    for _ in range(TOP_K):
        mx = jnp.max(l, axis=-1, keepdims=True)
        idx = jnp.min(jnp.where(l == mx, lane_f, float(LANES)), axis=-1, keepdims=True)
        hot = lane_f == idx
        l = jnp.where(hot, -jnp.inf, l)
        vals.append(mx)
        idxs.append(idx)
        hots.append(hot)
    es = [jnp.exp(v - vals[0]) for v in vals]
    tot = es[0] + es[1] + es[2] + es[3]
    sel = jnp.where(hots[0] | hots[1] | hots[2] | hots[3], 1.0, 0.0)
    row = lax.broadcasted_iota(I32, (T, T), 0)
    col = lax.broadcasted_iota(I32, (T, T), 1)
    strict = (col < row).astype(BF16)
    before = carry[...] + jnp.dot(strict, sel.astype(BF16), preferred_element_type=F32)
    eidx = jnp.zeros((T, LANES), I32)
    gate = jnp.zeros((T, LANES), F32)
    rank = jnp.zeros((T, LANES), I32)
    for k in range(TOP_K):
        r_k = jnp.sum(jnp.where(hots[k], before, 0.0), axis=-1, keepdims=True)
        eidx = jnp.where(lane == k, idxs[k].astype(I32), eidx)
        gate = jnp.where(lane == k, es[k] / tot, gate)
        rank = jnp.where(lane == k, r_k.astype(I32), rank)
    eidx_ref[...] = eidx
    gate_ref[...] = gate
    rank_ref[...] = rank
    carry[...] = carry[...] + jnp.sum(sel, axis=0, keepdims=True)
    cout_ref[...] = carry[...]


def _outproj_router(x2, m_out, a_out, w_m, w_a, norm_row, w_r, b_r, counts_in, tile):
    n = x2.shape[0]
    tok = lambda i: (i, 0)
    fix = lambda i: (0, 0)
    return pl.pallas_call(
        _outproj_router_kernel,
        out_shape=(
            jax.ShapeDtypeStruct((n, D_MODEL), F32),
            jax.ShapeDtypeStruct((n * ROW_TILES, LANES), F32),
            jax.ShapeDtypeStruct((n, LANES), I32),
            jax.ShapeDtypeStruct((n, LANES), F32),
            jax.ShapeDtypeStruct((n, LANES), I32),
            jax.ShapeDtypeStruct((1, LANES), F32),
        ),
        grid=(n // tile,),
        in_specs=[
            pl.BlockSpec((tile, D_MODEL), tok),
            pl.BlockSpec((tile, 512), tok),
            pl.BlockSpec((tile, 512), tok),
            pl.BlockSpec((512, D_MODEL), fix),
            pl.BlockSpec((512, D_MODEL), fix),
            pl.BlockSpec((1, D_MODEL), fix),
            pl.BlockSpec((D_MODEL, 2 * LANES), fix),
            pl.BlockSpec((1, LANES), fix),
            pl.BlockSpec((1, LANES), fix),
        ],
        out_specs=(
            pl.BlockSpec((tile, D_MODEL), tok),
            pl.BlockSpec((tile * ROW_TILES, LANES), tok),
            pl.BlockSpec((tile, LANES), tok),
            pl.BlockSpec((tile, LANES), tok),
            pl.BlockSpec((tile, LANES), tok),
            pl.BlockSpec((1, LANES), fix),
        ),
        scratch_shapes=[pltpu.VMEM((1, LANES), F32)],
        compiler_params=_cparams(("arbitrary",)),
        name="outproj_router",
    )(x2, m_out, a_out, w_m, w_a, norm_row, w_r, b_r, counts_in)


def _dispatch_kernel(dest_ref, tail_ref, yn_ref, *rest, first):
    xs_ref, zero_scr, sem = rest if first else rest[1:]
    T = yn_ref.shape[0]
    i = pl.program_id(0)

    if first:
        @pl.when(i == 0)
        def _():
            zero_scr[...] = jnp.zeros_like(zero_scr)

            def fill(e, carry):
                @pl.when(tail_ref[e] >= 0)
                def _():
                    pltpu.make_async_copy(zero_scr, xs_ref.at[pl.ds(tail_ref[e], FFN_SUB)], sem).start()
                return carry

            def drain(e, carry):
                @pl.when(tail_ref[e] >= 0)
                def _():
                    pltpu.make_async_copy(zero_scr, xs_ref.at[pl.ds(tail_ref[e], FFN_SUB)], sem).wait()
                return carry

            lax.fori_loop(0, N_EXPERTS, fill, 0)
            lax.fori_loop(0, N_EXPERTS, drain, 0)

    base = i * (T * TOP_K)

    def issue(tb, carry):
        for u in range(DMA_UNROLL):
            t = tb * DMA_UNROLL + u
            for k in range(TOP_K):
                d = dest_ref[base + t * TOP_K + k]
                pltpu.make_async_copy(yn_ref.at[t], xs_ref.at[d], sem).start(priority=k % 2)
        return carry

    lax.fori_loop(0, T // DMA_UNROLL, issue, 0)
    for k in range(TOP_K):
        pltpu.make_async_copy(yn_ref, xs_ref.at[pl.ds(0, T)], sem).wait()


def _dispatch(dest_flat, tail, yn, xs, n_rows, tile):
    n = yn.shape[0]
    first = xs is None
    in_specs = [pl.BlockSpec((tile, ROW_TILES, LANES), lambda i, *_: (i, 0, 0))]
    operands = [dest_flat, tail, yn]
    if not first:
        in_specs.append(pl.BlockSpec(memory_space=pl.ANY))
        operands.append(xs)
    grid_spec = pltpu.PrefetchScalarGridSpec(
        num_scalar_prefetch=2,
        grid=(n // tile,),
        in_specs=in_specs,
        out_specs=pl.BlockSpec(memory_space=pl.ANY),
        scratch_shapes=[pltpu.VMEM((FFN_SUB, ROW_TILES, LANES), F32), pltpu.SemaphoreType.DMA],
    )
    return pl.pallas_call(
        functools.partial(_dispatch_kernel, first=first),
        out_shape=jax.ShapeDtypeStruct((n_rows, ROW_TILES, LANES), F32),
        grid_spec=grid_spec,
        input_output_aliases={} if first else {3: 0},
        compiler_params=_cparams(("arbitrary",)),
        name="moe_dispatch_first" if first else "moe_dispatch_more",
    )(*operands)


def _ffn_kernel(be_ref, nused_ref, valid_ref, nxt_ref, xs_ref, wg_hbm, bg_ref, wu_hbm, bu_ref,
                wd_hbm, bd_ref, out_ref, wbuf, wg_bf, wu_bf, wd_bf, h_scr, sem):
    i = pl.program_id(0)
    w_hbm = (wg_hbm, wu_hbm, wd_hbm)
    w_bf = (wg_bf, wu_bf, wd_bf)

    def weight_copies(e):
        return [pltpu.make_async_copy(w_hbm[j].at[e], wbuf.at[j], sem.at[j]) for j in range(3)]

    @pl.when(i < nused_ref[0])
    def _():
        @pl.when((i == 0) | (be_ref[i] != be_ref[jnp.maximum(i - 1, 0)]))
        def _():
            @pl.when(i == 0)
            def _():
                for cp in weight_copies(be_ref[0]):
                    cp.start()

            for cp in weight_copies(be_ref[i]):
                cp.wait()

            for j in range(3):
                for r in range(0, D_MODEL, CAST_ROWS):
                    w_bf[j][r:r + CAST_ROWS, :] = wbuf[j, r:r + CAST_ROWS, :].astype(BF16)

            @pl.when(nxt_ref[i] >= 0)
            def _():
                for cp in weight_copies(nxt_ref[i]):
                    cp.start()

        def ffn_pass(rows):
            x = jnp.concatenate([xs_ref[pl.ds(s, rows, stride=ROW_TILES), :].astype(BF16)
                                 for s in range(ROW_TILES)], axis=1)
            for c in range(D_FF // FFN_COLS):
                cs = slice(c * FFN_COLS, (c + 1) * FFN_COLS)
                g = jnp.dot(x, wg_bf[:, cs], preferred_element_type=F32) + bg_ref[:, cs]
                u = jnp.dot(x, wu_bf[:, cs], preferred_element_type=F32) + bu_ref[:, cs]
                g = jnp.minimum(g, SWIGLU_LIMIT)
                u = jnp.clip(u, -SWIGLU_LIMIT, SWIGLU_LIMIT)
                h_scr[0:rows, cs] = ((u + 1.0) * (g * _sigmoid(SWIGLU_ALPHA * g))).astype(BF16)
            out = jnp.dot(h_scr[0:rows, :], wd_bf[...], preferred_element_type=F32) + bd_ref[...]
            for s in range(ROW_TILES):
                out_ref[pl.ds(s, rows, stride=ROW_TILES), :] = out[:, s * LANES:(s + 1) * LANES]

        for rows in range(FFN_SUB, FFN_ROWS + 1, FFN_SUB):
            @pl.when((valid_ref[i] > rows - FFN_SUB) & (valid_ref[i] <= rows))
            def _():
                ffn_pass(rows)


def _ffn(block_e, nused, valid, nxt, xs, wg, bg, wu, bu, wd, bd):
    nb = xs.shape[0] // (FFN_ROWS * ROW_TILES)
    blk = lambda i, be, nu, *_: (jnp.minimum(i, nu[0] - 1), 0)
    bsel = lambda i, be, *_: (be[i], 0, 0)
    hbm = pl.BlockSpec(memory_space=pl.ANY)
    grid_spec = pltpu.PrefetchScalarGridSpec(
        num_scalar_prefetch=4,
        grid=(nb,),
        in_specs=[
            pl.BlockSpec((FFN_ROWS * ROW_TILES, LANES), blk),
            hbm,
            pl.BlockSpec((None, 1, D_FF), bsel),
            hbm,
            pl.BlockSpec((None, 1, D_FF), bsel),
            hbm,
            pl.BlockSpec((None, 1, D_MODEL), bsel),
        ],
        out_specs=pl.BlockSpec((FFN_ROWS * ROW_TILES, LANES), blk),
        scratch_shapes=[pltpu.VMEM((3, D_MODEL, D_FF), F32),
                        pltpu.VMEM((D_MODEL, D_FF), BF16), pltpu.VMEM((D_MODEL, D_FF), BF16),
                        pltpu.VMEM((D_FF, D_MODEL), BF16), pltpu.VMEM((FFN_ROWS, D_FF), BF16),
                        pltpu.SemaphoreType.DMA((3,))],
    )
    return pl.pallas_call(
        _ffn_kernel,
        out_shape=jax.ShapeDtypeStruct(xs.shape, F32),
        grid_spec=grid_spec,
        compiler_params=_cparams(("arbitrary",)),
        name="moe_ffn",
    )(block_e, nused, valid, nxt, xs, wg, bg, wu, bu, wd, bd)


def _combine_kernel(dest_ref, y_ref, gate_ref, fn_ref, ffn_ref, out_ref, buf, sem):
    T = y_ref.shape[0]
    i = pl.program_id(0)
    n = pl.num_programs(0)
    slot = i % 2

    def issue(tile, s):
        base = tile * (T * TOP_K)

        def body(tb, carry):
            for u in range(DMA_UNROLL):
                t = tb * DMA_UNROLL + u
                for k in range(TOP_K):
                    d = dest_ref[base + t * TOP_K + k]
                    pltpu.make_async_copy(ffn_ref.at[d], buf.at[s, k, pl.ds(t * ROW_TILES, ROW_TILES)],
                                          sem.at[s]).start(priority=k % 2)
            return carry

        lax.fori_loop(0, T // DMA_UNROLL, body, 0)

    @pl.when(i == 0)
    def _():
        issue(0, 0)

    @pl.when(i + 1 < n)
    def _():
        issue(i + 1, 1 - slot)

    for k in range(TOP_K):
        pltpu.make_async_copy(buf.at[1 - slot, k], buf.at[slot, k], sem.at[slot]).wait()

    acc = y_ref[...]
    gate = gate_ref[...]
    for k in range(TOP_K):
        rows = jnp.concatenate([buf[slot, k, pl.ds(s, T, stride=ROW_TILES), :] for s in range(ROW_TILES)],
                               axis=1)
        acc = acc + gate[:, k:k + 1] * rows
    out_ref[...] = _rms(acc, fn_ref[...])


def _combine(dest_flat, y, gate, fnorm_row, ffn_out, tile):
    n = y.shape[0]
    grid_spec = pltpu.PrefetchScalarGridSpec(
        num_scalar_prefetch=1,
        grid=(n // tile,),
        in_specs=[
            pl.BlockSpec((tile, D_MODEL), lambda i, *_: (i, 0)),
            pl.BlockSpec((tile, LANES), lambda i, *_: (i, 0)),
            pl.BlockSpec((1, D_MODEL), lambda i, *_: (0, 0)),
            pl.BlockSpec(memory_space=pl.ANY),
        ],
        out_specs=pl.BlockSpec((tile, D_MODEL), lambda i, *_: (i, 0)),
        scratch_shapes=[pltpu.VMEM((2, TOP_K, tile * ROW_TILES, LANES), F32), pltpu.SemaphoreType.DMA((2,))],
    )
    return pl.pallas_call(
        _combine_kernel,
        out_shape=jax.ShapeDtypeStruct((n, D_MODEL), F32),
        grid_spec=grid_spec,
        compiler_params=_cparams(("arbitrary",)),
        name="moe_combine",
    )(dest_flat, y, gate, fnorm_row, ffn_out)


def _mlstm_step_kernel(q_ref, k_ref, v_ref, o_ref, gt_ref, bias_ref, gain_ref, c0_ref, n0_ref, m0_ref,
                       out_ref, c_ref, n_ref, m_ref):
    TB = SAMPLE_MLSTM_TB
    gb = gt_ref[...] + bias_ref[...]
    ls = _log_sigmoid(gb)
    lane = lax.broadcasted_iota(I32, (TB, LANES), 1)
    eye = (lax.broadcasted_iota(I32, (M_DK, M_DK), 0) == lax.broadcasted_iota(I32, (M_DK, M_DK), 1)).astype(F32)
    nt = (((1,), (1,)), ((), ()))
    m_all = jnp.zeros((TB, LANES), F32)
    for h in range(M_HEADS):
        i_pre = gb[:, h:h + 1]
        a = ls[:, M_HEADS + h:M_HEADS + h + 1] + m0_ref[:, h:h + 1]
        mt = jnp.maximum(a, i_pre)
        w_intra = jnp.exp(i_pre - mt)
        w_inter = jnp.exp(a - mt)
        q_h = q_ref[:, h * M_DK:(h + 1) * M_DK]
        k_h = k_ref[:, h * M_DK:(h + 1) * M_DK] * (M_DK ** -0.5)
        v_h = v_ref[:, h * M_DV:(h + 1) * M_DV]
        n0_h = n0_ref[:, h, :]
        s = jnp.sum(q_h * k_h, axis=-1, keepdims=True) * w_intra
        qn = w_inter * jnp.sum(q_h * n0_h, axis=-1, keepdims=True) + s
        den = jnp.maximum(jnp.abs(qn), jnp.exp(-mt))
        q_t = lax.dot_general(eye, q_h, nt, preferred_element_type=F32, precision=lax.Precision.HIGHEST)
        k_t = lax.dot_general(eye, k_h, nt, preferred_element_type=F32, precision=lax.Precision.HIGHEST)
        rows = []
        for b in range(TB):
            c0 = c0_ref[b, h]
            qc = jnp.sum(c0 * q_t[:, b:b + 1], axis=0, keepdims=True)
            v_b = v_h[b:b + 1, :]
            rows.append(w_inter[b:b + 1, :] * qc + s[b:b + 1, :] * v_b)
            c_ref[b, h] = w_inter[b:b + 1, :] * c0 + (w_intra[b:b + 1, :] * k_t[:, b:b + 1]) * v_b
        num = jnp.concatenate(rows, axis=0)
        hh = num / den
        hn = hh * lax.rsqrt(jnp.mean(hh * hh, axis=-1, keepdims=True) + EPS)
        hn = hn * gain_ref[:, h * M_DV:(h + 1) * M_DV]
        out_ref[:, h * M_DV:(h + 1) * M_DV] = (hn * _sigmoid(o_ref[:, h * M_DV:(h + 1) * M_DV])).astype(out_ref.dtype)
        n_ref[:, h * M_DK:(h + 1) * M_DK] = w_inter * n0_h + w_intra * k_h
        m_all = jnp.where(lane == h, mt, m_all)
    m_ref[...] = m_all


def _mlstm_step(zs, bias_row, gain_row, c0, n0, m0):
    TB = SAMPLE_MLSTM_TB
    nb = zs.shape[0]
    tok = lambda i: (i, 0)
    return pl.pallas_call(
        _mlstm_step_kernel,
        out_shape=(
            jax.ShapeDtypeStruct((nb, M_HEADS * M_DV), BF16),
            jax.ShapeDtypeStruct((nb, M_HEADS, M_DK, M_DV), F32),
            jax.ShapeDtypeStruct((nb, M_HEADS * M_DK), F32),
            jax.ShapeDtypeStruct((nb, LANES), F32),
        ),
        grid=(nb // TB,),
        in_specs=[
            pl.BlockSpec((TB, 256), lambda i: (i, C_MQ // 256)),
            pl.BlockSpec((TB, 256), lambda i: (i, C_MK // 256)),
            pl.BlockSpec((TB, 512), lambda i: (i, C_MV // 512)),
            pl.BlockSpec((TB, 512), lambda i: (i, C_MO // 512)),
            pl.BlockSpec((TB, LANES), lambda i: (i, C_GATE // LANES)),
            pl.BlockSpec((1, LANES), lambda i: (0, 0)),
            pl.BlockSpec((1, M_HEADS * M_DV), lambda i: (0, 0)),
            pl.BlockSpec((TB, M_HEADS, M_DK, M_DV), lambda i: (i, 0, 0, 0)),
            pl.BlockSpec((TB, M_HEADS, M_DK), lambda i: (i, 0, 0)),
            pl.BlockSpec((TB, M_HEADS), tok),
        ],
        out_specs=(
            pl.BlockSpec((TB, M_HEADS * M_DV), tok),
            pl.BlockSpec((TB, M_HEADS, M_DK, M_DV), lambda i: (i, 0, 0, 0)),
            pl.BlockSpec((TB, M_HEADS * M_DK), tok),
            pl.BlockSpec((TB, LANES), tok),
        ),
        compiler_params=_cparams(("arbitrary",)),
        name="mlstm_step",
    )(zs, zs, zs, zs, zs, bias_row, gain_row, c0, n0, m0)


def _attn_step_kernel(q_ref, kn_ref, vn_ref, ck_ref, cv_ref, bucket_ref, relt_ref, sink_ref,
                      out_ref, nk_ref, nv_ref, bias_scr):
    TB = SAMPLE_ATT_TB
    W = ck_ref.shape[1]

    @pl.when(pl.program_id(0) == 0)
    def _():
        bucket = jnp.broadcast_to(bucket_ref[...], (A_HEADS, W))
        acc = jnp.zeros((A_HEADS, W), F32)
        for bk in range(NUM_BUCKETS):
            acc = jnp.where(bucket == bk, relt_ref[:, bk:bk + 1], acc)
        bias_scr[...] = acc

    scale = A_HD ** -0.5
    nt = (((1,), (1,)), ((), ()))
    bias = bias_scr[...]
    bias_new = relt_ref[:, 0:1]
    sink = sink_ref[...]
    low = lax.broadcasted_iota(I32, (A_HEADS, 1), 0) < A_GROUP
    for b in range(TB):
        q = q_ref[b]
        qb = q.astype(BF16)
        kc = ck_ref[b]
        vc = cv_ref[b]
        kn = kn_ref[b:b + 1, :]
        vn = vn_ref[b:b + 1, :]
        l0 = lax.dot_general(qb, kc[:, :A_HD].astype(BF16), nt, preferred_element_type=F32)
        l1 = lax.dot_general(qb, kc[:, A_HD:].astype(BF16), nt, preferred_element_type=F32)
        logits = jnp.where(low, l0, l1) * scale + bias
        kn_h = jnp.where(low, kn[:, :A_HD], kn[:, A_HD:])
        vn_h = jnp.where(low, vn[:, :A_HD], vn[:, A_HD:])
        l_new = jnp.sum(q * kn_h, axis=-1, keepdims=True) * scale + bias_new
        m = jnp.maximum(jnp.maximum(jnp.max(logits, axis=-1, keepdims=True), l_new), sink)
        p = jnp.exp(logits - m)
        p_new = jnp.exp(l_new - m)
        den = jnp.sum(p, axis=-1, keepdims=True) + p_new + jnp.exp(sink - m)
        pb = p.astype(BF16)
        o0 = jnp.dot(pb, vc[:, :A_HD].astype(BF16), preferred_element_type=F32)
        o1 = jnp.dot(pb, vc[:, A_HD:].astype(BF16), preferred_element_type=F32)
        o = jnp.where(low, o0, o1) + p_new * vn_h
        out_ref[b] = o / den
        nk_ref[b, 0:W - 1, :] = ck_ref[b, 1:W, :]
        nk_ref[b, W - 1:W, :] = kn
        nv_ref[b, 0:W - 1, :] = cv_ref[b, 1:W, :]
        nv_ref[b, W - 1:W, :] = vn


def _attn_step(q3, k_new, v_new, ck, cv, rel_bias, sinks):
    TB = SAMPLE_ATT_TB
    nb, W = ck.shape[0], ck.shape[1]
    bucket = _t5_bucket_np(W - np.arange(W))[None, :].astype(np.int32)
    tok = lambda i: (i, 0)
    tok3 = lambda i: (i, 0, 0)
    fix = lambda i: (0, 0)
    return pl.pallas_call(
        _attn_step_kernel,
        out_shape=(
            jax.ShapeDtypeStruct((nb, A_HEADS, A_HD), F32),
            jax.ShapeDtypeStruct(ck.shape, F32),
            jax.ShapeDtypeStruct(cv.shape, F32),
        ),
        grid=(nb // TB,),
        in_specs=[
            pl.BlockSpec((TB, A_HEADS, A_HD), tok3),
            pl.BlockSpec((TB, LANES), lambda i: (i, C_AK // LANES)),
            pl.BlockSpec((TB, LANES), lambda i: (i, C_AV // LANES)),
            pl.BlockSpec((TB, W, A_KV * A_HD), tok3),
            pl.BlockSpec((TB, W, A_KV * A_HD), tok3),
            pl.BlockSpec((1, W), fix),
            pl.BlockSpec((A_HEADS, NUM_BUCKETS), fix),
            pl.BlockSpec((A_HEADS, 1), fix),
        ],
        out_specs=(
            pl.BlockSpec((TB, A_HEADS, A_HD), tok3),
            pl.BlockSpec((TB, W, A_KV * A_HD), tok3),
            pl.BlockSpec((TB, W, A_KV * A_HD), tok3),
        ),
        scratch_shapes=[pltpu.VMEM((A_HEADS, W), F32)],
        compiler_params=_cparams(("arbitrary",)),
        name="attn_step",
    )(q3, k_new, v_new, ck, cv, jnp.asarray(bucket), rel_bias.T, sinks.reshape(A_HEADS, 1))


def _reorder_w_in(w_in):
    o = 0
    parts = {}
    for name, width in (("mq", 256), ("mk", 256), ("mv", 512), ("mo", 512), ("mi", 4), ("mf", 4),
                        ("aq", 512), ("ak", 128), ("av", 128)):
        parts[name] = w_in[:, o:o + width]
        o += width
    pad = jnp.zeros((w_in.shape[0], LANES - 2 * M_HEADS), w_in.dtype)
    cols = [parts[n] for n in ("mq", "mk", "mv", "mo", "aq", "ak", "av", "mi", "mf")] + [pad]
    return jnp.concatenate(cols, axis=1).astype(BF16)


def _lane_row(v, fill=0.0):
    return jnp.concatenate([v.astype(F32), jnp.full((LANES - v.shape[0],), fill, F32)])[None, :]


def kernel(x_prompt, x_sample, state_C, state_n, state_m, cache_k, cache_v, rel_bias, norm1, w_in, b_if,
           m_gain, sinks, w_out, norm2, w_router, b_router, w_gate, b_gate, w_up, b_up, w_down, b_down,
           final_norm):
    assert norm1.shape[0] == 1, "single-layer trunk"
    batch, seq, _ = x_prompt.shape
    nsmp = x_sample.shape[0]
    n_p = batch * seq
    W = cache_k.shape[2]

    xp = x_prompt.reshape(n_p, D_MODEL)
    xs_ = x_sample.reshape(nsmp, D_MODEL)
    w_in_r = _reorder_w_in(w_in[0])
    n1 = norm1[0][None, :]
    n2 = norm2[0][None, :]
    fn = final_norm[None, :]
    bias_row = _lane_row(b_if[0])
    gain_row = m_gain[0][None, :]
    w_m = w_out[0][:M_HEADS * M_DV].astype(BF16)
    w_a = w_out[0][M_HEADS * M_DV:].astype(BF16)
    w_r32 = jnp.concatenate([w_router[0], jnp.zeros((D_MODEL, LANES - N_EXPERTS), F32)], axis=1)
    w_r_hi = w_r32.astype(BF16)
    w_r = jnp.concatenate([w_r_hi, (w_r32 - w_r_hi.astype(F32)).astype(BF16)], axis=1)
    b_r = _lane_row(b_router[0], NEG)

    zp = _inproj(xp, n1, w_in_r, TOK_TILE)
    zs = _inproj(xs_, n1, w_in_r, nsmp)
    m_out_p, p_c, p_nrep, p_mrep = _mlstm_prompt(zp, bias_row, gain_row, batch, seq)
    m_out_p = m_out_p.reshape(n_p, M_HEADS * M_DV)
    a_out_p = _attn_prompt(zp, rel_bias, sinks[0], batch, seq)
    m_out_s, s_c, s_n, s_mrep = _mlstm_step(zs, bias_row, gain_row, state_C[0], state_n[0], state_m[0])
    q3 = zs[:, C_AQ:C_AQ + A_HEADS * A_HD].reshape(nsmp, A_HEADS, A_HD)
    a3, s_k, s_v = _attn_step(q3, zs, zs, cache_k[0].reshape(nsmp, W, A_KV * A_HD),
                              cache_v[0].reshape(nsmp, W, A_KV * A_HD), rel_bias, sinks[0])
    a_out_s = a3.reshape(nsmp, A_HEADS * A_HD).astype(BF16)

    zero_counts = jnp.zeros((1, LANES), F32)
    y_p, yn_p, e_p, g_p, r_p, cnt_p = _outproj_router(xp, m_out_p, a_out_p, w_m, w_a, n2, w_r, b_r,
                                                      zero_counts, TOK_TILE)
    y_s, yn_s, e_s, g_s, r_s, cnt = _outproj_router(xs_, m_out_s, a_out_s, w_m, w_a, n2, w_r, b_r,
                                                    cnt_p, nsmp)

    counts = cnt[0, :N_EXPERTS].astype(I32)
    padded = (counts + FFN_ROWS - 1) // FFN_ROWS * FFN_ROWS
    pad_end = jnp.cumsum(padded)
    pad_start = pad_end - padded
    n_rows = ((n_p + nsmp) * TOP_K + N_EXPERTS * (FFN_ROWS - 1) + FFN_ROWS - 1) // FFN_ROWS * FFN_ROWS
    n_blocks = n_rows // FFN_ROWS
    nused = jnp.maximum(pad_end[-1] // FFN_ROWS, 1).astype(I32)
    blk_start = jnp.minimum(jnp.arange(n_blocks, dtype=I32), nused - 1) * FFN_ROWS
    block_e = jnp.minimum(jnp.sum((blk_start[:, None] >= pad_end[None, :]).astype(I32), axis=1), N_EXPERTS - 1)
    tail = jnp.where(counts % FFN_SUB != 0, pad_start + counts // FFN_SUB * FFN_SUB, -1).astype(I32)
    blk_valid = jnp.clip((pad_start + counts)[block_e] - blk_start, 0, FFN_ROWS).astype(I32)
    def slots(e, r):
        hot = e[:, :TOP_K, None] == jnp.arange(N_EXPERTS, dtype=I32)
        return (jnp.sum(jnp.where(hot, pad_start, 0), axis=-1) + r[:, :TOP_K]).reshape(-1).astype(I32)

    dest_p = slots(e_p, r_p)
    dest_s = slots(e_s, r_s)

    as_tiles = lambda a: a.reshape(-1, ROW_TILES, LANES)
    xs_rows = _dispatch(dest_p, tail, as_tiles(yn_p), None, n_rows, DISPATCH_TILE)
    xs_rows = _dispatch(dest_s, tail, as_tiles(yn_s), xs_rows, n_rows, nsmp)
    ex = jnp.arange(N_EXPERTS, dtype=I32)
    later = jnp.where((ex[None, :] > ex[:, None]) & (counts[None, :] > 0), ex[None, :], N_EXPERTS)
    next_e = jnp.min(later, axis=1)
    blk_next = jnp.where(next_e < N_EXPERTS, next_e, -1)[block_e].astype(I32)
    ffn_out = _ffn(block_e, nused.reshape(1), blk_valid, blk_next, xs_rows.reshape(-1, LANES),
                   w_gate[0], b_gate[0][:, None, :], w_up[0], b_up[0][:, None, :],
                   w_down[0], b_down[0][:, None, :])
    out_p = _combine(dest_p, y_p, g_p, fn, as_tiles(ffn_out), COMBINE_TILE)
    out_s = _combine(dest_s, y_s, g_s, fn, as_tiles(ffn_out), min(COMBINE_TILE, nsmp))

    kv_shape = (1, batch, WINDOW, A_KV, A_HD)
    zk = zp[:, C_AK:C_AK + A_KV * A_HD].reshape(batch, seq, A_KV * A_HD)[:, seq - WINDOW:]
    zv = zp[:, C_AV:C_AV + A_KV * A_HD].reshape(batch, seq, A_KV * A_HD)[:, seq - WINDOW:]
    return (
        out_p.reshape(batch, seq, D_MODEL),
        out_s.reshape(nsmp, 1, D_MODEL),
        p_c[None],
        p_nrep[None, :, :, :, 0],
        p_mrep[None, :, :M_HEADS, 0],
        zk.reshape(kv_shape),
        zv.reshape(kv_shape),
        s_c[None],
        s_n.reshape(1, nsmp, M_HEADS, M_DK),
        s_mrep[None, :, :M_HEADS],
        s_k.reshape(1, nsmp, W, A_KV, A_HD),
        s_v.reshape(1, nsmp, W, A_KV, A_HD),
    )
```

```python
import functools
import math

import numpy as np
import jax
import jax.numpy as jnp
from jax import lax
from jax.experimental import pallas as pl
from jax.experimental.pallas import tpu as pltpu

F32 = jnp.float32
BF16 = jnp.bfloat16
I32 = jnp.int32

D_MODEL = 1024
M_HEADS = 4
M_DK = 64
M_DV = 128
A_HEADS = 8
A_KV = 2
A_GROUP = A_HEADS // A_KV
A_HD = 64
WINDOW = 128
NUM_BUCKETS = 32
MAX_DISTANCE = 128
N_EXPERTS = 32
TOP_K = 4
D_FF = 1024
SWIGLU_LIMIT = 7.0
SWIGLU_ALPHA = 1.702
EPS = 1e-5

LANES = 128
NEG = -1e30
VMEM_LIMIT = 52 * 1024 * 1024

C_MQ, C_MK, C_MV, C_MO, C_AQ, C_AK, C_AV, C_GATE = 0, 256, 512, 1024, 1536, 2048, 2176, 2304
PROJ_W = 2432

MLSTM_CHUNK = 128
MLSTM_SEQS = 4
ATT_BLOCK = 128
TOK_TILE = 512
FFN_ROWS = 512
FFN_SUB = 256
ROW_TILES = D_MODEL // LANES
FFN_COLS = 256
CAST_ROWS = 128
COMBINE_TILE = 256
DMA_UNROLL = 8
SAMPLE_MLSTM_TB = 16
SAMPLE_ATT_TB = 8


def _t5_bucket_np(dist):
    n = np.maximum(dist, 0)
    max_exact = NUM_BUCKETS // 2
    ratio = np.log(np.maximum(n, 1).astype(np.float32) / np.float32(max_exact)) / np.float32(
        math.log(MAX_DISTANCE / max_exact))
    large = max_exact + (ratio * np.float32(NUM_BUCKETS - max_exact)).astype(np.int32)
    large = np.minimum(large, NUM_BUCKETS - 1)
    return np.where(n < max_exact, n, large).astype(np.int32)


def _cparams(sem):
    return pltpu.CompilerParams(dimension_semantics=sem, vmem_limit_bytes=VMEM_LIMIT)


def _rms(x, g):
    return x * lax.rsqrt(jnp.mean(x * x, axis=-1, keepdims=True) + EPS) * g


def _log_sigmoid(x):
    return jnp.minimum(x, 0.0) - jnp.log(1.0 + jnp.exp(-jnp.abs(x)))


def _sigmoid(x):
    return 1.0 / (1.0 + jnp.exp(-x))


def _inproj_kernel(x_ref, g_ref, w_ref, z_ref):
    xn = _rms(x_ref[...], g_ref[...]).astype(BF16)
    z_ref[...] = jnp.dot(xn, w_ref[...], preferred_element_type=F32)


def _inproj(x2, norm_row, w_bf16, tile):
    n = x2.shape[0]
    return pl.pallas_call(
        _inproj_kernel,
        out_shape=jax.ShapeDtypeStruct((n, PROJ_W), F32),
        grid=(n // tile,),
        in_specs=[
            pl.BlockSpec((tile, D_MODEL), lambda i: (i, 0)),
            pl.BlockSpec((1, D_MODEL), lambda i: (0, 0)),
            pl.BlockSpec((D_MODEL, PROJ_W), lambda i: (0, 0)),
        ],
        out_specs=pl.BlockSpec((tile, PROJ_W), lambda i: (i, 0)),
        compiler_params=_cparams(("arbitrary",)),
        name="inproj",
    )(x2, norm_row, w_bf16)


def _mlstm_prompt_kernel(q_ref, k_ref, v_ref, o_ref, gt_ref, bias_ref, gain_ref,
                         out_ref, c_ref, n_ref, m_ref, s_scr, m_scr):
    L = MLSTM_CHUNK
    c = pl.program_id(1)

    @pl.when(c == 0)
    def _():
        s_scr[...] = jnp.zeros_like(s_scr)
        m_scr[...] = jnp.zeros_like(m_scr)

    row = lax.broadcasted_iota(I32, (L, L), 0)
    col = lax.broadcasted_iota(I32, (L, L), 1)
    causal = col <= row
    tril = causal.astype(F32)
    ones = jnp.ones((L, M_DV), BF16)

    for nb in range(MLSTM_SEQS):
        gb = gt_ref[nb] + bias_ref[...]
        ls = _log_sigmoid(gb)
        bcum = jnp.dot(tril, ls, preferred_element_type=F32, precision=lax.Precision.HIGHEST)
        gb_t = gb.T
        bcum_t = bcum.T
        k_t = (k_ref[nb] * (M_DK ** -0.5)).T
        for h in range(M_HEADS):
            sh = nb * M_HEADS + h
            b_col = bcum[:, M_HEADS + h:M_HEADS + h + 1]
            b_row = bcum_t[M_HEADS + h:M_HEADS + h + 1, :]
            i_row = gb_t[h:h + 1, :]
            m_prev = m_scr[nb, h:h + 1, 0:1]
            dmat = jnp.where(causal, b_col + (i_row - b_row), NEG)
            a_col = b_col + m_prev
            mt = jnp.maximum(a_col, jnp.max(dmat, axis=1, keepdims=True))
            w_intra = jnp.exp(dmat - mt)
            w_inter = jnp.exp(a_col - mt)
            q_h = q_ref[nb, :, h * M_DK:(h + 1) * M_DK].astype(BF16)
            kt_h = k_t[h * M_DK:(h + 1) * M_DK, :]
            qk = jnp.dot(q_h, kt_h.astype(BF16), preferred_element_type=F32)
            s_w = (qk * w_intra).astype(BF16)
            v_ext = jnp.concatenate([v_ref[nb, :, h * M_DV:(h + 1) * M_DV].astype(BF16), ones], axis=1)
            state = s_scr[sh]
            inter = jnp.dot(q_h, state.astype(BF16), preferred_element_type=F32)
            tot = w_inter * inter + jnp.dot(s_w, v_ext, preferred_element_type=F32)
            num = tot[:, :M_DV]
            qn = tot[:, M_DV:]
            den = jnp.maximum(jnp.abs(qn), jnp.exp(-mt))
            hh = num / den
            hn = hh * lax.rsqrt(jnp.mean(hh * hh, axis=-1, keepdims=True) + EPS)
            hn = hn * gain_ref[:, h * M_DV:(h + 1) * M_DV]
            out = hn * _sigmoid(o_ref[nb, :, h * M_DV:(h + 1) * M_DV])
            out_ref[nb, :, h * M_DV:(h + 1) * M_DV] = out.astype(out_ref.dtype)
            b_last = b_col[L - 1:L, :]
            m_new = mt[L - 1:L, :]
            g_prev = jnp.exp(b_last + m_prev - m_new)
            g_row = jnp.exp(b_last - b_row + i_row - m_new)
            kg_t = (kt_h * g_row).astype(BF16)
            s_scr[sh] = g_prev * state + jnp.dot(kg_t, v_ext, preferred_element_type=F32)
            m_scr[nb, h:h + 1, :] = jnp.broadcast_to(m_new, (1, LANES))

    @pl.when(c == pl.num_programs(1) - 1)
    def _():
        for nb in range(MLSTM_SEQS):
            for h in range(M_HEADS):
                st = s_scr[nb * M_HEADS + h]
                c_ref[nb, h] = st[:, :M_DV]
                n_ref[nb, h] = st[:, M_DV:]
        m_ref[...] = m_scr[...]


def _mlstm_prompt(z, bias_row, gain_row, batch, seq):
    L = MLSTM_CHUNK
    S = MLSTM_SEQS
    z3 = z.reshape(batch, seq, PROJ_W)
    return pl.pallas_call(
        _mlstm_prompt_kernel,
        out_shape=(
            jax.ShapeDtypeStruct((batch, seq, M_HEADS * M_DV), BF16),
            jax.ShapeDtypeStruct((batch, M_HEADS, M_DK, M_DV), F32),
            jax.ShapeDtypeStruct((batch, M_HEADS, M_DK, M_DV), F32),
            jax.ShapeDtypeStruct((batch, 8, LANES), F32),
        ),
        grid=(batch // S, seq // L),
        in_specs=[
            pl.BlockSpec((S, L, 256), lambda b, c: (b, c, C_MQ // 256)),
            pl.BlockSpec((S, L, 256), lambda b, c: (b, c, C_MK // 256)),
            pl.BlockSpec((S, L, 512), lambda b, c: (b, c, C_MV // 512)),
            pl.BlockSpec((S, L, 512), lambda b, c: (b, c, C_MO // 512)),
            pl.BlockSpec((S, L, LANES), lambda b, c: (b, c, C_GATE // LANES)),
            pl.BlockSpec((1, LANES), lambda b, c: (0, 0)),
            pl.BlockSpec((1, M_HEADS * M_DV), lambda b, c: (0, 0)),
        ],
        out_specs=(
            pl.BlockSpec((S, L, M_HEADS * M_DV), lambda b, c: (b, c, 0)),
            pl.BlockSpec((S, M_HEADS, M_DK, M_DV), lambda b, c: (b, 0, 0, 0)),
            pl.BlockSpec((S, M_HEADS, M_DK, M_DV), lambda b, c: (b, 0, 0, 0)),
            pl.BlockSpec((S, 8, LANES), lambda b, c: (b, 0, 0)),
        ),
        scratch_shapes=[pltpu.VMEM((S * M_HEADS, M_DK, 2 * M_DV), F32), pltpu.VMEM((S, 8, LANES), F32)],
        compiler_params=_cparams(("arbitrary", "arbitrary")),
        name="mlstm_prompt",
    )(z3, z3, z3, z3, z3, bias_row, gain_row)


def _attn_prompt_kernel(relb_ref, sink_ref, q_ref, kp_ref, kc_ref, vp_ref, vc_ref, bucket_ref,
                        out_ref, bias_scr):
    B = ATT_BLOCK
    j = pl.program_id(1)

    @pl.when((pl.program_id(0) == 0) & (j == 0))
    def _():
        bucket = bucket_ref[...]
        for h in range(A_HEADS):
            acc = jnp.full((B, 2 * B), NEG, F32)
            for bk in range(NUM_BUCKETS):
                acc = jnp.where(bucket == bk, relb_ref[bk * A_HEADS + h], acc)
            bias_scr[h] = acc

    scale = A_HD ** -0.5
    s_iota = lax.broadcasted_iota(I32, (B, 2 * B), 1)
    first = jnp.where((s_iota < B) & (j == 0), NEG, 0.0)
    outs = []
    for h in range(A_HEADS):
        g = h // A_GROUP
        q_h = q_ref[:, h * A_HD:(h + 1) * A_HD].astype(BF16)
        k2 = jnp.concatenate([kp_ref[:, g * A_HD:(g + 1) * A_HD], kc_ref[:, g * A_HD:(g + 1) * A_HD]],
                             axis=0).astype(BF16)
        v2 = jnp.concatenate([vp_ref[:, g * A_HD:(g + 1) * A_HD], vc_ref[:, g * A_HD:(g + 1) * A_HD]],
                             axis=0).astype(BF16)
        logits = lax.dot_general(q_h, k2, (((1,), (1,)), ((), ())), preferred_element_type=F32)
        logits = logits * scale + bias_scr[h] + first
        sink = sink_ref[h]
        m = jnp.maximum(jnp.max(logits, axis=-1, keepdims=True), sink)
        p = jnp.exp(logits - m)
        den = jnp.sum(p, axis=-1, keepdims=True) + jnp.exp(sink - m)
        o = jnp.dot(p.astype(BF16), v2, preferred_element_type=F32) / den
        outs.append(o)
    out_ref[...] = jnp.concatenate(outs, axis=1).astype(out_ref.dtype)


def _attn_prompt(z, rel_bias, sinks, batch, seq):
    B = ATT_BLOCK
    nb = seq // B
    qi = np.arange(B)[:, None]
    si = np.arange(2 * B)[None, :]
    dist = qi + B - si
    bucket = np.where((dist >= 0) & (dist <= WINDOW), _t5_bucket_np(dist), -1).astype(np.int32)
    cur = lambda b, j, *_: b * nb + j
    prev = lambda b, j, *_: b * nb + jnp.maximum(j - 1, 0)
    grid_spec = pltpu.PrefetchScalarGridSpec(
        num_scalar_prefetch=2,
        grid=(batch, nb),
        in_specs=[
            pl.BlockSpec((B, 512), lambda b, j, *_: (cur(b, j), C_AQ // 512)),
            pl.BlockSpec((B, LANES), lambda b, j, *_: (prev(b, j), C_AK // LANES)),
            pl.BlockSpec((B, LANES), lambda b, j, *_: (cur(b, j), C_AK // LANES)),
            pl.BlockSpec((B, LANES), lambda b, j, *_: (prev(b, j), C_AV // LANES)),
            pl.BlockSpec((B, LANES), lambda b, j, *_: (cur(b, j), C_AV // LANES)),
            pl.BlockSpec((B, 2 * B), lambda b, j, *_: (0, 0)),
        ],
        out_specs=pl.BlockSpec((B, A_HEADS * A_HD), lambda b, j, *_: (cur(b, j), 0)),
        scratch_shapes=[pltpu.VMEM((A_HEADS, B, 2 * B), F32)],
    )
    return pl.pallas_call(
        _attn_prompt_kernel,
        out_shape=jax.ShapeDtypeStruct((batch * seq, A_HEADS * A_HD), BF16),
        grid_spec=grid_spec,
        compiler_params=_cparams(("arbitrary", "arbitrary")),
        name="attn_prompt",
    )(rel_bias.reshape(-1), sinks, z, z, z, z, z, jnp.asarray(bucket))


def _outproj_router_kernel(x_ref, mo_ref, ao_ref, wm_ref, wa_ref, g_ref, wr_ref, br_ref, cin_ref, *rest):
    y_ref, yn_ref, eidx_ref, gate_ref, rank_ref, cout_ref, carry = rest[-7:]
    T = x_ref.shape[0]
    i = pl.program_id(0)

    @pl.when(i == 0)
    def _():
        carry[...] = cin_ref[...]

    y = (x_ref[...] + jnp.dot(mo_ref[...], wm_ref[...], preferred_element_type=F32)
         + jnp.dot(ao_ref[...], wa_ref[...], preferred_element_type=F32))
    y_ref[...] = y
    yn = _rms(y, g_ref[...])
    for s in range(ROW_TILES):
        yn_ref[pl.ds(s, T, stride=ROW_TILES), :] = yn[:, s * LANES:(s + 1) * LANES]
    yh = yn.astype(BF16)
    yl = (yn - yh.astype(F32)).astype(BF16)
    hh = jnp.dot(yh, wr_ref[...], preferred_element_type=F32)
    lh = jnp.dot(yl, wr_ref[:, :LANES], preferred_element_type=F32)
    logits = hh[:, :LANES] + (hh[:, LANES:] + lh) + br_ref[...]
    lane = lax.broadcasted_iota(I32, (T, LANES), 1)
    lane_f = lane.astype(F32)
    vals, idxs, hots = [], [], []
    l = logits
    for _ in range(TOP_K):
        mx = jnp.max(l, axis=-1, keepdims=True)
        idx = jnp.min(jnp.where(l == mx, lane_f, float(LANES)), axis=-1, keepdims=True)
        hot = lane_f == idx
        l = jnp.where(hot, -jnp.inf, l)
        vals.append(mx)
        idxs.append(idx)
        hots.append(hot)
    es = [jnp.exp(v - vals[0]) for v in vals]
    tot = es[0] + es[1] + es[2] + es[3]
    sel = jnp.where(hots[0] | hots[1] | hots[2] | hots[3], 1.0, 0.0)
    row = lax.broadcasted_iota(I32, (T, T), 0)
    col = lax.broadcasted_iota(I32, (T, T), 1)
    strict = (col < row).astype(BF16)
    before = carry[...] + jnp.dot(strict, sel.astype(BF16), preferred_element_type=F32)
    eidx = jnp.zeros((T, LANES), I32)
    gate = jnp.zeros((T, LANES), F32)
    rank = jnp.zeros((T, LANES), I32)
    for k in range(TOP_K):
        r_k = jnp.sum(jnp.where(hots[k], before, 0.0), axis=-1, keepdims=True)
        eidx = jnp.where(lane == k, idxs[k].astype(I32), eidx)
        gate = jnp.where(lane == k, es[k] / tot, gate)
        rank = jnp.where(lane == k, r_k.astype(I32), rank)
    eidx_ref[...] = eidx
    gate_ref[...] = gate
    rank_ref[...] = rank
    carry[...] = carry[...] + jnp.sum(sel, axis=0, keepdims=True)
    cout_ref[...] = carry[...]


def _outproj_router(x2, m_out, a_out, w_m, w_a, norm_row, w_r, b_r, counts_in, tile, yn_rows, yn_prev=None):
    n = x2.shape[0]
    tok = lambda i: (i, 0)
    fix = lambda i: (0, 0)
    assert yn_prev is None or (yn_rows - n) % tile == 0
    row0 = 0 if yn_prev is None else (yn_rows - n) // tile
    extra_specs = [] if yn_prev is None else [pl.BlockSpec(memory_space=pl.ANY)]
    extra_args = [] if yn_prev is None else [yn_prev]
    return pl.pallas_call(
        _outproj_router_kernel,
        out_shape=(
            jax.ShapeDtypeStruct((n, D_MODEL), F32),
            jax.ShapeDtypeStruct((yn_rows * ROW_TILES, LANES), F32),
            jax.ShapeDtypeStruct((n, LANES), I32),
            jax.ShapeDtypeStruct((n, LANES), F32),
            jax.ShapeDtypeStruct((n, LANES), I32),
            jax.ShapeDtypeStruct((1, LANES), F32),
        ),
        grid=(n // tile,),
        in_specs=[
            pl.BlockSpec((tile, D_MODEL), tok),
            pl.BlockSpec((tile, 512), tok),
            pl.BlockSpec((tile, 512), tok),
            pl.BlockSpec((512, D_MODEL), fix),
            pl.BlockSpec((512, D_MODEL), fix),
            pl.BlockSpec((1, D_MODEL), fix),
            pl.BlockSpec((D_MODEL, 2 * LANES), fix),
            pl.BlockSpec((1, LANES), fix),
            pl.BlockSpec((1, LANES), fix),
        ] + extra_specs,
        out_specs=(
            pl.BlockSpec((tile, D_MODEL), tok),
            pl.BlockSpec((tile * ROW_TILES, LANES), lambda i: (row0 + i, 0)),
            pl.BlockSpec((tile, LANES), tok),
            pl.BlockSpec((tile, LANES), tok),
            pl.BlockSpec((tile, LANES), tok),
            pl.BlockSpec((1, LANES), fix),
        ),
        scratch_shapes=[pltpu.VMEM((1, LANES), F32)],
        input_output_aliases={} if yn_prev is None else {9: 1},
        compiler_params=_cparams(("arbitrary",)),
        name="outproj_router",
    )(x2, m_out, a_out, w_m, w_a, norm_row, w_r, b_r, counts_in, *extra_args)


def _ffn_kernel(be_ref, nused_ref, valid_ref, nxt_ref, cs_ref, gcs_ref, order_ref,
                yn_hbm, wg_hbm, bg_ref, wu_hbm, bu_ref, wd_hbm, bd_ref, out_ref,
                wbuf, wg_bf, wu_bf, wd_bf, h_scr, xbuf0, xbuf1, wsem, gsem):
    i = pl.program_id(0)
    slot = i % 2
    R = FFN_ROWS
    w_hbm = (wg_hbm, wu_hbm, wd_hbm)
    w_bf = (wg_bf, wu_bf, wd_bf)
    xbuf = (xbuf0, xbuf1)

    def weight_copies(e):
        return [pltpu.make_async_copy(w_hbm[j].at[e], wbuf.at[j], wsem.at[j]) for j in range(3)]

    def gather_row(c0, r, dst_slot):
        return pltpu.make_async_copy(yn_hbm.at[order_ref[c0 + r]],
                                     xbuf[dst_slot].at[pl.ds(r * ROW_TILES, ROW_TILES)], gsem.at[dst_slot])

    def wait_rows(s):
        pltpu.make_async_copy(xbuf[1 - s], xbuf[s], gsem.at[s]).wait()

    def by_parity(fn):
        for parity in range(2):
            @pl.when(slot == parity)
            def _():
                fn(parity)

    @pl.when(i < nused_ref[0])
    def _():
        @pl.when((i == 0) | (be_ref[i] != be_ref[jnp.maximum(i - 1, 0)]))
        def _():
            @pl.when(i == 0)
            def _():
                for cp in weight_copies(be_ref[0]):
                    cp.start()

            for cp in weight_copies(be_ref[i]):
                cp.wait()

            for j in range(3):
                for r in range(0, D_MODEL, CAST_ROWS):
                    w_bf[j][r:r + CAST_ROWS, :] = wbuf[j, r:r + CAST_ROWS, :].astype(BF16)

            @pl.when(nxt_ref[i] >= 0)
            def _():
                for cp in weight_copies(nxt_ref[i]):
                    cp.start()

        @pl.when(i == 0)
        def _():
            def body(rb, carry):
                for u in range(DMA_UNROLL):
                    gather_row(cs_ref[0], rb * DMA_UNROLL + u, 0).start(priority=u % 2)
                return carry
            lax.fori_loop(0, R // DMA_UNROLL, body, 0)

        by_parity(wait_rows)

        def ffn_pass(rows, slot):
            x = jnp.concatenate([xbuf[slot][pl.ds(s, rows, stride=ROW_TILES), :].astype(BF16)
                                 for s in range(ROW_TILES)], axis=1)
            g_c0 = gcs_ref[i]
            for r in range(R):
                gather_row(g_c0, r, 1 - slot).start(priority=r % 2)
            for c in range(D_FF // FFN_COLS):
                cs = slice(c * FFN_COLS, (c + 1) * FFN_COLS)
                g = jnp.dot(x, wg_bf[:, cs], preferred_element_type=F32) + bg_ref[:, cs]
                u = jnp.dot(x, wu_bf[:, cs], preferred_element_type=F32) + bu_ref[:, cs]
                g = jnp.minimum(g, SWIGLU_LIMIT)
                u = jnp.clip(u, -SWIGLU_LIMIT, SWIGLU_LIMIT)
                h_scr[0:rows, cs] = ((u + 1.0) * (g * _sigmoid(SWIGLU_ALPHA * g))).astype(BF16)
            out = jnp.dot(h_scr[0:rows, :], wd_bf[...], preferred_element_type=F32) + bd_ref[...]
            for s in range(ROW_TILES):
                out_ref[pl.ds(s, rows, stride=ROW_TILES), :] = out[:, s * LANES:(s + 1) * LANES]

        for parity in range(2):
            for rows in range(FFN_SUB, R + 1, FFN_SUB):
                @pl.when((slot == parity) & (valid_ref[i] > rows - FFN_SUB) & (valid_ref[i] <= rows))
                def _():
                    ffn_pass(rows, parity)

        @pl.when(i == nused_ref[0] - 1)
        def _():
            by_parity(lambda s: wait_rows(1 - s))


def _ffn(plan, order, yn_tiles, wg, bg, wu, bu, wd, bd, n_blocks):
    blk = lambda i, be, nu, *_: (jnp.minimum(i, nu[0] - 1), 0)
    bsel = lambda i, be, *_: (be[i], 0, 0)
    hbm = pl.BlockSpec(memory_space=pl.ANY)
    grid_spec = pltpu.PrefetchScalarGridSpec(
        num_scalar_prefetch=7,
        grid=(n_blocks,),
        in_specs=[
            hbm,
            hbm,
            pl.BlockSpec((None, 1, D_FF), bsel),
            hbm,
            pl.BlockSpec((None, 1, D_FF), bsel),
            hbm,
            pl.BlockSpec((None, 1, D_MODEL), bsel),
        ],
        out_specs=pl.BlockSpec((FFN_ROWS * ROW_TILES, LANES), blk),
        scratch_shapes=[pltpu.VMEM((3, D_MODEL, D_FF), F32),
                        pltpu.VMEM((D_MODEL, D_FF), BF16), pltpu.VMEM((D_MODEL, D_FF), BF16),
                        pltpu.VMEM((D_FF, D_MODEL), BF16), pltpu.VMEM((FFN_ROWS, D_FF), BF16),
                        pltpu.VMEM((FFN_ROWS * ROW_TILES, LANES), F32),
                        pltpu.VMEM((FFN_ROWS * ROW_TILES, LANES), F32),
                        pltpu.SemaphoreType.DMA((3,)), pltpu.SemaphoreType.DMA((2,))],
    )
    return pl.pallas_call(
        _ffn_kernel,
        out_shape=jax.ShapeDtypeStruct((n_blocks * FFN_ROWS * ROW_TILES, LANES), F32),
        grid_spec=grid_spec,
        compiler_params=_cparams(("arbitrary",)),
        name="moe_ffn",
    )(*plan, order, yn_tiles, wg, bg, wu, bu, wd, bd)


def _combine_kernel(dest_ref, y_ref, gate_ref, fn_ref, ffn_ref, out_ref, buf, sem):
    T = y_ref.shape[0]
    i = pl.program_id(0)
    n = pl.num_programs(0)
    slot = i % 2

    def issue(tile, s):
        base = tile * (T * TOP_K)

        def body(tb, carry):
            for u in range(DMA_UNROLL):
                t = tb * DMA_UNROLL + u
                for k in range(TOP_K):
                    d = dest_ref[base + t * TOP_K + k]
                    pltpu.make_async_copy(ffn_ref.at[d], buf.at[s, k, pl.ds(t * ROW_TILES, ROW_TILES)],
                                          sem.at[s]).start(priority=k % 2)
            return carry

        lax.fori_loop(0, T // DMA_UNROLL, body, 0)

    @pl.when(i == 0)
    def _():
        issue(0, 0)

    @pl.when(i + 1 < n)
    def _():
        issue(i + 1, 1 - slot)

    for k in range(TOP_K):
        pltpu.make_async_copy(buf.at[1 - slot, k], buf.at[slot, k], sem.at[slot]).wait()

    acc = y_ref[...]
    gate = gate_ref[...]
    for k in range(TOP_K):
        rows = jnp.concatenate([buf[slot, k, pl.ds(s, T, stride=ROW_TILES), :] for s in range(ROW_TILES)],
                               axis=1)
        acc = acc + gate[:, k:k + 1] * rows
    out_ref[...] = _rms(acc, fn_ref[...])


def _combine(dest_flat, y, gate, fnorm_row, ffn_out, tile):
    n = y.shape[0]
    grid_spec = pltpu.PrefetchScalarGridSpec(
        num_scalar_prefetch=1,
        grid=(n // tile,),
        in_specs=[
            pl.BlockSpec((tile, D_MODEL), lambda i, *_: (i, 0)),
            pl.BlockSpec((tile, LANES), lambda i, *_: (i, 0)),
            pl.BlockSpec((1, D_MODEL), lambda i, *_: (0, 0)),
            pl.BlockSpec(memory_space=pl.ANY),
        ],
        out_specs=pl.BlockSpec((tile, D_MODEL), lambda i, *_: (i, 0)),
        scratch_shapes=[pltpu.VMEM((2, TOP_K, tile * ROW_TILES, LANES), F32), pltpu.SemaphoreType.DMA((2,))],
    )
    return pl.pallas_call(
        _combine_kernel,
        out_shape=jax.ShapeDtypeStruct((n, D_MODEL), F32),
        grid_spec=grid_spec,
        compiler_params=_cparams(("arbitrary",)),
        name="moe_combine",
    )(dest_flat, y, gate, fnorm_row, ffn_out)


def _mlstm_step_kernel(q_ref, k_ref, v_ref, o_ref, gt_ref, bias_ref, gain_ref, c0_ref, n0_ref, m0_ref,
                       out_ref, c_ref, n_ref, m_ref):
    TB = SAMPLE_MLSTM_TB
    gb = gt_ref[...] + bias_ref[...]
    ls = _log_sigmoid(gb)
    lane = lax.broadcasted_iota(I32, (TB, LANES), 1)
    eye = (lax.broadcasted_iota(I32, (M_DK, M_DK), 0) == lax.broadcasted_iota(I32, (M_DK, M_DK), 1)).astype(F32)
    nt = (((1,), (1,)), ((), ()))
    m_all = jnp.zeros((TB, LANES), F32)
    for h in range(M_HEADS):
        i_pre = gb[:, h:h + 1]
        a = ls[:, M_HEADS + h:M_HEADS + h + 1] + m0_ref[:, h:h + 1]
        mt = jnp.maximum(a, i_pre)
        w_intra = jnp.exp(i_pre - mt)
        w_inter = jnp.exp(a - mt)
        q_h = q_ref[:, h * M_DK:(h + 1) * M_DK]
        k_h = k_ref[:, h * M_DK:(h + 1) * M_DK] * (M_DK ** -0.5)
        v_h = v_ref[:, h * M_DV:(h + 1) * M_DV]
        n0_h = n0_ref[:, h, :]
        s = jnp.sum(q_h * k_h, axis=-1, keepdims=True) * w_intra
        qn = w_inter * jnp.sum(q_h * n0_h, axis=-1, keepdims=True) + s
        den = jnp.maximum(jnp.abs(qn), jnp.exp(-mt))
        q_t = lax.dot_general(eye, q_h, nt, preferred_element_type=F32, precision=lax.Precision.HIGHEST)
        k_t = lax.dot_general(eye, k_h, nt, preferred_element_type=F32, precision=lax.Precision.HIGHEST)
        rows = []
        for b in range(TB):
            c0 = c0_ref[b, h]
            qc = jnp.sum(c0 * q_t[:, b:b + 1], axis=0, keepdims=True)
            v_b = v_h[b:b + 1, :]
            rows.append(w_inter[b:b + 1, :] * qc + s[b:b + 1, :] * v_b)
            c_ref[b, h] = w_inter[b:b + 1, :] * c0 + (w_intra[b:b + 1, :] * k_t[:, b:b + 1]) * v_b
        num = jnp.concatenate(rows, axis=0)
        hh = num / den
        hn = hh * lax.rsqrt(jnp.mean(hh * hh, axis=-1, keepdims=True) + EPS)
        hn = hn * gain_ref[:, h * M_DV:(h + 1) * M_DV]
        out_ref[:, h * M_DV:(h + 1) * M_DV] = (hn * _sigmoid(o_ref[:, h * M_DV:(h + 1) * M_DV])).astype(out_ref.dtype)
        n_ref[:, h * M_DK:(h + 1) * M_DK] = w_inter * n0_h + w_intra * k_h
        m_all = jnp.where(lane == h, mt, m_all)
    m_ref[...] = m_all


def _mlstm_step(zs, bias_row, gain_row, c0, n0, m0):
    TB = SAMPLE_MLSTM_TB
    nb = zs.shape[0]
    tok = lambda i: (i, 0)
    return pl.pallas_call(
        _mlstm_step_kernel,
        out_shape=(
            jax.ShapeDtypeStruct((nb, M_HEADS * M_DV), BF16),
            jax.ShapeDtypeStruct((nb, M_HEADS, M_DK, M_DV), F32),
            jax.ShapeDtypeStruct((nb, M_HEADS * M_DK), F32),
            jax.ShapeDtypeStruct((nb, LANES), F32),
        ),
        grid=(nb // TB,),
        in_specs=[
            pl.BlockSpec((TB, 256), lambda i: (i, C_MQ // 256)),
            pl.BlockSpec((TB, 256), lambda i: (i, C_MK // 256)),
            pl.BlockSpec((TB, 512), lambda i: (i, C_MV // 512)),
            pl.BlockSpec((TB, 512), lambda i: (i, C_MO // 512)),
            pl.BlockSpec((TB, LANES), lambda i: (i, C_GATE // LANES)),
            pl.BlockSpec((1, LANES), lambda i: (0, 0)),
            pl.BlockSpec((1, M_HEADS * M_DV), lambda i: (0, 0)),
            pl.BlockSpec((TB, M_HEADS, M_DK, M_DV), lambda i: (i, 0, 0, 0)),
            pl.BlockSpec((TB, M_HEADS, M_DK), lambda i: (i, 0, 0)),
            pl.BlockSpec((TB, M_HEADS), tok),
        ],
        out_specs=(
            pl.BlockSpec((TB, M_HEADS * M_DV), tok),
            pl.BlockSpec((TB, M_HEADS, M_DK, M_DV), lambda i: (i, 0, 0, 0)),
            pl.BlockSpec((TB, M_HEADS * M_DK), tok),
            pl.BlockSpec((TB, LANES), tok),
        ),
        compiler_params=_cparams(("arbitrary",)),
        name="mlstm_step",
    )(zs, zs, zs, zs, zs, bias_row, gain_row, c0, n0, m0)


def _attn_step_kernel(q_ref, kn_ref, vn_ref, ck_ref, cv_ref, bucket_ref, relt_ref, sink_ref,
                      out_ref, nk_ref, nv_ref, bias_scr):
    TB = SAMPLE_ATT_TB
    W = ck_ref.shape[1]

    @pl.when(pl.program_id(0) == 0)
    def _():
        bucket = jnp.broadcast_to(bucket_ref[...], (A_HEADS, W))
        acc = jnp.zeros((A_HEADS, W), F32)
        for bk in range(NUM_BUCKETS):
            acc = jnp.where(bucket == bk, relt_ref[:, bk:bk + 1], acc)
        bias_scr[...] = acc

    scale = A_HD ** -0.5
    nt = (((1,), (1,)), ((), ()))
    bias = bias_scr[...]
    bias_new = relt_ref[:, 0:1]
    sink = sink_ref[...]
    low = lax.broadcasted_iota(I32, (A_HEADS, 1), 0) < A_GROUP
    for b in range(TB):
        q = q_ref[b]
        qb = q.astype(BF16)
        kc = ck_ref[b]
        vc = cv_ref[b]
        kn = kn_ref[b:b + 1, :]
        vn = vn_ref[b:b + 1, :]
        l0 = lax.dot_general(qb, kc[:, :A_HD].astype(BF16), nt, preferred_element_type=F32)
        l1 = lax.dot_general(qb, kc[:, A_HD:].astype(BF16), nt, preferred_element_type=F32)
        logits = jnp.where(low, l0, l1) * scale + bias
        kn_h = jnp.where(low, kn[:, :A_HD], kn[:, A_HD:])
        vn_h = jnp.where(low, vn[:, :A_HD], vn[:, A_HD:])
        l_new = jnp.sum(q * kn_h, axis=-1, keepdims=True) * scale + bias_new
        m = jnp.maximum(jnp.maximum(jnp.max(logits, axis=-1, keepdims=True), l_new), sink)
        p = jnp.exp(logits - m)
        p_new = jnp.exp(l_new - m)
        den = jnp.sum(p, axis=-1, keepdims=True) + p_new + jnp.exp(sink - m)
        pb = p.astype(BF16)
        o0 = jnp.dot(pb, vc[:, :A_HD].astype(BF16), preferred_element_type=F32)
        o1 = jnp.dot(pb, vc[:, A_HD:].astype(BF16), preferred_element_type=F32)
        o = jnp.where(low, o0, o1) + p_new * vn_h
        out_ref[b] = o / den
        nk_ref[b, 0:W - 1, :] = ck_ref[b, 1:W, :]
        nk_ref[b, W - 1:W, :] = kn
        nv_ref[b, 0:W - 1, :] = cv_ref[b, 1:W, :]
        nv_ref[b, W - 1:W, :] = vn


def _attn_step(q3, k_new, v_new, ck, cv, rel_bias, sinks):
    TB = SAMPLE_ATT_TB
    nb, W = ck.shape[0], ck.shape[1]
    bucket = _t5_bucket_np(W - np.arange(W))[None, :].astype(np.int32)
    tok = lambda i: (i, 0)
    tok3 = lambda i: (i, 0, 0)
    fix = lambda i: (0, 0)
    return pl.pallas_call(
        _attn_step_kernel,
        out_shape=(
            jax.ShapeDtypeStruct((nb, A_HEADS, A_HD), F32),
            jax.ShapeDtypeStruct(ck.shape, F32),
            jax.ShapeDtypeStruct(cv.shape, F32),
        ),
        grid=(nb // TB,),
        in_specs=[
            pl.BlockSpec((TB, A_HEADS, A_HD), tok3),
            pl.BlockSpec((TB, LANES), lambda i: (i, C_AK // LANES)),
            pl.BlockSpec((TB, LANES), lambda i: (i, C_AV // LANES)),
            pl.BlockSpec((TB, W, A_KV * A_HD), tok3),
            pl.BlockSpec((TB, W, A_KV * A_HD), tok3),
            pl.BlockSpec((1, W), fix),
            pl.BlockSpec((A_HEADS, NUM_BUCKETS), fix),
            pl.BlockSpec((A_HEADS, 1), fix),
        ],
        out_specs=(
            pl.BlockSpec((TB, A_HEADS, A_HD), tok3),
            pl.BlockSpec((TB, W, A_KV * A_HD), tok3),
            pl.BlockSpec((TB, W, A_KV * A_HD), tok3),
        ),
        scratch_shapes=[pltpu.VMEM((A_HEADS, W), F32)],
        compiler_params=_cparams(("arbitrary",)),
        name="attn_step",
    )(q3, k_new, v_new, ck, cv, jnp.asarray(bucket), rel_bias.T, sinks.reshape(A_HEADS, 1))


def _reorder_w_in(w_in):
    o = 0
    parts = {}
    for name, width in (("mq", 256), ("mk", 256), ("mv", 512), ("mo", 512), ("mi", 4), ("mf", 4),
                        ("aq", 512), ("ak", 128), ("av", 128)):
        parts[name] = w_in[:, o:o + width]
        o += width
    pad = jnp.zeros((w_in.shape[0], LANES - 2 * M_HEADS), w_in.dtype)
    cols = [parts[n] for n in ("mq", "mk", "mv", "mo", "aq", "ak", "av", "mi", "mf")] + [pad]
    return jnp.concatenate(cols, axis=1).astype(BF16)


def _lane_row(v, fill=0.0):
    return jnp.concatenate([v.astype(F32), jnp.full((LANES - v.shape[0],), fill, F32)])[None, :]


def kernel(x_prompt, x_sample, state_C, state_n, state_m, cache_k, cache_v, rel_bias, norm1, w_in, b_if,
           m_gain, sinks, w_out, norm2, w_router, b_router, w_gate, b_gate, w_up, b_up, w_down, b_down,
           final_norm):
    assert norm1.shape[0] == 1, "single-layer trunk"
    batch, seq, _ = x_prompt.shape
    nsmp = x_sample.shape[0]
    n_p = batch * seq
    W = cache_k.shape[2]

    xp = x_prompt.reshape(n_p, D_MODEL)
    xs_ = x_sample.reshape(nsmp, D_MODEL)
    w_in_r = _reorder_w_in(w_in[0])
    n1 = norm1[0][None, :]
    n2 = norm2[0][None, :]
    fn = final_norm[None, :]
    bias_row = _lane_row(b_if[0])
    gain_row = m_gain[0][None, :]
    w_m = w_out[0][:M_HEADS * M_DV].astype(BF16)
    w_a = w_out[0][M_HEADS * M_DV:].astype(BF16)
    w_r32 = jnp.concatenate([w_router[0], jnp.zeros((D_MODEL, LANES - N_EXPERTS), F32)], axis=1)
    w_r_hi = w_r32.astype(BF16)
    w_r = jnp.concatenate([w_r_hi, (w_r32 - w_r_hi.astype(F32)).astype(BF16)], axis=1)
    b_r = _lane_row(b_router[0], NEG)

    zp = _inproj(xp, n1, w_in_r, TOK_TILE)
    zs = _inproj(xs_, n1, w_in_r, nsmp)
    m_out_p, p_c, p_nrep, p_mrep = _mlstm_prompt(zp, bias_row, gain_row, batch, seq)
    m_out_p = m_out_p.reshape(n_p, M_HEADS * M_DV)
    a_out_p = _attn_prompt(zp, rel_bias, sinks[0], batch, seq)
    m_out_s, s_c, s_n, s_mrep = _mlstm_step(zs, bias_row, gain_row, state_C[0], state_n[0], state_m[0])
    q3 = zs[:, C_AQ:C_AQ + A_HEADS * A_HD].reshape(nsmp, A_HEADS, A_HD)
    a3, s_k, s_v = _attn_step(q3, zs, zs, cache_k[0].reshape(nsmp, W, A_KV * A_HD),
                              cache_v[0].reshape(nsmp, W, A_KV * A_HD), rel_bias, sinks[0])
    a_out_s = a3.reshape(nsmp, A_HEADS * A_HD).astype(BF16)

    n_tot = n_p + nsmp
    zero_counts = jnp.zeros((1, LANES), F32)
    y_p, yn_all, e_p, g_p, r_p, cnt_p = _outproj_router(xp, m_out_p, a_out_p, w_m, w_a, n2, w_r, b_r,
                                                        zero_counts, TOK_TILE, n_tot)
    y_s, yn_all, e_s, g_s, r_s, cnt = _outproj_router(xs_, m_out_s, a_out_s, w_m, w_a, n2, w_r, b_r,
                                                      cnt_p, nsmp, n_tot, yn_all)

    counts = cnt[0, :N_EXPERTS].astype(I32)
    start = jnp.cumsum(counts) - counts
    nblk = (counts + FFN_ROWS - 1) // FFN_ROWS
    blk_end = jnp.cumsum(nblk)
    ex = jnp.arange(N_EXPERTS, dtype=I32)

    def per_slot(e, r, table):
        hot = e[:, :TOP_K, None] == ex
        return jnp.sum(jnp.where(hot, table, 0), axis=-1) + r[:, :TOP_K]

    pos = jnp.concatenate([per_slot(e_p, r_p, start), per_slot(e_s, r_s, start)], axis=0).reshape(-1)
    order = jnp.concatenate([jnp.argsort(pos).astype(I32) // TOP_K, jnp.zeros((FFN_ROWS,), I32)])
    pad_start = (blk_end - nblk) * FFN_ROWS
    dest_p = per_slot(e_p, r_p, pad_start).reshape(-1).astype(I32)
    dest_s = per_slot(e_s, r_s, pad_start).reshape(-1).astype(I32)

    n_blocks = (n_tot * TOP_K + N_EXPERTS * (FFN_ROWS - 1) + FFN_ROWS - 1) // FFN_ROWS
    nused = jnp.maximum(blk_end[-1], 1).astype(I32)
    bi = jnp.minimum(jnp.arange(n_blocks, dtype=I32), nused - 1)
    block_e = jnp.minimum(jnp.sum((bi[:, None] >= blk_end[None, :]).astype(I32), axis=1), N_EXPERTS - 1)
    hot = block_e[:, None] == ex
    pick = lambda v: jnp.sum(jnp.where(hot, v, 0), axis=1)
    row0 = (bi - pick(blk_end - nblk)) * FFN_ROWS
    blk_cs = (pick(start) + row0).astype(I32)
    blk_valid = jnp.clip(pick(counts) - row0, 0, FFN_ROWS).astype(I32)
    later = jnp.where((ex[None, :] > ex[:, None]) & (counts[None, :] > 0), ex[None, :], N_EXPERTS)
    next_e = jnp.min(later, axis=1)
    blk_next = pick(jnp.where(next_e < N_EXPERTS, next_e, -1)).astype(I32)
    has_next = jnp.arange(n_blocks, dtype=I32) + 1 < nused
    blk_gcs = jnp.where(has_next, jnp.concatenate([blk_cs[1:], blk_cs[:1]]), 0)
    plan = (block_e.astype(I32), nused.reshape(1), blk_valid, blk_next, blk_cs, blk_gcs)

    as_tiles = lambda a: a.reshape(-1, ROW_TILES, LANES)
    ffn_out = _ffn(plan, order, as_tiles(yn_all), w_gate[0], b_gate[0][:, None, :], w_up[0], b_up[0][:, None, :],
                   w_down[0], b_down[0][:, None, :], n_blocks)
    out_p = _combine(dest_p, y_p, g_p, fn, as_tiles(ffn_out), COMBINE_TILE)
    out_s = _combine(dest_s, y_s, g_s, fn, as_tiles(ffn_out), min(COMBINE_TILE, nsmp))

    kv_shape = (1, batch, WINDOW, A_KV, A_HD)
    zk = zp[:, C_AK:C_AK + A_KV * A_HD].reshape(batch, seq, A_KV * A_HD)[:, seq - WINDOW:]
    zv = zp[:, C_AV:C_AV + A_KV * A_HD].reshape(batch, seq, A_KV * A_HD)[:, seq - WINDOW:]
    return (
        out_p.reshape(batch, seq, D_MODEL),
        out_s.reshape(nsmp, 1, D_MODEL),
        p_c[None],
        p_nrep[None, :, :, :, 0],
        p_mrep[None, :, :M_HEADS, 0],
        zk.reshape(kv_shape),
        zv.reshape(kv_shape),
        s_c[None],
        s_n.reshape(1, nsmp, M_HEADS, M_DK),
        s_mrep[None, :, :M_HEADS],
        s_k.reshape(1, nsmp, W, A_KV, A_HD),
        s_v.reshape(1, nsmp, W, A_KV, A_HD),
    )
```

```python
import math

import numpy as np
import jax
import jax.numpy as jnp
from jax import lax
from jax.experimental import pallas as pl
from jax.experimental.pallas import tpu as pltpu

F32 = jnp.float32
BF16 = jnp.bfloat16
I32 = jnp.int32

D_MODEL = 1024
M_HEADS = 4
M_DK = 64
M_DV = 128
A_HEADS = 8
A_KV = 2
A_GROUP = A_HEADS // A_KV
A_HD = 64
WINDOW = 128
NUM_BUCKETS = 32
MAX_DISTANCE = 128
N_EXPERTS = 32
TOP_K = 4
D_FF = 1024
SWIGLU_LIMIT = 7.0
SWIGLU_ALPHA = 1.702
EPS = 1e-5

LANES = 128
NEG = -1e30
VMEM_LIMIT = 52 * 1024 * 1024

C_MQ, C_MK, C_MV, C_MO, C_AQ, C_AK, C_AV, C_GATE = 0, 256, 512, 1024, 1536, 2048, 2176, 2304
PROJ_W = 2432

MLSTM_CHUNK = LANES
MLSTM_SEQS = 4
ATT_BLOCK = 128
TOK_TILE = 512
FFN_ROWS = 512
FFN_SUB = 256
FFN_BUFS = 3
ROW_TILES = D_MODEL // LANES
FFN_COLS = 256
CAST_ROWS = 128
COMBINE_TILE = 256
DMA_UNROLL = 8
SAMPLE_MLSTM_TB = 16
SAMPLE_ATT_TB = 8


def _t5_bucket_np(dist):
    n = np.maximum(dist, 0)
    max_exact = NUM_BUCKETS // 2
    ratio = np.log(np.maximum(n, 1).astype(np.float32) / np.float32(max_exact)) / np.float32(
        math.log(MAX_DISTANCE / max_exact))
    large = max_exact + (ratio * np.float32(NUM_BUCKETS - max_exact)).astype(np.int32)
    large = np.minimum(large, NUM_BUCKETS - 1)
    return np.where(n < max_exact, n, large).astype(np.int32)


def _cparams(sem):
    return pltpu.CompilerParams(dimension_semantics=sem, vmem_limit_bytes=VMEM_LIMIT)


def _rms(x, g):
    return x * lax.rsqrt(jnp.mean(x * x, axis=-1, keepdims=True) + EPS) * g


def _log_sigmoid(x):
    return jnp.minimum(x, 0.0) - jnp.log(1.0 + jnp.exp(-jnp.abs(x)))


def _sigmoid(x):
    return 1.0 / (1.0 + jnp.exp(-x))


def _inproj_kernel(x_ref, g_ref, w_ref, z_ref):
    xn = _rms(x_ref[...], g_ref[...]).astype(BF16)
    z_ref[...] = jnp.dot(xn, w_ref[...], preferred_element_type=F32)


def _inproj(x2, norm_row, w_bf16, tile):
    n = x2.shape[0]
    return pl.pallas_call(
        _inproj_kernel,
        out_shape=jax.ShapeDtypeStruct((n, PROJ_W), F32),
        grid=(n // tile,),
        in_specs=[
            pl.BlockSpec((tile, D_MODEL), lambda i: (i, 0)),
            pl.BlockSpec((1, D_MODEL), lambda i: (0, 0)),
            pl.BlockSpec((D_MODEL, PROJ_W), lambda i: (0, 0)),
        ],
        out_specs=pl.BlockSpec((tile, PROJ_W), lambda i: (i, 0)),
        compiler_params=_cparams(("arbitrary",)),
        name="inproj",
    )(x2, norm_row, w_bf16)


def _mlstm_prompt_kernel(q_ref, k_ref, v_ref, o_ref, gt_ref, bias_ref, gain_ref,
                         out_ref, c_ref, n_ref, m_ref, s_scr, m_scr):
    L = MLSTM_CHUNK
    c = pl.program_id(1)

    @pl.when(c == 0)
    def _():
        s_scr[...] = jnp.zeros_like(s_scr)
        m_scr[...] = jnp.zeros_like(m_scr)

    row = lax.broadcasted_iota(I32, (L, L), 0)
    col = lax.broadcasted_iota(I32, (L, L), 1)
    causal = col <= row
    tril = causal.astype(F32)
    ones = jnp.ones((L, M_DV), BF16)

    for nb in range(MLSTM_SEQS):
        gb = gt_ref[nb] + bias_ref[...]
        ls = _log_sigmoid(gb)
        bcum = jnp.dot(tril, ls, preferred_element_type=F32, precision=lax.Precision.HIGHEST)
        gb_t = gb.T
        bcum_t = bcum.T
        k_t = (k_ref[nb] * (M_DK ** -0.5)).T
        for h in range(M_HEADS):
            sh = nb * M_HEADS + h
            b_rep = jnp.broadcast_to(bcum[:, M_HEADS + h:M_HEADS + h + 1], (L, LANES))
            b_row = bcum_t[M_HEADS + h:M_HEADS + h + 1, :]
            i_row = gb_t[h:h + 1, :]
            m_prev = m_scr[nb, h:h + 1, :]
            dmat = jnp.where(causal, b_rep + (i_row - b_row), NEG)
            a_rep = b_rep + m_prev
            mt = jnp.maximum(a_rep, jnp.broadcast_to(jnp.max(dmat, axis=1, keepdims=True), (L, LANES)))
            w_intra = jnp.exp(dmat - mt)
            w_inter = jnp.exp(a_rep - mt)
            q_h = q_ref[nb, :, h * M_DK:(h + 1) * M_DK].astype(BF16)
            kt_h = k_t[h * M_DK:(h + 1) * M_DK, :]
            qk = jnp.dot(q_h, kt_h.astype(BF16), preferred_element_type=F32)
            s_w = (qk * w_intra).astype(BF16)
            v_ext = jnp.concatenate([v_ref[nb, :, h * M_DV:(h + 1) * M_DV].astype(BF16), ones], axis=1)
            state = s_scr[sh]
            inter = jnp.dot(q_h, state.astype(BF16), preferred_element_type=F32)
            intra = jnp.dot(s_w, v_ext, preferred_element_type=F32)
            num = w_inter * inter[:, :M_DV] + intra[:, :M_DV]
            qn = w_inter * inter[:, M_DV:] + intra[:, M_DV:]
            den = jnp.maximum(jnp.abs(qn), jnp.exp(-mt))
            hh = num / den
            hn = hh * lax.rsqrt(jnp.mean(hh * hh, axis=-1, keepdims=True) + EPS)
            hn = hn * gain_ref[:, h * M_DV:(h + 1) * M_DV]
            out = hn * _sigmoid(o_ref[nb, :, h * M_DV:(h + 1) * M_DV])
            out_ref[nb, :, h * M_DV:(h + 1) * M_DV] = out.astype(out_ref.dtype)
            b_last = b_rep[L - 1:L, :]
            m_new = mt[L - 1:L, :]
            g_prev = jnp.exp(b_last + m_prev - m_new)
            g_row = jnp.exp(b_last - b_row + i_row - m_new)
            kg_t = (kt_h * g_row).astype(BF16)
            s_scr[sh] = (jnp.concatenate([g_prev, g_prev], axis=1) * state
                         + jnp.dot(kg_t, v_ext, preferred_element_type=F32))
            m_scr[nb, h:h + 1, :] = m_new

    @pl.when(c == pl.num_programs(1) - 1)
    def _():
        for nb in range(MLSTM_SEQS):
            for h in range(M_HEADS):
                st = s_scr[nb * M_HEADS + h]
                c_ref[nb, h] = st[:, :M_DV]
                n_ref[nb, h] = st[:, M_DV:]
        m_ref[...] = m_scr[...]


def _mlstm_prompt(z, bias_row, gain_row, batch, seq):
    L = MLSTM_CHUNK
    S = MLSTM_SEQS
    z3 = z.reshape(batch, seq, PROJ_W)
    return pl.pallas_call(
        _mlstm_prompt_kernel,
        out_shape=(
            jax.ShapeDtypeStruct((batch, seq, M_HEADS * M_DV), BF16),
            jax.ShapeDtypeStruct((batch, M_HEADS, M_DK, M_DV), F32),
            jax.ShapeDtypeStruct((batch, M_HEADS, M_DK, M_DV), F32),
            jax.ShapeDtypeStruct((batch, 8, LANES), F32),
        ),
        grid=(batch // S, seq // L),
        in_specs=[
            pl.BlockSpec((S, L, 256), lambda b, c: (b, c, C_MQ // 256)),
            pl.BlockSpec((S, L, 256), lambda b, c: (b, c, C_MK // 256)),
            pl.BlockSpec((S, L, 512), lambda b, c: (b, c, C_MV // 512)),
            pl.BlockSpec((S, L, 512), lambda b, c: (b, c, C_MO // 512)),
            pl.BlockSpec((S, L, LANES), lambda b, c: (b, c, C_GATE // LANES)),
            pl.BlockSpec((1, LANES), lambda b, c: (0, 0)),
            pl.BlockSpec((1, M_HEADS * M_DV), lambda b, c: (0, 0)),
        ],
        out_specs=(
            pl.BlockSpec((S, L, M_HEADS * M_DV), lambda b, c: (b, c, 0)),
            pl.BlockSpec((S, M_HEADS, M_DK, M_DV), lambda b, c: (b, 0, 0, 0)),
            pl.BlockSpec((S, M_HEADS, M_DK, M_DV), lambda b, c: (b, 0, 0, 0)),
            pl.BlockSpec((S, 8, LANES), lambda b, c: (b, 0, 0)),
        ),
        scratch_shapes=[pltpu.VMEM((S * M_HEADS, M_DK, 2 * M_DV), F32), pltpu.VMEM((S, 8, LANES), F32)],
        compiler_params=_cparams(("arbitrary", "arbitrary")),
        name="mlstm_prompt",
    )(z3, z3, z3, z3, z3, bias_row, gain_row)


def _attn_prompt_kernel(relb_ref, sink_ref, q_ref, kp_ref, kc_ref, vp_ref, vc_ref, bucket_ref,
                        out_ref, bias_scr):
    B = ATT_BLOCK
    j = pl.program_id(1)

    @pl.when((pl.program_id(0) == 0) & (j == 0))
    def _():
        bucket = bucket_ref[...]
        for h in range(A_HEADS):
            acc = jnp.full((B, 2 * B), NEG, F32)
            for bk in range(NUM_BUCKETS):
                acc = jnp.where(bucket == bk, relb_ref[bk * A_HEADS + h], acc)
            bias_scr[h] = acc

    scale = A_HD ** -0.5
    s_iota = lax.broadcasted_iota(I32, (B, 2 * B), 1)
    first = jnp.where((s_iota < B) & (j == 0), NEG, 0.0)
    outs = []
    for h in range(A_HEADS):
        g = h // A_GROUP
        q_h = q_ref[:, h * A_HD:(h + 1) * A_HD].astype(BF16)
        k2 = jnp.concatenate([kp_ref[:, g * A_HD:(g + 1) * A_HD], kc_ref[:, g * A_HD:(g + 1) * A_HD]],
                             axis=0).astype(BF16)
        v2 = jnp.concatenate([vp_ref[:, g * A_HD:(g + 1) * A_HD], vc_ref[:, g * A_HD:(g + 1) * A_HD]],
                             axis=0).astype(BF16)
        logits = lax.dot_general(q_h, k2, (((1,), (1,)), ((), ())), preferred_element_type=F32)
        logits = logits * scale + bias_scr[h] + first
        sink = sink_ref[h]
        m = jnp.maximum(jnp.max(logits, axis=-1, keepdims=True), sink)
        p = jnp.exp(logits - m)
        den = jnp.sum(p, axis=-1, keepdims=True) + jnp.exp(sink - m)
        o = jnp.dot(p.astype(BF16), v2, preferred_element_type=F32) / den
        outs.append(o)
    out_ref[...] = jnp.concatenate(outs, axis=1).astype(out_ref.dtype)


def _attn_prompt(z, rel_bias, sinks, batch, seq):
    B = ATT_BLOCK
    nb = seq // B
    qi = np.arange(B)[:, None]
    si = np.arange(2 * B)[None, :]
    dist = qi + B - si
    bucket = np.where((dist >= 0) & (dist <= WINDOW), _t5_bucket_np(dist), -1).astype(np.int32)
    cur = lambda b, j, *_: b * nb + j
    prev = lambda b, j, *_: b * nb + jnp.maximum(j - 1, 0)
    grid_spec = pltpu.PrefetchScalarGridSpec(
        num_scalar_prefetch=2,
        grid=(batch, nb),
        in_specs=[
            pl.BlockSpec((B, 512), lambda b, j, *_: (cur(b, j), C_AQ // 512)),
            pl.BlockSpec((B, LANES), lambda b, j, *_: (prev(b, j), C_AK // LANES)),
            pl.BlockSpec((B, LANES), lambda b, j, *_: (cur(b, j), C_AK // LANES)),
            pl.BlockSpec((B, LANES), lambda b, j, *_: (prev(b, j), C_AV // LANES)),
            pl.BlockSpec((B, LANES), lambda b, j, *_: (cur(b, j), C_AV // LANES)),
            pl.BlockSpec((B, 2 * B), lambda b, j, *_: (0, 0)),
        ],
        out_specs=pl.BlockSpec((B, A_HEADS * A_HD), lambda b, j, *_: (cur(b, j), 0)),
        scratch_shapes=[pltpu.VMEM((A_HEADS, B, 2 * B), F32)],
    )
    return pl.pallas_call(
        _attn_prompt_kernel,
        out_shape=jax.ShapeDtypeStruct((batch * seq, A_HEADS * A_HD), BF16),
        grid_spec=grid_spec,
        compiler_params=_cparams(("arbitrary", "arbitrary")),
        name="attn_prompt",
    )(rel_bias.reshape(-1), sinks, z, z, z, z, z, jnp.asarray(bucket))


def _outproj_router_kernel(x_ref, mo_ref, ao_ref, wm_ref, wa_ref, g_ref, wr_ref, br_ref, cin_ref, *rest):
    y_ref, yn_ref, eidx_ref, gate_ref, rank_ref, cout_ref, carry = rest[-7:]
    T = x_ref.shape[0]
    i = pl.program_id(0)

    @pl.when(i == 0)
    def _():
        carry[...] = cin_ref[...]

    y = (x_ref[...] + jnp.dot(mo_ref[...], wm_ref[...], preferred_element_type=F32)
         + jnp.dot(ao_ref[...], wa_ref[...], preferred_element_type=F32))
    y_ref[...] = y
    yn = _rms(y, g_ref[...])
    for s in range(ROW_TILES):
        yn_ref[pl.ds(s, T, stride=ROW_TILES), :] = yn[:, s * LANES:(s + 1) * LANES]
    yh = yn.astype(BF16)
    yl = (yn - yh.astype(F32)).astype(BF16)
    hh = jnp.dot(yh, wr_ref[...], preferred_element_type=F32)
    lh = jnp.dot(yl, wr_ref[:, :LANES], preferred_element_type=F32)
    logits = hh[:, :LANES] + (hh[:, LANES:] + lh) + br_ref[...]
    lane = lax.broadcasted_iota(I32, (T, LANES), 1)
    lane_f = lane.astype(F32)
    vals, idxs, hots = [], [], []
    l = logits
    for _ in range(TOP_K):
        mx = jnp.max(l, axis=-1, keepdims=True)
        idx = jnp.min(jnp.where(l == mx, lane_f, float(LANES)), axis=-1, keepdims=True)
        hot = lane_f == idx
        l = jnp.where(hot, -jnp.inf, l)
        vals.append(mx)
        idxs.append(idx)
        hots.append(hot)
    es = [jnp.exp(v - vals[0]) for v in vals]
    tot = es[0] + es[1] + es[2] + es[3]
    sel = jnp.where(hots[0] | hots[1] | hots[2] | hots[3], 1.0, 0.0)
    row = lax.broadcasted_iota(I32, (T, T), 0)
    col = lax.broadcasted_iota(I32, (T, T), 1)
    strict = (col < row).astype(BF16)
    before = carry[...] + jnp.dot(strict, sel.astype(BF16), preferred_element_type=F32)
    eidx = jnp.zeros((T, LANES), I32)
    gate = jnp.zeros((T, LANES), F32)
    rank = jnp.zeros((T, LANES), I32)
    for k in range(TOP_K):
        r_k = jnp.sum(jnp.where(hots[k], before, 0.0), axis=-1, keepdims=True)
        eidx = jnp.where(lane == k, idxs[k].astype(I32), eidx)
        gate = jnp.where(lane == k, es[k] / tot, gate)
        rank = jnp.where(lane == k, r_k.astype(I32), rank)
    eidx_ref[...] = eidx
    gate_ref[...] = gate
    rank_ref[...] = rank
    carry[...] = carry[...] + jnp.sum(sel, axis=0, keepdims=True)
    cout_ref[...] = carry[...]


def _outproj_router(x2, m_out, a_out, w_m, w_a, norm_row, w_r, b_r, counts_in, tile, yn_rows, yn_prev=None):
    n = x2.shape[0]
    tok = lambda i: (i, 0)
    fix = lambda i: (0, 0)
    assert yn_prev is None or (yn_rows - n) % tile == 0
    row0 = 0 if yn_prev is None else (yn_rows - n) // tile
    extra_specs = [] if yn_prev is None else [pl.BlockSpec(memory_space=pl.ANY)]
    extra_args = [] if yn_prev is None else [yn_prev]
    return pl.pallas_call(
        _outproj_router_kernel,
        out_shape=(
            jax.ShapeDtypeStruct((n, D_MODEL), F32),
            jax.ShapeDtypeStruct((yn_rows * ROW_TILES, LANES), F32),
            jax.ShapeDtypeStruct((n, LANES), I32),
            jax.ShapeDtypeStruct((n, LANES), F32),
            jax.ShapeDtypeStruct((n, LANES), I32),
            jax.ShapeDtypeStruct((1, LANES), F32),
        ),
        grid=(n // tile,),
        in_specs=[
            pl.BlockSpec((tile, D_MODEL), tok),
            pl.BlockSpec((tile, 512), tok),
            pl.BlockSpec((tile, 512), tok),
            pl.BlockSpec((512, D_MODEL), fix),
            pl.BlockSpec((512, D_MODEL), fix),
            pl.BlockSpec((1, D_MODEL), fix),
            pl.BlockSpec((D_MODEL, 2 * LANES), fix),
            pl.BlockSpec((1, LANES), fix),
            pl.BlockSpec((1, LANES), fix),
        ] + extra_specs,
        out_specs=(
            pl.BlockSpec((tile, D_MODEL), tok),
            pl.BlockSpec((tile * ROW_TILES, LANES), lambda i: (row0 + i, 0)),
            pl.BlockSpec((tile, LANES), tok),
            pl.BlockSpec((tile, LANES), tok),
            pl.BlockSpec((tile, LANES), tok),
            pl.BlockSpec((1, LANES), fix),
        ),
        scratch_shapes=[pltpu.VMEM((1, LANES), F32)],
        input_output_aliases={} if yn_prev is None else {9: 1},
        compiler_params=_cparams(("arbitrary",)),
        name="outproj_router",
    )(x2, m_out, a_out, w_m, w_a, norm_row, w_r, b_r, counts_in, *extra_args)


def _ffn_kernel(be_ref, nused_ref, valid_ref, nxt_ref, gp_ref, gs_ref, gn_ref, order_ref,
                yn_hbm, wg_hbm, bg_ref, wu_hbm, bu_ref, wd_hbm, bd_ref, out_ref,
                wbuf, wg_bf, wu_bf, wd_bf, h_scr, xbuf0, xbuf1, xbuf2, wsem, gsem):
    i = pl.program_id(0)
    slot = lax.rem(i, FFN_BUFS)
    R = FFN_ROWS
    w_hbm = (wg_hbm, wu_hbm, wd_hbm)
    w_bf = (wg_bf, wu_bf, wd_bf)
    xbuf = (xbuf0, xbuf1, xbuf2)

    def weight_copies(e):
        return [pltpu.make_async_copy(w_hbm[j].at[e], wbuf.at[j], wsem.at[j]) for j in range(3)]

    def gather_row(b, r, dst_slot):
        tok = order_ref[jnp.where(r < gn_ref[b], gp_ref[b], gs_ref[b]) + r]
        return pltpu.make_async_copy(yn_hbm.at[tok], xbuf[dst_slot].at[pl.ds(r * ROW_TILES, ROW_TILES)],
                                     gsem.at[dst_slot])

    def gather_rolled(b, dst_slot):
        def body(rb, carry):
            for u in range(DMA_UNROLL):
                gather_row(b, rb * DMA_UNROLL + u, dst_slot).start(priority=u % 2)
            return carry
        lax.fori_loop(0, R // DMA_UNROLL, body, 0)

    def wait_rows(s):
        pltpu.make_async_copy(xbuf[(s + 1) % FFN_BUFS], xbuf[s], gsem.at[s]).wait()

    def by_slot(fn):
        for s in range(FFN_BUFS):
            @pl.when(slot == s)
            def _():
                fn(s)

    @pl.when(i < nused_ref[0])
    def _():
        @pl.when((i == 0) | (be_ref[i] != be_ref[jnp.maximum(i - 1, 0)]))
        def _():
            @pl.when(i == 0)
            def _():
                for cp in weight_copies(be_ref[0]):
                    cp.start()

            for cp in weight_copies(be_ref[i]):
                cp.wait()

            for j in range(3):
                for r in range(0, D_MODEL, CAST_ROWS):
                    w_bf[j][r:r + CAST_ROWS, :] = wbuf[j, r:r + CAST_ROWS, :].astype(BF16)

            @pl.when(nxt_ref[i] >= 0)
            def _():
                for cp in weight_copies(nxt_ref[i]):
                    cp.start()

        @pl.when(i == 0)
        def _():
            for b in range(FFN_BUFS - 1):
                gather_rolled(b, b)

        by_slot(wait_rows)

        def ffn_pass(rows, slot):
            x = jnp.concatenate([xbuf[slot][pl.ds(s, rows, stride=ROW_TILES), :].astype(BF16)
                                 for s in range(ROW_TILES)], axis=1)
            ahead = i + (FFN_BUFS - 1)
            g_p, g_s, g_n = gp_ref[ahead], gs_ref[ahead], gn_ref[ahead]
            dst = (slot + FFN_BUFS - 1) % FFN_BUFS
            for r in range(R):
                tok = order_ref[jnp.where(r < g_n, g_p, g_s) + r]
                pltpu.make_async_copy(yn_hbm.at[tok], xbuf[dst].at[pl.ds(r * ROW_TILES, ROW_TILES)],
                                      gsem.at[dst]).start(priority=r % 2)
            for c in range(D_FF // FFN_COLS):
                cs = slice(c * FFN_COLS, (c + 1) * FFN_COLS)
                g = jnp.dot(x, wg_bf[:, cs], preferred_element_type=F32) + bg_ref[:, cs]
                u = jnp.dot(x, wu_bf[:, cs], preferred_element_type=F32) + bu_ref[:, cs]
                g = jnp.minimum(g, SWIGLU_LIMIT)
                u = jnp.clip(u, -SWIGLU_LIMIT, SWIGLU_LIMIT)
                h_scr[0:rows, cs] = ((u + 1.0) * (g * _sigmoid(SWIGLU_ALPHA * g))).astype(BF16)
            out = jnp.dot(h_scr[0:rows, :], wd_bf[...], preferred_element_type=F32) + bd_ref[...]
            for s in range(ROW_TILES):
                out_ref[pl.ds(s, rows, stride=ROW_TILES), :] = out[:, s * LANES:(s + 1) * LANES]

        for s in range(FFN_BUFS):
            for rows in range(FFN_SUB, R + 1, FFN_SUB):
                @pl.when((slot == s) & (valid_ref[i] > rows - FFN_SUB) & (valid_ref[i] <= rows))
                def _():
                    ffn_pass(rows, s)

        @pl.when(i == nused_ref[0] - 1)
        def _():
            def drain(s):
                for d in range(1, FFN_BUFS):
                    wait_rows((s + d) % FFN_BUFS)
            by_slot(drain)


def _ffn(plan, order, yn_tiles, wg, bg, wu, bu, wd, bd, n_blocks):
    blk = lambda i, be, nu, *_: (jnp.minimum(i, nu[0] - 1), 0)
    bsel = lambda i, be, *_: (be[i], 0, 0)
    hbm = pl.BlockSpec(memory_space=pl.ANY)
    grid_spec = pltpu.PrefetchScalarGridSpec(
        num_scalar_prefetch=8,
        grid=(n_blocks,),
        in_specs=[
            hbm,
            hbm,
            pl.BlockSpec((None, 1, D_FF), bsel),
            hbm,
            pl.BlockSpec((None, 1, D_FF), bsel),
            hbm,
            pl.BlockSpec((None, 1, D_MODEL), bsel),
        ],
        out_specs=pl.BlockSpec((FFN_ROWS * ROW_TILES, LANES), blk),
        scratch_shapes=[pltpu.VMEM((3, D_MODEL, D_FF), F32),
                        pltpu.VMEM((D_MODEL, D_FF), BF16), pltpu.VMEM((D_MODEL, D_FF), BF16),
                        pltpu.VMEM((D_FF, D_MODEL), BF16), pltpu.VMEM((FFN_ROWS, D_FF), BF16),
                        pltpu.VMEM((FFN_ROWS * ROW_TILES, LANES), F32),
                        pltpu.VMEM((FFN_ROWS * ROW_TILES, LANES), F32),
                        pltpu.VMEM((FFN_ROWS * ROW_TILES, LANES), F32),
                        pltpu.SemaphoreType.DMA((3,)), pltpu.SemaphoreType.DMA((FFN_BUFS,))],
    )
    return pl.pallas_call(
        _ffn_kernel,
        out_shape=jax.ShapeDtypeStruct((n_blocks * FFN_ROWS * ROW_TILES, LANES), F32),
        grid_spec=grid_spec,
        compiler_params=_cparams(("arbitrary",)),
        name="moe_ffn",
    )(*plan, order, yn_tiles, wg, bg, wu, bu, wd, bd)


def _combine_kernel(dest_ref, y_ref, gate_ref, fn_ref, ffn_ref, out_ref, buf, sem):
    T = y_ref.shape[0]
    i = pl.program_id(0)
    n = pl.num_programs(0)
    slot = i % 2

    def issue(tile, s):
        base = tile * (T * TOP_K)

        def body(tb, carry):
            for u in range(DMA_UNROLL):
                t = tb * DMA_UNROLL + u
                for k in range(TOP_K):
                    d = dest_ref[base + t * TOP_K + k]
                    pltpu.make_async_copy(ffn_ref.at[d], buf.at[s, k, pl.ds(t * ROW_TILES, ROW_TILES)],
                                          sem.at[s]).start(priority=k % 2)
            return carry

        lax.fori_loop(0, T // DMA_UNROLL, body, 0)

    @pl.when(i == 0)
    def _():
        issue(0, 0)

    @pl.when(i + 1 < n)
    def _():
        issue(i + 1, 1 - slot)

    for k in range(TOP_K):
        pltpu.make_async_copy(buf.at[1 - slot, k], buf.at[slot, k], sem.at[slot]).wait()

    acc = y_ref[...]
    gate = gate_ref[...]
    for k in range(TOP_K):
        rows = jnp.concatenate([buf[slot, k, pl.ds(s, T, stride=ROW_TILES), :] for s in range(ROW_TILES)],
                               axis=1)
        acc = acc + gate[:, k:k + 1] * rows
    out_ref[...] = _rms(acc, fn_ref[...])


def _combine(dest_flat, y, gate, fnorm_row, ffn_out, tile):
    n = y.shape[0]
    grid_spec = pltpu.PrefetchScalarGridSpec(
        num_scalar_prefetch=1,
        grid=(n // tile,),
        in_specs=[
            pl.BlockSpec((tile, D_MODEL), lambda i, *_: (i, 0)),
            pl.BlockSpec((tile, LANES), lambda i, *_: (i, 0)),
            pl.BlockSpec((1, D_MODEL), lambda i, *_: (0, 0)),
            pl.BlockSpec(memory_space=pl.ANY),
        ],
        out_specs=pl.BlockSpec((tile, D_MODEL), lambda i, *_: (i, 0)),
        scratch_shapes=[pltpu.VMEM((2, TOP_K, tile * ROW_TILES, LANES), F32), pltpu.SemaphoreType.DMA((2,))],
    )
    return pl.pallas_call(
        _combine_kernel,
        out_shape=jax.ShapeDtypeStruct((n, D_MODEL), F32),
        grid_spec=grid_spec,
        compiler_params=_cparams(("arbitrary",)),
        name="moe_combine",
    )(dest_flat, y, gate, fnorm_row, ffn_out)


def _mlstm_step_kernel(q_ref, k_ref, v_ref, o_ref, gt_ref, bias_ref, gain_ref, c0_ref, n0_ref, m0_ref,
                       out_ref, c_ref, n_ref, m_ref):
    TB = SAMPLE_MLSTM_TB
    gb = gt_ref[...] + bias_ref[...]
    ls = _log_sigmoid(gb)
    lane = lax.broadcasted_iota(I32, (TB, LANES), 1)
    eye = (lax.broadcasted_iota(I32, (M_DK, M_DK), 0) == lax.broadcasted_iota(I32, (M_DK, M_DK), 1)).astype(F32)
    nt = (((1,), (1,)), ((), ()))
    m_all = jnp.zeros((TB, LANES), F32)
    for h in range(M_HEADS):
        i_pre = gb[:, h:h + 1]
        a = ls[:, M_HEADS + h:M_HEADS + h + 1] + m0_ref[:, h:h + 1]
        mt = jnp.maximum(a, i_pre)
        w_intra = jnp.exp(i_pre - mt)
        w_inter = jnp.exp(a - mt)
        q_h = q_ref[:, h * M_DK:(h + 1) * M_DK]
        k_h = k_ref[:, h * M_DK:(h + 1) * M_DK] * (M_DK ** -0.5)
        v_h = v_ref[:, h * M_DV:(h + 1) * M_DV]
        n0_h = n0_ref[:, h, :]
        s = jnp.sum(q_h * k_h, axis=-1, keepdims=True) * w_intra
        qn = w_inter * jnp.sum(q_h * n0_h, axis=-1, keepdims=True) + s
        den = jnp.maximum(jnp.abs(qn), jnp.exp(-mt))
        q_t = lax.dot_general(eye, q_h, nt, preferred_element_type=F32, precision=lax.Precision.HIGHEST)
        k_t = lax.dot_general(eye, k_h, nt, preferred_element_type=F32, precision=lax.Precision.HIGHEST)
        rows = []
        for b in range(TB):
            c0 = c0_ref[b, h]
            qc = jnp.sum(c0 * q_t[:, b:b + 1], axis=0, keepdims=True)
            v_b = v_h[b:b + 1, :]
            rows.append(w_inter[b:b + 1, :] * qc + s[b:b + 1, :] * v_b)
            c_ref[b, h] = w_inter[b:b + 1, :] * c0 + (w_intra[b:b + 1, :] * k_t[:, b:b + 1]) * v_b
        num = jnp.concatenate(rows, axis=0)
        hh = num / den
        hn = hh * lax.rsqrt(jnp.mean(hh * hh, axis=-1, keepdims=True) + EPS)
        hn = hn * gain_ref[:, h * M_DV:(h + 1) * M_DV]
        out_ref[:, h * M_DV:(h + 1) * M_DV] = (hn * _sigmoid(o_ref[:, h * M_DV:(h + 1) * M_DV])).astype(out_ref.dtype)
        n_ref[:, h * M_DK:(h + 1) * M_DK] = w_inter * n0_h + w_intra * k_h
        m_all = jnp.where(lane == h, mt, m_all)
    m_ref[...] = m_all


def _mlstm_step(zs, bias_row, gain_row, c0, n0, m0):
    TB = SAMPLE_MLSTM_TB
    nb = zs.shape[0]
    tok = lambda i: (i, 0)
    return pl.pallas_call(
        _mlstm_step_kernel,
        out_shape=(
            jax.ShapeDtypeStruct((nb, M_HEADS * M_DV), BF16),
            jax.ShapeDtypeStruct((nb, M_HEADS, M_DK, M_DV), F32),
            jax.ShapeDtypeStruct((nb, M_HEADS * M_DK), F32),
            jax.ShapeDtypeStruct((nb, LANES), F32),
        ),
        grid=(nb // TB,),
        in_specs=[
            pl.BlockSpec((TB, 256), lambda i: (i, C_MQ // 256)),
            pl.BlockSpec((TB, 256), lambda i: (i, C_MK // 256)),
            pl.BlockSpec((TB, 512), lambda i: (i, C_MV // 512)),
            pl.BlockSpec((TB, 512), lambda i: (i, C_MO // 512)),
            pl.BlockSpec((TB, LANES), lambda i: (i, C_GATE // LANES)),
            pl.BlockSpec((1, LANES), lambda i: (0, 0)),
            pl.BlockSpec((1, M_HEADS * M_DV), lambda i: (0, 0)),
            pl.BlockSpec((TB, M_HEADS, M_DK, M_DV), lambda i: (i, 0, 0, 0)),
            pl.BlockSpec((TB, M_HEADS, M_DK), lambda i: (i, 0, 0)),
            pl.BlockSpec((TB, M_HEADS), tok),
        ],
        out_specs=(
            pl.BlockSpec((TB, M_HEADS * M_DV), tok),
            pl.BlockSpec((TB, M_HEADS, M_DK, M_DV), lambda i: (i, 0, 0, 0)),
            pl.BlockSpec((TB, M_HEADS * M_DK), tok),
            pl.BlockSpec((TB, LANES), tok),
        ),
        compiler_params=_cparams(("arbitrary",)),
        name="mlstm_step",
    )(zs, zs, zs, zs, zs, bias_row, gain_row, c0, n0, m0)


def _attn_step_kernel(q_ref, kn_ref, vn_ref, ck_ref, cv_ref, bucket_ref, relt_ref, sink_ref,
                      out_ref, nk_ref, nv_ref, bias_scr):
    TB = SAMPLE_ATT_TB
    W = ck_ref.shape[1]

    @pl.when(pl.program_id(0) == 0)
    def _():
        bucket = jnp.broadcast_to(bucket_ref[...], (A_HEADS, W))
        acc = jnp.zeros((A_HEADS, W), F32)
        for bk in range(NUM_BUCKETS):
            acc = jnp.where(bucket == bk, relt_ref[:, bk:bk + 1], acc)
        bias_scr[...] = acc

    scale = A_HD ** -0.5
    nt = (((1,), (1,)), ((), ()))
    bias = bias_scr[...]
    bias_new = relt_ref[:, 0:1]
    sink = sink_ref[...]
    low = lax.broadcasted_iota(I32, (A_HEADS, 1), 0) < A_GROUP
    for b in range(TB):
        q = q_ref[b]
        qb = q.astype(BF16)
        kc = ck_ref[b]
        vc = cv_ref[b]
        kn = kn_ref[b:b + 1, :]
        vn = vn_ref[b:b + 1, :]
        l0 = lax.dot_general(qb, kc[:, :A_HD].astype(BF16), nt, preferred_element_type=F32)
        l1 = lax.dot_general(qb, kc[:, A_HD:].astype(BF16), nt, preferred_element_type=F32)
        logits = jnp.where(low, l0, l1) * scale + bias
        kn_h = jnp.where(low, kn[:, :A_HD], kn[:, A_HD:])
        vn_h = jnp.where(low, vn[:, :A_HD], vn[:, A_HD:])
        l_new = jnp.sum(q * kn_h, axis=-1, keepdims=True) * scale + bias_new
        m = jnp.maximum(jnp.maximum(jnp.max(logits, axis=-1, keepdims=True), l_new), sink)
        p = jnp.exp(logits - m)
        p_new = jnp.exp(l_new - m)
        den = jnp.sum(p, axis=-1, keepdims=True) + p_new + jnp.exp(sink - m)
        pb = p.astype(BF16)
        o0 = jnp.dot(pb, vc[:, :A_HD].astype(BF16), preferred_element_type=F32)
        o1 = jnp.dot(pb, vc[:, A_HD:].astype(BF16), preferred_element_type=F32)
        o = jnp.where(low, o0, o1) + p_new * vn_h
        out_ref[b] = o / den
        nk_ref[b, 0:W - 1, :] = ck_ref[b, 1:W, :]
        nk_ref[b, W - 1:W, :] = kn
        nv_ref[b, 0:W - 1, :] = cv_ref[b, 1:W, :]
        nv_ref[b, W - 1:W, :] = vn


def _attn_step(q3, k_new, v_new, ck, cv, rel_bias, sinks):
    TB = SAMPLE_ATT_TB
    nb, W = ck.shape[0], ck.shape[1]
    bucket = _t5_bucket_np(W - np.arange(W))[None, :].astype(np.int32)
    tok = lambda i: (i, 0)
    tok3 = lambda i: (i, 0, 0)
    fix = lambda i: (0, 0)
    return pl.pallas_call(
        _attn_step_kernel,
        out_shape=(
            jax.ShapeDtypeStruct((nb, A_HEADS, A_HD), F32),
            jax.ShapeDtypeStruct(ck.shape, F32),
            jax.ShapeDtypeStruct(cv.shape, F32),
        ),
        grid=(nb // TB,),
        in_specs=[
            pl.BlockSpec((TB, A_HEADS, A_HD), tok3),
            pl.BlockSpec((TB, LANES), lambda i: (i, C_AK // LANES)),
            pl.BlockSpec((TB, LANES), lambda i: (i, C_AV // LANES)),
            pl.BlockSpec((TB, W, A_KV * A_HD), tok3),
            pl.BlockSpec((TB, W, A_KV * A_HD), tok3),
            pl.BlockSpec((1, W), fix),
            pl.BlockSpec((A_HEADS, NUM_BUCKETS), fix),
            pl.BlockSpec((A_HEADS, 1), fix),
        ],
        out_specs=(
            pl.BlockSpec((TB, A_HEADS, A_HD), tok3),
            pl.BlockSpec((TB, W, A_KV * A_HD), tok3),
            pl.BlockSpec((TB, W, A_KV * A_HD), tok3),
        ),
        scratch_shapes=[pltpu.VMEM((A_HEADS, W), F32)],
        compiler_params=_cparams(("arbitrary",)),
        name="attn_step",
    )(q3, k_new, v_new, ck, cv, jnp.asarray(bucket), rel_bias.T, sinks.reshape(A_HEADS, 1))


def _reorder_w_in(w_in):
    o = 0
    parts = {}
    for name, width in (("mq", 256), ("mk", 256), ("mv", 512), ("mo", 512), ("mi", 4), ("mf", 4),
                        ("aq", 512), ("ak", 128), ("av", 128)):
        parts[name] = w_in[:, o:o + width]
        o += width
    pad = jnp.zeros((w_in.shape[0], LANES - 2 * M_HEADS), w_in.dtype)
    cols = [parts[n] for n in ("mq", "mk", "mv", "mo", "aq", "ak", "av", "mi", "mf")] + [pad]
    return jnp.concatenate(cols, axis=1).astype(BF16)


def _lane_row(v, fill=0.0):
    return jnp.concatenate([v.astype(F32), jnp.full((LANES - v.shape[0],), fill, F32)])[None, :]


def kernel(x_prompt, x_sample, state_C, state_n, state_m, cache_k, cache_v, rel_bias, norm1, w_in, b_if,
           m_gain, sinks, w_out, norm2, w_router, b_router, w_gate, b_gate, w_up, b_up, w_down, b_down,
           final_norm):
    assert norm1.shape[0] == 1, "single-layer trunk"
    batch, seq, _ = x_prompt.shape
    nsmp = x_sample.shape[0]
    n_p = batch * seq
    W = cache_k.shape[2]

    xp = x_prompt.reshape(n_p, D_MODEL)
    xs_ = x_sample.reshape(nsmp, D_MODEL)
    w_in_r = _reorder_w_in(w_in[0])
    n1 = norm1[0][None, :]
    n2 = norm2[0][None, :]
    fn = final_norm[None, :]
    bias_row = _lane_row(b_if[0])
    gain_row = m_gain[0][None, :]
    w_m = w_out[0][:M_HEADS * M_DV].astype(BF16)
    w_a = w_out[0][M_HEADS * M_DV:].astype(BF16)
    w_r32 = jnp.concatenate([w_router[0], jnp.zeros((D_MODEL, LANES - N_EXPERTS), F32)], axis=1)
    w_r_hi = w_r32.astype(BF16)
    w_r = jnp.concatenate([w_r_hi, (w_r32 - w_r_hi.astype(F32)).astype(BF16)], axis=1)
    b_r = _lane_row(b_router[0], NEG)

    zp = _inproj(xp, n1, w_in_r, TOK_TILE)
    zs = _inproj(xs_, n1, w_in_r, nsmp)
    m_out_p, p_c, p_nrep, p_mrep = _mlstm_prompt(zp, bias_row, gain_row, batch, seq)
    m_out_p = m_out_p.reshape(n_p, M_HEADS * M_DV)
    a_out_p = _attn_prompt(zp, rel_bias, sinks[0], batch, seq)
    m_out_s, s_c, s_n, s_mrep = _mlstm_step(zs, bias_row, gain_row, state_C[0], state_n[0], state_m[0])
    q3 = zs[:, C_AQ:C_AQ + A_HEADS * A_HD].reshape(nsmp, A_HEADS, A_HD)
    a3, s_k, s_v = _attn_step(q3, zs, zs, cache_k[0].reshape(nsmp, W, A_KV * A_HD),
                              cache_v[0].reshape(nsmp, W, A_KV * A_HD), rel_bias, sinks[0])
    a_out_s = a3.reshape(nsmp, A_HEADS * A_HD).astype(BF16)

    n_tot = n_p + nsmp
    zero_counts = jnp.zeros((1, LANES), F32)
    y_p, yn_all, e_p, g_p, r_p, cnt_p = _outproj_router(xp, m_out_p, a_out_p, w_m, w_a, n2, w_r, b_r,
                                                        zero_counts, TOK_TILE, n_tot)
    y_s, yn_all, e_s, g_s, r_s, cnt = _outproj_router(xs_, m_out_s, a_out_s, w_m, w_a, n2, w_r, b_r,
                                                      cnt_p, nsmp, n_tot, yn_all)

    counts = cnt[0, :N_EXPERTS].astype(I32)
    counts_p = cnt_p[0, :N_EXPERTS].astype(I32)
    counts_s = counts - counts_p
    start_p = jnp.cumsum(counts_p) - counts_p
    start_s = jnp.cumsum(counts_s) - counts_s
    nblk = (counts + FFN_ROWS - 1) // FFN_ROWS
    blk_end = jnp.cumsum(nblk)
    ex = jnp.arange(N_EXPERTS, dtype=I32)

    def per_slot(e, r, table):
        hot = e[:, :TOP_K, None] == ex
        return jnp.sum(jnp.where(hot, table, 0), axis=-1) + r[:, :TOP_K]

    order = jnp.concatenate([
        jnp.argsort(per_slot(e_p, r_p, start_p).reshape(-1)).astype(I32) // TOP_K,
        jnp.argsort(per_slot(e_s, r_s, start_s - counts_p).reshape(-1)).astype(I32) // TOP_K + n_p,
        jnp.zeros((2 * FFN_ROWS,), I32)])
    pad_start = (blk_end - nblk) * FFN_ROWS
    dest_p = per_slot(e_p, r_p, pad_start).reshape(-1).astype(I32)
    dest_s = per_slot(e_s, r_s, pad_start).reshape(-1).astype(I32)

    n_blocks = (n_tot * TOP_K + N_EXPERTS * (FFN_ROWS - 1) + FFN_ROWS - 1) // FFN_ROWS
    nused = jnp.maximum(blk_end[-1], 1).astype(I32)
    bi = jnp.minimum(jnp.arange(n_blocks + FFN_BUFS - 1, dtype=I32), nused - 1)
    block_e = jnp.minimum(jnp.sum((bi[:, None] >= blk_end[None, :]).astype(I32), axis=1), N_EXPERTS - 1)
    hot = block_e[:, None] == ex
    pick = lambda v: jnp.sum(jnp.where(hot, v, 0), axis=1)
    row0 = (bi - pick(blk_end - nblk)) * FFN_ROWS
    blk_valid = jnp.clip(pick(counts) - row0, 0, FFN_ROWS).astype(I32)
    real = jnp.arange(n_blocks + FFN_BUFS - 1, dtype=I32) < nused
    blk_gn = jnp.where(real, jnp.clip(pick(counts_p) - row0, 0, FFN_ROWS), 0).astype(I32)
    blk_gp = jnp.where(real, pick(start_p) + row0, 0).astype(I32)
    blk_gs = jnp.where(real, n_p * TOP_K + pick(start_s - counts_p) + row0, n_tot * TOP_K).astype(I32)
    later = jnp.where((ex[None, :] > ex[:, None]) & (counts[None, :] > 0), ex[None, :], N_EXPERTS)
    next_e = jnp.min(later, axis=1)
    blk_next = pick(jnp.where(next_e < N_EXPERTS, next_e, -1)).astype(I32)
    plan = (block_e[:n_blocks].astype(I32), nused.reshape(1), blk_valid[:n_blocks], blk_next[:n_blocks],
            blk_gp, blk_gs, blk_gn)

    as_tiles = lambda a: a.reshape(-1, ROW_TILES, LANES)
    ffn_out = _ffn(plan, order, as_tiles(yn_all), w_gate[0], b_gate[0][:, None, :], w_up[0], b_up[0][:, None, :],
                   w_down[0], b_down[0][:, None, :], n_blocks)
    out_p = _combine(dest_p, y_p, g_p, fn, as_tiles(ffn_out), COMBINE_TILE)
    out_s = _combine(dest_s, y_s, g_s, fn, as_tiles(ffn_out), min(COMBINE_TILE, nsmp))

    kv_shape = (1, batch, WINDOW, A_KV, A_HD)
    zk = zp[:, C_AK:C_AK + A_KV * A_HD].reshape(batch, seq, A_KV * A_HD)[:, seq - WINDOW:]
    zv = zp[:, C_AV:C_AV + A_KV * A_HD].reshape(batch, seq, A_KV * A_HD)[:, seq - WINDOW:]
    return (
        out_p.reshape(batch, seq, D_MODEL),
        out_s.reshape(nsmp, 1, D_MODEL),
        p_c[None],
        p_nrep[None, :, :, :, 0],
        p_mrep[None, :, :M_HEADS, 0],
        zk.reshape(kv_shape),
        zv.reshape(kv_shape),
        s_c[None],
        s_n.reshape(1, nsmp, M_HEADS, M_DK),
        s_mrep[None, :, :M_HEADS],
        s_k.reshape(1, nsmp, W, A_KV, A_HD),
        s_v.reshape(1, nsmp, W, A_KV, A_HD),
    )
```

```python
import math

import numpy as np
import jax
import jax.numpy as jnp
from jax import lax
from jax.experimental import pallas as pl
from jax.experimental.pallas import tpu as pltpu

F32 = jnp.float32
BF16 = jnp.bfloat16
I32 = jnp.int32

D_MODEL = 1024
M_HEADS = 4
M_DK = 64
M_DV = 128
A_HEADS = 8
A_KV = 2
A_GROUP = A_HEADS // A_KV
A_HD = 64
WINDOW = 128
NUM_BUCKETS = 32
MAX_DISTANCE = 128
N_EXPERTS = 32
TOP_K = 4
D_FF = 1024
SWIGLU_LIMIT = 7.0
SWIGLU_ALPHA = 1.702
EPS = 1e-5

LANES = 128
NEG = -1e30
VMEM_LIMIT = 52 * 1024 * 1024

C_MQ, C_MK, C_MV, C_MO, C_AQ, C_AK, C_AV, C_GATE = 0, 256, 512, 1024, 1536, 2048, 2176, 2304
PROJ_W = 2432

MLSTM_CHUNK = LANES
MLSTM_SEQS = 4
ATT_BLOCK = 128
ATT_QBLOCKS = 2
TOK_TILE = 512
FFN_ROWS = 512
FFN_SUB = 256
FFN_BUFS = 3
ROW_TILES = D_MODEL // LANES
FFN_COLS = 256
CAST_ROWS = 128
COMBINE_TILE = 256
DMA_UNROLL = 8
SAMPLE_MLSTM_TB = 16
SAMPLE_ATT_TB = 8


def _t5_bucket_np(dist):
    n = np.maximum(dist, 0)
    max_exact = NUM_BUCKETS // 2
    ratio = np.log(np.maximum(n, 1).astype(np.float32) / np.float32(max_exact)) / np.float32(
        math.log(MAX_DISTANCE / max_exact))
    large = max_exact + (ratio * np.float32(NUM_BUCKETS - max_exact)).astype(np.int32)
    large = np.minimum(large, NUM_BUCKETS - 1)
    return np.where(n < max_exact, n, large).astype(np.int32)


def _cparams(sem):
    return pltpu.CompilerParams(dimension_semantics=sem, vmem_limit_bytes=VMEM_LIMIT)


def _rms(x, g):
    return x * lax.rsqrt(jnp.mean(x * x, axis=-1, keepdims=True) + EPS) * g


def _log_sigmoid(x):
    return jnp.minimum(x, 0.0) - jnp.log(1.0 + jnp.exp(-jnp.abs(x)))


def _sigmoid(x):
    return 1.0 / (1.0 + jnp.exp(-x))


def _inproj_kernel(x_ref, g_ref, w_ref, z_ref):
    xn = _rms(x_ref[...], g_ref[...]).astype(BF16)
    z_ref[...] = jnp.dot(xn, w_ref[...], preferred_element_type=F32)


def _inproj(x2, norm_row, w_bf16, tile):
    n = x2.shape[0]
    return pl.pallas_call(
        _inproj_kernel,
        out_shape=jax.ShapeDtypeStruct((n, PROJ_W), F32),
        grid=(n // tile,),
        in_specs=[
            pl.BlockSpec((tile, D_MODEL), lambda i: (i, 0)),
            pl.BlockSpec((1, D_MODEL), lambda i: (0, 0)),
            pl.BlockSpec((D_MODEL, PROJ_W), lambda i: (0, 0)),
        ],
        out_specs=pl.BlockSpec((tile, PROJ_W), lambda i: (i, 0)),
        compiler_params=_cparams(("arbitrary",)),
        name="inproj",
    )(x2, norm_row, w_bf16)


def _mlstm_prompt_kernel(q_ref, k_ref, v_ref, o_ref, gt_ref, bias_ref, gain_ref,
                         out_ref, c_ref, n_ref, m_ref, s_scr, m_scr):
    L = MLSTM_CHUNK
    c = pl.program_id(1)

    @pl.when(c == 0)
    def _():
        s_scr[...] = jnp.zeros_like(s_scr)
        m_scr[...] = jnp.zeros_like(m_scr)

    row = lax.broadcasted_iota(I32, (L, L), 0)
    col = lax.broadcasted_iota(I32, (L, L), 1)
    causal = col <= row
    tril = causal.astype(F32)
    ones = jnp.ones((L, M_DV), BF16)

    for nb in range(MLSTM_SEQS):
        gb = gt_ref[nb] + bias_ref[...]
        ls = _log_sigmoid(gb)
        bcum = jnp.dot(tril, ls, preferred_element_type=F32, precision=lax.Precision.HIGHEST)
        gb_t = gb.T
        bcum_t = bcum.T
        k_t = (k_ref[nb] * (M_DK ** -0.5)).T
        for h in range(M_HEADS):
            sh = nb * M_HEADS + h
            b_rep = jnp.broadcast_to(bcum[:, M_HEADS + h:M_HEADS + h + 1], (L, LANES))
            b_row = bcum_t[M_HEADS + h:M_HEADS + h + 1, :]
            i_row = gb_t[h:h + 1, :]
            m_prev = m_scr[nb, h:h + 1, :]
            dmat = jnp.where(causal, b_rep + (i_row - b_row), NEG)
            a_rep = b_rep + m_prev
            mt = jnp.maximum(a_rep, jnp.broadcast_to(jnp.max(dmat, axis=1, keepdims=True), (L, LANES)))
            w_intra = jnp.exp(dmat - mt)
            w_inter = jnp.exp(a_rep - mt)
            q_h = q_ref[nb, :, h * M_DK:(h + 1) * M_DK].astype(BF16)
            kt_h = k_t[h * M_DK:(h + 1) * M_DK, :]
            qk = jnp.dot(q_h, kt_h.astype(BF16), preferred_element_type=F32)
            s_w = (qk * w_intra).astype(BF16)
            v_ext = jnp.concatenate([v_ref[nb, :, h * M_DV:(h + 1) * M_DV].astype(BF16), ones], axis=1)
            state = s_scr[sh]
            inter = jnp.dot(q_h, state.astype(BF16), preferred_element_type=F32)
            intra = jnp.dot(s_w, v_ext, preferred_element_type=F32)
            num = w_inter * inter[:, :M_DV] + intra[:, :M_DV]
            qn = w_inter * inter[:, M_DV:] + intra[:, M_DV:]
            den = jnp.maximum(jnp.abs(qn), jnp.exp(-mt))
            hh = num / den
            hn = hh * lax.rsqrt(jnp.mean(hh * hh, axis=-1, keepdims=True) + EPS)
            hn = hn * gain_ref[:, h * M_DV:(h + 1) * M_DV]
            out = hn * _sigmoid(o_ref[nb, :, h * M_DV:(h + 1) * M_DV])
            out_ref[nb, :, h * M_DV:(h + 1) * M_DV] = out.astype(out_ref.dtype)
            b_last = b_rep[L - 1:L, :]
            m_new = mt[L - 1:L, :]
            g_prev = jnp.exp(b_last + m_prev - m_new)
            g_row = jnp.exp(b_last - b_row + i_row - m_new)
            kg_t = (kt_h * g_row).astype(BF16)
            s_scr[sh] = (jnp.concatenate([g_prev, g_prev], axis=1) * state
                         + jnp.dot(kg_t, v_ext, preferred_element_type=F32))
            m_scr[nb, h:h + 1, :] = m_new

    @pl.when(c == pl.num_programs(1) - 1)
    def _():
        for nb in range(MLSTM_SEQS):
            for h in range(M_HEADS):
                st = s_scr[nb * M_HEADS + h]
                c_ref[nb, h] = st[:, :M_DV]
                n_ref[nb, h] = st[:, M_DV:]
        m_ref[...] = m_scr[...]


def _mlstm_prompt(z, bias_row, gain_row, batch, seq):
    L = MLSTM_CHUNK
    S = MLSTM_SEQS
    z3 = z.reshape(batch, seq, PROJ_W)
    return pl.pallas_call(
        _mlstm_prompt_kernel,
        out_shape=(
            jax.ShapeDtypeStruct((batch, seq, M_HEADS * M_DV), BF16),
            jax.ShapeDtypeStruct((batch, M_HEADS, M_DK, M_DV), F32),
            jax.ShapeDtypeStruct((batch, M_HEADS, M_DK, M_DV), F32),
            jax.ShapeDtypeStruct((batch, 8, LANES), F32),
        ),
        grid=(batch // S, seq // L),
        in_specs=[
            pl.BlockSpec((S, L, 256), lambda b, c: (b, c, C_MQ // 256)),
            pl.BlockSpec((S, L, 256), lambda b, c: (b, c, C_MK // 256)),
            pl.BlockSpec((S, L, 512), lambda b, c: (b, c, C_MV // 512)),
            pl.BlockSpec((S, L, 512), lambda b, c: (b, c, C_MO // 512)),
            pl.BlockSpec((S, L, LANES), lambda b, c: (b, c, C_GATE // LANES)),
            pl.BlockSpec((1, LANES), lambda b, c: (0, 0)),
            pl.BlockSpec((1, M_HEADS * M_DV), lambda b, c: (0, 0)),
        ],
        out_specs=(
            pl.BlockSpec((S, L, M_HEADS * M_DV), lambda b, c: (b, c, 0)),
            pl.BlockSpec((S, M_HEADS, M_DK, M_DV), lambda b, c: (b, 0, 0, 0)),
            pl.BlockSpec((S, M_HEADS, M_DK, M_DV), lambda b, c: (b, 0, 0, 0)),
            pl.BlockSpec((S, 8, LANES), lambda b, c: (b, 0, 0)),
        ),
        scratch_shapes=[pltpu.VMEM((S * M_HEADS, M_DK, 2 * M_DV), F32), pltpu.VMEM((S, 8, LANES), F32)],
        compiler_params=_cparams(("arbitrary", "arbitrary")),
        name="mlstm_prompt",
    )(z3, z3, z3, z3, z3, bias_row, gain_row)


def _attn_prompt_kernel(relb_ref, sink_ref, q_ref, kp_ref, kc_ref, vp_ref, vc_ref, bucket_ref,
                        out_ref, bias_scr):
    B = ATT_BLOCK
    j = pl.program_id(1)

    @pl.when((pl.program_id(0) == 0) & (j == 0))
    def _():
        bucket = bucket_ref[...]
        for h in range(A_HEADS):
            acc = jnp.full((B, 2 * B), NEG, F32)
            for bk in range(NUM_BUCKETS):
                acc = jnp.where(bucket == bk, relb_ref[bk * A_HEADS + h], acc)
            bias_scr[h] = acc

    scale = A_HD ** -0.5
    s_iota = lax.broadcasted_iota(I32, (B, 2 * B), 1)
    first = jnp.where((s_iota < B) & (j == 0), NEG, 0.0)
    for sub in range(ATT_QBLOCKS):
        rows = slice(sub * B, (sub + 1) * B)
        prev_rows = slice((sub - 1) * B, sub * B)
        outs = []
        for h in range(A_HEADS):
            g = h // A_GROUP
            cols = slice(g * A_HD, (g + 1) * A_HD)
            q_h = (q_ref[rows, h * A_HD:(h + 1) * A_HD] * scale).astype(BF16)
            k_prev = kp_ref[:, cols] if sub == 0 else kc_ref[prev_rows, cols]
            v_prev = vp_ref[:, cols] if sub == 0 else vc_ref[prev_rows, cols]
            k2 = jnp.concatenate([k_prev, kc_ref[rows, cols]], axis=0).astype(BF16)
            v2 = jnp.concatenate([v_prev, vc_ref[rows, cols]], axis=0).astype(BF16)
            logits = lax.dot_general(q_h, k2, (((1,), (1,)), ((), ())), preferred_element_type=F32)
            logits = logits + bias_scr[h]
            if sub == 0:
                logits = logits + first
            sink = sink_ref[h]
            m = jnp.maximum(jnp.max(logits, axis=-1, keepdims=True), sink)
            p = jnp.exp(logits - m)
            den = jnp.sum(p, axis=-1, keepdims=True) + jnp.exp(sink - m)
            o = jnp.dot(p.astype(BF16), v2, preferred_element_type=F32) / den
            outs.append(o)
        out_ref[rows, :] = jnp.concatenate(outs, axis=1).astype(out_ref.dtype)


def _attn_prompt(z, rel_bias, sinks, batch, seq):
    B = ATT_BLOCK
    nb = seq // B
    qi = np.arange(B)[:, None]
    si = np.arange(2 * B)[None, :]
    dist = qi + B - si
    bucket = np.where((dist >= 0) & (dist <= WINDOW), _t5_bucket_np(dist), -1).astype(np.int32)
    Q = ATT_QBLOCKS
    ns = nb // Q
    cur = lambda b, j, *_: b * ns + j
    prev = lambda b, j, *_: b * nb + jnp.maximum(Q * j - 1, 0)
    grid_spec = pltpu.PrefetchScalarGridSpec(
        num_scalar_prefetch=2,
        grid=(batch, ns),
        in_specs=[
            pl.BlockSpec((Q * B, 512), lambda b, j, *_: (cur(b, j), C_AQ // 512)),
            pl.BlockSpec((B, LANES), lambda b, j, *_: (prev(b, j), C_AK // LANES)),
            pl.BlockSpec((Q * B, LANES), lambda b, j, *_: (cur(b, j), C_AK // LANES)),
            pl.BlockSpec((B, LANES), lambda b, j, *_: (prev(b, j), C_AV // LANES)),
            pl.BlockSpec((Q * B, LANES), lambda b, j, *_: (cur(b, j), C_AV // LANES)),
            pl.BlockSpec((B, 2 * B), lambda b, j, *_: (0, 0)),
        ],
        out_specs=pl.BlockSpec((Q * B, A_HEADS * A_HD), lambda b, j, *_: (cur(b, j), 0)),
        scratch_shapes=[pltpu.VMEM((A_HEADS, B, 2 * B), F32)],
    )
    return pl.pallas_call(
        _attn_prompt_kernel,
        out_shape=jax.ShapeDtypeStruct((batch * seq, A_HEADS * A_HD), BF16),
        grid_spec=grid_spec,
        compiler_params=_cparams(("arbitrary", "arbitrary")),
        name="attn_prompt",
    )(rel_bias.reshape(-1), sinks, z, z, z, z, z, jnp.asarray(bucket))


def _outproj_router_kernel(x_ref, mo_ref, ao_ref, wm_ref, wa_ref, g_ref, wr_ref, br_ref, cin_ref, *rest):
    y_ref, yn_ref, eidx_ref, gate_ref, rank_ref, cout_ref, carry = rest[-7:]
    T = x_ref.shape[0]
    i = pl.program_id(0)

    @pl.when(i == 0)
    def _():
        carry[...] = cin_ref[...]

    y = (x_ref[...] + jnp.dot(mo_ref[...], wm_ref[...], preferred_element_type=F32)
         + jnp.dot(ao_ref[...], wa_ref[...], preferred_element_type=F32))
    y_ref[...] = y
    yn = _rms(y, g_ref[...])
    for s in range(ROW_TILES):
        yn_ref[pl.ds(s, T, stride=ROW_TILES), :] = yn[:, s * LANES:(s + 1) * LANES]
    yh = yn.astype(BF16)
    yl = (yn - yh.astype(F32)).astype(BF16)
    hh = jnp.dot(yh, wr_ref[...], preferred_element_type=F32)
    lh = jnp.dot(yl, wr_ref[:, :LANES], preferred_element_type=F32)
    logits = hh[:, :LANES] + (hh[:, LANES:] + lh) + br_ref[...]
    lane = lax.broadcasted_iota(I32, (T, LANES), 1)
    lane_f = lane.astype(F32)
    vals, idxs, hots = [], [], []
    l = logits
    for _ in range(TOP_K):
        mx = jnp.max(l, axis=-1, keepdims=True)
        idx = jnp.min(jnp.where(l == mx, lane_f, float(LANES)), axis=-1, keepdims=True)
        hot = lane_f == idx
        l = jnp.where(hot, -jnp.inf, l)
        vals.append(mx)
        idxs.append(idx)
        hots.append(hot)
    es = [jnp.exp(v - vals[0]) for v in vals]
    tot = es[0] + es[1] + es[2] + es[3]
    sel = jnp.where(hots[0] | hots[1] | hots[2] | hots[3], 1.0, 0.0)
    row = lax.broadcasted_iota(I32, (T, T), 0)
    col = lax.broadcasted_iota(I32, (T, T), 1)
    strict = (col < row).astype(BF16)
    before = carry[...] + jnp.dot(strict, sel.astype(BF16), preferred_element_type=F32)
    per_row = LANES // TOP_K
    tok_in_row = lax.broadcasted_iota(I32, (T, LANES), 0) % per_row
    gate = jnp.zeros((T, LANES), F32)
    e_sp = jnp.zeros((T, LANES), F32)
    r_sp = jnp.zeros((T, LANES), F32)
    for k in range(TOP_K):
        r_k = jnp.sum(jnp.where(hots[k], before, 0.0), axis=-1, keepdims=True)
        gate = jnp.where(lane == k, es[k] / tot, gate)
        mine = lane == tok_in_row * TOP_K + k
        e_sp = jnp.where(mine, idxs[k], e_sp)
        r_sp = jnp.where(mine, r_k, r_sp)
    fold = (lax.broadcasted_iota(I32, (T // per_row, T), 1) // per_row
            == lax.broadcasted_iota(I32, (T // per_row, T), 0)).astype(F32)
    exact = dict(preferred_element_type=F32, precision=lax.Precision.HIGHEST)
    eidx_ref[...] = jnp.dot(fold, e_sp, **exact).astype(I32)
    rank_ref[...] = jnp.dot(fold, r_sp, **exact).astype(I32)
    gate_ref[...] = gate
    carry[...] = carry[...] + jnp.sum(sel, axis=0, keepdims=True)
    cout_ref[...] = carry[...]


def _outproj_router(x2, m_out, a_out, w_m, w_a, norm_row, w_r, b_r, counts_in, tile, yn_rows, yn_prev=None):
    n = x2.shape[0]
    tok = lambda i: (i, 0)
    fix = lambda i: (0, 0)
    assert yn_prev is None or (yn_rows - n) % tile == 0
    row0 = 0 if yn_prev is None else (yn_rows - n) // tile
    extra_specs = [] if yn_prev is None else [pl.BlockSpec(memory_space=pl.ANY)]
    extra_args = [] if yn_prev is None else [yn_prev]
    return pl.pallas_call(
        _outproj_router_kernel,
        out_shape=(
            jax.ShapeDtypeStruct((n, D_MODEL), F32),
            jax.ShapeDtypeStruct((yn_rows * ROW_TILES, LANES), F32),
            jax.ShapeDtypeStruct((n * TOP_K // LANES, LANES), I32),
            jax.ShapeDtypeStruct((n, LANES), F32),
            jax.ShapeDtypeStruct((n * TOP_K // LANES, LANES), I32),
            jax.ShapeDtypeStruct((1, LANES), F32),
        ),
        grid=(n // tile,),
        in_specs=[
            pl.BlockSpec((tile, D_MODEL), tok),
            pl.BlockSpec((tile, 512), tok),
            pl.BlockSpec((tile, 512), tok),
            pl.BlockSpec((512, D_MODEL), fix),
            pl.BlockSpec((512, D_MODEL), fix),
            pl.BlockSpec((1, D_MODEL), fix),
            pl.BlockSpec((D_MODEL, 2 * LANES), fix),
            pl.BlockSpec((1, LANES), fix),
            pl.BlockSpec((1, LANES), fix),
        ] + extra_specs,
        out_specs=(
            pl.BlockSpec((tile, D_MODEL), tok),
            pl.BlockSpec((tile * ROW_TILES, LANES), lambda i: (row0 + i, 0)),
            pl.BlockSpec((tile * TOP_K // LANES, LANES), tok),
            pl.BlockSpec((tile, LANES), tok),
            pl.BlockSpec((tile * TOP_K // LANES, LANES), tok),
            pl.BlockSpec((1, LANES), fix),
        ),
        scratch_shapes=[pltpu.VMEM((1, LANES), F32)],
        input_output_aliases={} if yn_prev is None else {9: 1},
        compiler_params=_cparams(("arbitrary",)),
        name="outproj_router",
    )(x2, m_out, a_out, w_m, w_a, norm_row, w_r, b_r, counts_in, *extra_args)


def _ffn_kernel(be_ref, nused_ref, valid_ref, nxt_ref, gp_ref, gs_ref, gn_ref, order_ref,
                yn_hbm, wg_hbm, bg_ref, wu_hbm, bu_ref, wd_hbm, bd_ref, out_ref,
                wbuf, wg_bf, wu_bf, wd_bf, h_scr, xbuf0, xbuf1, xbuf2, wsem, gsem):
    i = pl.program_id(0)
    slot = lax.rem(i, FFN_BUFS)
    R = FFN_ROWS
    w_hbm = (wg_hbm, wu_hbm, wd_hbm)
    w_bf = (wg_bf, wu_bf, wd_bf)
    xbuf = (xbuf0, xbuf1, xbuf2)

    def weight_copies(e):
        return [pltpu.make_async_copy(w_hbm[j].at[e], wbuf.at[j], wsem.at[j]) for j in range(3)]

    def gather_row(b, r, dst_slot):
        tok = order_ref[jnp.where(r < gn_ref[b], gp_ref[b], gs_ref[b]) + r]
        return pltpu.make_async_copy(yn_hbm.at[tok], xbuf[dst_slot].at[pl.ds(r * ROW_TILES, ROW_TILES)],
                                     gsem.at[dst_slot])

    def gather_rolled(b, dst_slot):
        def body(rb, carry):
            for u in range(DMA_UNROLL):
                gather_row(b, rb * DMA_UNROLL + u, dst_slot).start(priority=u % 2)
            return carry
        lax.fori_loop(0, R // DMA_UNROLL, body, 0)

    def wait_rows(s):
        pltpu.make_async_copy(xbuf[(s + 1) % FFN_BUFS], xbuf[s], gsem.at[s]).wait()

    def by_slot(fn):
        for s in range(FFN_BUFS):
            @pl.when(slot == s)
            def _():
                fn(s)

    @pl.when(i < nused_ref[0])
    def _():
        @pl.when((i == 0) | (be_ref[i] != be_ref[jnp.maximum(i - 1, 0)]))
        def _():
            @pl.when(i == 0)
            def _():
                for cp in weight_copies(be_ref[0]):
                    cp.start()

            for cp in weight_copies(be_ref[i]):
                cp.wait()

            for j in range(3):
                for r in range(0, D_MODEL, CAST_ROWS):
                    w_bf[j][r:r + CAST_ROWS, :] = wbuf[j, r:r + CAST_ROWS, :].astype(BF16)

            @pl.when(nxt_ref[i] >= 0)
            def _():
                for cp in weight_copies(nxt_ref[i]):
                    cp.start()

        @pl.when(i == 0)
        def _():
            for b in range(FFN_BUFS - 1):
                gather_rolled(b, b)

        by_slot(wait_rows)

        def ffn_pass(rows, slot):
            x = jnp.concatenate([xbuf[slot][pl.ds(s, rows, stride=ROW_TILES), :].astype(BF16)
                                 for s in range(ROW_TILES)], axis=1)
            ahead = i + (FFN_BUFS - 1)
            g_p, g_s, g_n = gp_ref[ahead], gs_ref[ahead], gn_ref[ahead]
            dst = (slot + FFN_BUFS - 1) % FFN_BUFS
            for r in range(R):
                tok = order_ref[jnp.where(r < g_n, g_p, g_s) + r]
                pltpu.make_async_copy(yn_hbm.at[tok], xbuf[dst].at[pl.ds(r * ROW_TILES, ROW_TILES)],
                                      gsem.at[dst]).start(priority=r % 2)
            for c in range(D_FF // FFN_COLS):
                cs = slice(c * FFN_COLS, (c + 1) * FFN_COLS)
                g = jnp.dot(x, wg_bf[:, cs], preferred_element_type=F32) + bg_ref[:, cs]
                u = jnp.dot(x, wu_bf[:, cs], preferred_element_type=F32) + bu_ref[:, cs]
                g = jnp.minimum(g, SWIGLU_LIMIT)
                u = jnp.clip(u, -SWIGLU_LIMIT, SWIGLU_LIMIT)
                h_scr[0:rows, cs] = ((u + 1.0) * (g * _sigmoid(SWIGLU_ALPHA * g))).astype(BF16)
            out = jnp.dot(h_scr[0:rows, :], wd_bf[...], preferred_element_type=F32) + bd_ref[...]
            for s in range(ROW_TILES):
                out_ref[pl.ds(s, rows, stride=ROW_TILES), :] = out[:, s * LANES:(s + 1) * LANES]

        for s in range(FFN_BUFS):
            for rows in range(FFN_SUB, R + 1, FFN_SUB):
                @pl.when((slot == s) & (valid_ref[i] > rows - FFN_SUB) & (valid_ref[i] <= rows))
                def _():
                    ffn_pass(rows, s)

        @pl.when(i == nused_ref[0] - 1)
        def _():
            def drain(s):
                for d in range(1, FFN_BUFS):
                    wait_rows((s + d) % FFN_BUFS)
            by_slot(drain)


def _ffn(plan, order, yn_tiles, wg, bg, wu, bu, wd, bd, n_blocks):
    blk = lambda i, be, nu, *_: (jnp.minimum(i, nu[0] - 1), 0)
    bsel = lambda i, be, *_: (be[i], 0, 0)
    hbm = pl.BlockSpec(memory_space=pl.ANY)
    grid_spec = pltpu.PrefetchScalarGridSpec(
        num_scalar_prefetch=8,
        grid=(n_blocks,),
        in_specs=[
            hbm,
            hbm,
            pl.BlockSpec((None, 1, D_FF), bsel),
            hbm,
            pl.BlockSpec((None, 1, D_FF), bsel),
            hbm,
            pl.BlockSpec((None, 1, D_MODEL), bsel),
        ],
        out_specs=pl.BlockSpec((FFN_ROWS * ROW_TILES, LANES), blk),
        scratch_shapes=[pltpu.VMEM((3, D_MODEL, D_FF), F32),
                        pltpu.VMEM((D_MODEL, D_FF), BF16), pltpu.VMEM((D_MODEL, D_FF), BF16),
                        pltpu.VMEM((D_FF, D_MODEL), BF16), pltpu.VMEM((FFN_ROWS, D_FF), BF16),
                        pltpu.VMEM((FFN_ROWS * ROW_TILES, LANES), F32),
                        pltpu.VMEM((FFN_ROWS * ROW_TILES, LANES), F32),
                        pltpu.VMEM((FFN_ROWS * ROW_TILES, LANES), F32),
                        pltpu.SemaphoreType.DMA((3,)), pltpu.SemaphoreType.DMA((FFN_BUFS,))],
    )
    return pl.pallas_call(
        _ffn_kernel,
        out_shape=jax.ShapeDtypeStruct((n_blocks * FFN_ROWS * ROW_TILES, LANES), F32),
        grid_spec=grid_spec,
        compiler_params=_cparams(("arbitrary",)),
        name="moe_ffn",
    )(*plan, order, yn_tiles, wg, bg, wu, bu, wd, bd)


def _combine_kernel(dest_ref, y_ref, gate_ref, fn_ref, ffn_ref, out_ref, buf, sem):
    T = y_ref.shape[0]
    i = pl.program_id(0)
    n = pl.num_programs(0)
    slot = i % 2

    def issue(tile, s):
        base = tile * (T * TOP_K)

        def body(tb, carry):
            for u in range(DMA_UNROLL):
                t = tb * DMA_UNROLL + u
                for k in range(TOP_K):
                    d = dest_ref[base + t * TOP_K + k]
                    pltpu.make_async_copy(ffn_ref.at[d], buf.at[s, k, pl.ds(t * ROW_TILES, ROW_TILES)],
                                          sem.at[s]).start(priority=k % 2)
            return carry

        lax.fori_loop(0, T // DMA_UNROLL, body, 0)

    @pl.when(i == 0)
    def _():
        issue(0, 0)

    @pl.when(i + 1 < n)
    def _():
        issue(i + 1, 1 - slot)

    for k in range(TOP_K):
        pltpu.make_async_copy(buf.at[1 - slot, k], buf.at[slot, k], sem.at[slot]).wait()

    acc = y_ref[...]
    gate = gate_ref[...]
    for k in range(TOP_K):
        rows = jnp.concatenate([buf[slot, k, pl.ds(s, T, stride=ROW_TILES), :] for s in range(ROW_TILES)],
                               axis=1)
        acc = acc + gate[:, k:k + 1] * rows
    out_ref[...] = _rms(acc, fn_ref[...])


def _combine(dest_flat, y, gate, fnorm_row, ffn_out, tile):
    n = y.shape[0]
    grid_spec = pltpu.PrefetchScalarGridSpec(
        num_scalar_prefetch=1,
        grid=(n // tile,),
        in_specs=[
            pl.BlockSpec((tile, D_MODEL), lambda i, *_: (i, 0)),
            pl.BlockSpec((tile, LANES), lambda i, *_: (i, 0)),
            pl.BlockSpec((1, D_MODEL), lambda i, *_: (0, 0)),
            pl.BlockSpec(memory_space=pl.ANY),
        ],
        out_specs=pl.BlockSpec((tile, D_MODEL), lambda i, *_: (i, 0)),
        scratch_shapes=[pltpu.VMEM((2, TOP_K, tile * ROW_TILES, LANES), F32), pltpu.SemaphoreType.DMA((2,))],
    )
    return pl.pallas_call(
        _combine_kernel,
        out_shape=jax.ShapeDtypeStruct((n, D_MODEL), F32),
        grid_spec=grid_spec,
        compiler_params=_cparams(("arbitrary",)),
        name="moe_combine",
    )(dest_flat, y, gate, fnorm_row, ffn_out)


def _mlstm_step_kernel(q_ref, k_ref, v_ref, o_ref, gt_ref, bias_ref, gain_ref, c0_ref, n0_ref, m0_ref,
                       out_ref, c_ref, n_ref, m_ref):
    TB = SAMPLE_MLSTM_TB
    gb = gt_ref[...] + bias_ref[...]
    ls = _log_sigmoid(gb)
    lane = lax.broadcasted_iota(I32, (TB, LANES), 1)
    eye = (lax.broadcasted_iota(I32, (M_DK, M_DK), 0) == lax.broadcasted_iota(I32, (M_DK, M_DK), 1)).astype(F32)
    nt = (((1,), (1,)), ((), ()))
    m_all = jnp.zeros((TB, LANES), F32)
    for h in range(M_HEADS):
        i_pre = gb[:, h:h + 1]
        a = ls[:, M_HEADS + h:M_HEADS + h + 1] + m0_ref[:, h:h + 1]
        mt = jnp.maximum(a, i_pre)
        w_intra = jnp.exp(i_pre - mt)
        w_inter = jnp.exp(a - mt)
        q_h = q_ref[:, h * M_DK:(h + 1) * M_DK]
        k_h = k_ref[:, h * M_DK:(h + 1) * M_DK] * (M_DK ** -0.5)
        v_h = v_ref[:, h * M_DV:(h + 1) * M_DV]
        n0_h = n0_ref[:, h, :]
        s = jnp.sum(q_h * k_h, axis=-1, keepdims=True) * w_intra
        qn = w_inter * jnp.sum(q_h * n0_h, axis=-1, keepdims=True) + s
        den = jnp.maximum(jnp.abs(qn), jnp.exp(-mt))
        q_t = lax.dot_general(eye, q_h, nt, preferred_element_type=F32, precision=lax.Precision.HIGHEST)
        k_t = lax.dot_general(eye, k_h, nt, preferred_element_type=F32, precision=lax.Precision.HIGHEST)
        rows = []
        for b in range(TB):
            c0 = c0_ref[b, h]
            qc = jnp.sum(c0 * q_t[:, b:b + 1], axis=0, keepdims=True)
            v_b = v_h[b:b + 1, :]
            rows.append(w_inter[b:b + 1, :] * qc + s[b:b + 1, :] * v_b)
            c_ref[b, h] = w_inter[b:b + 1, :] * c0 + (w_intra[b:b + 1, :] * k_t[:, b:b + 1]) * v_b
        num = jnp.concatenate(rows, axis=0)
        hh = num / den
        hn = hh * lax.rsqrt(jnp.mean(hh * hh, axis=-1, keepdims=True) + EPS)
        hn = hn * gain_ref[:, h * M_DV:(h + 1) * M_DV]
        out_ref[:, h * M_DV:(h + 1) * M_DV] = (hn * _sigmoid(o_ref[:, h * M_DV:(h + 1) * M_DV])).astype(out_ref.dtype)
        n_ref[:, h * M_DK:(h + 1) * M_DK] = w_inter * n0_h + w_intra * k_h
        m_all = jnp.where(lane == h, mt, m_all)
    m_ref[...] = m_all


def _mlstm_step(zs, bias_row, gain_row, c0, n0, m0):
    TB = SAMPLE_MLSTM_TB
    nb = zs.shape[0]
    tok = lambda i: (i, 0)
    return pl.pallas_call(
        _mlstm_step_kernel,
        out_shape=(
            jax.ShapeDtypeStruct((nb, M_HEADS * M_DV), BF16),
            jax.ShapeDtypeStruct((nb, M_HEADS, M_DK, M_DV), F32),
            jax.ShapeDtypeStruct((nb, M_HEADS * M_DK), F32),
            jax.ShapeDtypeStruct((nb, LANES), F32),
        ),
        grid=(nb // TB,),
        in_specs=[
            pl.BlockSpec((TB, 256), lambda i: (i, C_MQ // 256)),
            pl.BlockSpec((TB, 256), lambda i: (i, C_MK // 256)),
            pl.BlockSpec((TB, 512), lambda i: (i, C_MV // 512)),
            pl.BlockSpec((TB, 512), lambda i: (i, C_MO // 512)),
            pl.BlockSpec((TB, LANES), lambda i: (i, C_GATE // LANES)),
            pl.BlockSpec((1, LANES), lambda i: (0, 0)),
            pl.BlockSpec((1, M_HEADS * M_DV), lambda i: (0, 0)),
            pl.BlockSpec((TB, M_HEADS, M_DK, M_DV), lambda i: (i, 0, 0, 0)),
            pl.BlockSpec((TB, M_HEADS, M_DK), lambda i: (i, 0, 0)),
            pl.BlockSpec((TB, M_HEADS), tok),
        ],
        out_specs=(
            pl.BlockSpec((TB, M_HEADS * M_DV), tok),
            pl.BlockSpec((TB, M_HEADS, M_DK, M_DV), lambda i: (i, 0, 0, 0)),
            pl.BlockSpec((TB, M_HEADS * M_DK), tok),
            pl.BlockSpec((TB, LANES), tok),
        ),
        compiler_params=_cparams(("arbitrary",)),
        name="mlstm_step",
    )(zs, zs, zs, zs, zs, bias_row, gain_row, c0, n0, m0)


def _attn_step_kernel(q_ref, kn_ref, vn_ref, ck_ref, cv_ref, bucket_ref, relt_ref, sink_ref,
                      out_ref, nk_ref, nv_ref, bias_scr):
    TB = SAMPLE_ATT_TB
    W = ck_ref.shape[1]

    @pl.when(pl.program_id(0) == 0)
    def _():
        bucket = jnp.broadcast_to(bucket_ref[...], (A_HEADS, W))
        acc = jnp.zeros((A_HEADS, W), F32)
        for bk in range(NUM_BUCKETS):
            acc = jnp.where(bucket == bk, relt_ref[:, bk:bk + 1], acc)
        bias_scr[...] = acc

    scale = A_HD ** -0.5
    nt = (((1,), (1,)), ((), ()))
    bias = bias_scr[...]
    bias_new = relt_ref[:, 0:1]
    sink = sink_ref[...]
    low = lax.broadcasted_iota(I32, (A_HEADS, 1), 0) < A_GROUP
    for b in range(TB):
        q = q_ref[b]
        qb = q.astype(BF16)
        kc = ck_ref[b]
        vc = cv_ref[b]
        kn = kn_ref[b:b + 1, :]
        vn = vn_ref[b:b + 1, :]
        l0 = lax.dot_general(qb, kc[:, :A_HD].astype(BF16), nt, preferred_element_type=F32)
        l1 = lax.dot_general(qb, kc[:, A_HD:].astype(BF16), nt, preferred_element_type=F32)
        logits = jnp.where(low, l0, l1) * scale + bias
        kn_h = jnp.where(low, kn[:, :A_HD], kn[:, A_HD:])
        vn_h = jnp.where(low, vn[:, :A_HD], vn[:, A_HD:])
        l_new = jnp.sum(q * kn_h, axis=-1, keepdims=True) * scale + bias_new
        m = jnp.maximum(jnp.maximum(jnp.max(logits, axis=-1, keepdims=True), l_new), sink)
        p = jnp.exp(logits - m)
        p_new = jnp.exp(l_new - m)
        den = jnp.sum(p, axis=-1, keepdims=True) + p_new + jnp.exp(sink - m)
        pb = p.astype(BF16)
        o0 = jnp.dot(pb, vc[:, :A_HD].astype(BF16), preferred_element_type=F32)
        o1 = jnp.dot(pb, vc[:, A_HD:].astype(BF16), preferred_element_type=F32)
        o = jnp.where(low, o0, o1) + p_new * vn_h
        out_ref[b] = o / den
        nk_ref[b, 0:W - 1, :] = ck_ref[b, 1:W, :]
        nk_ref[b, W - 1:W, :] = kn
        nv_ref[b, 0:W - 1, :] = cv_ref[b, 1:W, :]
        nv_ref[b, W - 1:W, :] = vn


def _attn_step(q3, k_new, v_new, ck, cv, rel_bias, sinks):
    TB = SAMPLE_ATT_TB
    nb, W = ck.shape[0], ck.shape[1]
    bucket = _t5_bucket_np(W - np.arange(W))[None, :].astype(np.int32)
    tok = lambda i: (i, 0)
    tok3 = lambda i: (i, 0, 0)
    fix = lambda i: (0, 0)
    return pl.pallas_call(
        _attn_step_kernel,
        out_shape=(
            jax.ShapeDtypeStruct((nb, A_HEADS, A_HD), F32),
            jax.ShapeDtypeStruct(ck.shape, F32),
            jax.ShapeDtypeStruct(cv.shape, F32),
        ),
        grid=(nb // TB,),
        in_specs=[
            pl.BlockSpec((TB, A_HEADS, A_HD), tok3),
            pl.BlockSpec((TB, LANES), lambda i: (i, C_AK // LANES)),
            pl.BlockSpec((TB, LANES), lambda i: (i, C_AV // LANES)),
            pl.BlockSpec((TB, W, A_KV * A_HD), tok3),
            pl.BlockSpec((TB, W, A_KV * A_HD), tok3),
            pl.BlockSpec((1, W), fix),
            pl.BlockSpec((A_HEADS, NUM_BUCKETS), fix),
            pl.BlockSpec((A_HEADS, 1), fix),
        ],
        out_specs=(
            pl.BlockSpec((TB, A_HEADS, A_HD), tok3),
            pl.BlockSpec((TB, W, A_KV * A_HD), tok3),
            pl.BlockSpec((TB, W, A_KV * A_HD), tok3),
        ),
        scratch_shapes=[pltpu.VMEM((A_HEADS, W), F32)],
        compiler_params=_cparams(("arbitrary",)),
        name="attn_step",
    )(q3, k_new, v_new, ck, cv, jnp.asarray(bucket), rel_bias.T, sinks.reshape(A_HEADS, 1))


def _reorder_w_in(w_in):
    o = 0
    parts = {}
    for name, width in (("mq", 256), ("mk", 256), ("mv", 512), ("mo", 512), ("mi", 4), ("mf", 4),
                        ("aq", 512), ("ak", 128), ("av", 128)):
        parts[name] = w_in[:, o:o + width]
        o += width
    pad = jnp.zeros((w_in.shape[0], LANES - 2 * M_HEADS), w_in.dtype)
    cols = [parts[n] for n in ("mq", "mk", "mv", "mo", "aq", "ak", "av", "mi", "mf")] + [pad]
    return jnp.concatenate(cols, axis=1).astype(BF16)


def _lane_row(v, fill=0.0):
    return jnp.concatenate([v.astype(F32), jnp.full((LANES - v.shape[0],), fill, F32)])[None, :]


def kernel(x_prompt, x_sample, state_C, state_n, state_m, cache_k, cache_v, rel_bias, norm1, w_in, b_if,
           m_gain, sinks, w_out, norm2, w_router, b_router, w_gate, b_gate, w_up, b_up, w_down, b_down,
           final_norm):
    assert norm1.shape[0] == 1, "single-layer trunk"
    batch, seq, _ = x_prompt.shape
    nsmp = x_sample.shape[0]
    n_p = batch * seq
    W = cache_k.shape[2]

    xp = x_prompt.reshape(n_p, D_MODEL)
    xs_ = x_sample.reshape(nsmp, D_MODEL)
    w_in_r = _reorder_w_in(w_in[0])
    n1 = norm1[0][None, :]
    n2 = norm2[0][None, :]
    fn = final_norm[None, :]
    bias_row = _lane_row(b_if[0])
    gain_row = m_gain[0][None, :]
    w_m = w_out[0][:M_HEADS * M_DV].astype(BF16)
    w_a = w_out[0][M_HEADS * M_DV:].astype(BF16)
    w_r32 = jnp.concatenate([w_router[0], jnp.zeros((D_MODEL, LANES - N_EXPERTS), F32)], axis=1)
    w_r_hi = w_r32.astype(BF16)
    w_r = jnp.concatenate([w_r_hi, (w_r32 - w_r_hi.astype(F32)).astype(BF16)], axis=1)
    b_r = _lane_row(b_router[0], NEG)

    zp = _inproj(xp, n1, w_in_r, TOK_TILE)
    zs = _inproj(xs_, n1, w_in_r, nsmp)
    m_out_p, p_c, p_nrep, p_mrep = _mlstm_prompt(zp, bias_row, gain_row, batch, seq)
    m_out_p = m_out_p.reshape(n_p, M_HEADS * M_DV)
    a_out_p = _attn_prompt(zp, rel_bias, sinks[0], batch, seq)
    m_out_s, s_c, s_n, s_mrep = _mlstm_step(zs, bias_row, gain_row, state_C[0], state_n[0], state_m[0])
    q3 = zs[:, C_AQ:C_AQ + A_HEADS * A_HD].reshape(nsmp, A_HEADS, A_HD)
    a3, s_k, s_v = _attn_step(q3, zs, zs, cache_k[0].reshape(nsmp, W, A_KV * A_HD),
                              cache_v[0].reshape(nsmp, W, A_KV * A_HD), rel_bias, sinks[0])
    a_out_s = a3.reshape(nsmp, A_HEADS * A_HD).astype(BF16)

    n_tot = n_p + nsmp
    zero_counts = jnp.zeros((1, LANES), F32)
    y_p, yn_all, e_p, g_p, r_p, cnt_p = _outproj_router(xp, m_out_p, a_out_p, w_m, w_a, n2, w_r, b_r,
                                                        zero_counts, TOK_TILE, n_tot)
    y_s, yn_all, e_s, g_s, r_s, cnt = _outproj_router(xs_, m_out_s, a_out_s, w_m, w_a, n2, w_r, b_r,
                                                      cnt_p, nsmp, n_tot, yn_all)

    counts = cnt[0, :N_EXPERTS].astype(I32)
    counts_p = cnt_p[0, :N_EXPERTS].astype(I32)
    counts_s = counts - counts_p
    start_p = jnp.cumsum(counts_p) - counts_p
    start_s = jnp.cumsum(counts_s) - counts_s
    nblk = (counts + FFN_ROWS - 1) // FFN_ROWS
    blk_end = jnp.cumsum(nblk)
    ex = jnp.arange(N_EXPERTS, dtype=I32)

    def per_slot(e, r, table):
        hot = e[:, :, None] == ex
        return jnp.sum(jnp.where(hot, table, 0), axis=-1) + r

    order = jnp.concatenate([
        jnp.argsort(per_slot(e_p, r_p, start_p).reshape(-1)).astype(I32) // TOP_K,
        jnp.argsort(per_slot(e_s, r_s, start_s - counts_p).reshape(-1)).astype(I32) // TOP_K + n_p,
        jnp.zeros((2 * FFN_ROWS,), I32)])
    pad_start = (blk_end - nblk) * FFN_ROWS
    dest_p = per_slot(e_p, r_p, pad_start).reshape(-1).astype(I32)
    dest_s = per_slot(e_s, r_s, pad_start).reshape(-1).astype(I32)

    n_blocks = (n_tot * TOP_K + N_EXPERTS * (FFN_ROWS - 1) + FFN_ROWS - 1) // FFN_ROWS
    nused = jnp.maximum(blk_end[-1], 1).astype(I32)
    bi = jnp.minimum(jnp.arange(n_blocks + FFN_BUFS - 1, dtype=I32), nused - 1)
    block_e = jnp.minimum(jnp.sum((bi[:, None] >= blk_end[None, :]).astype(I32), axis=1), N_EXPERTS - 1)
    hot = block_e[:, None] == ex
    pick = lambda v: jnp.sum(jnp.where(hot, v, 0), axis=1)
    row0 = (bi - pick(blk_end - nblk)) * FFN_ROWS
    blk_valid = jnp.clip(pick(counts) - row0, 0, FFN_ROWS).astype(I32)
    real = jnp.arange(n_blocks + FFN_BUFS - 1, dtype=I32) < nused
    blk_gn = jnp.where(real, jnp.clip(pick(counts_p) - row0, 0, FFN_ROWS), 0).astype(I32)
    blk_gp = jnp.where(real, pick(start_p) + row0, 0).astype(I32)
    blk_gs = jnp.where(real, n_p * TOP_K + pick(start_s - counts_p) + row0, n_tot * TOP_K).astype(I32)
    later = jnp.where((ex[None, :] > ex[:, None]) & (counts[None, :] > 0), ex[None, :], N_EXPERTS)
    next_e = jnp.min(later, axis=1)
    blk_next = pick(jnp.where(next_e < N_EXPERTS, next_e, -1)).astype(I32)
    plan = (block_e[:n_blocks].astype(I32), nused.reshape(1), blk_valid[:n_blocks], blk_next[:n_blocks],
            blk_gp, blk_gs, blk_gn)

    as_tiles = lambda a: a.reshape(-1, ROW_TILES, LANES)
    ffn_out = _ffn(plan, order, as_tiles(yn_all), w_gate[0], b_gate[0][:, None, :], w_up[0], b_up[0][:, None, :],
                   w_down[0], b_down[0][:, None, :], n_blocks)
    out_p = _combine(dest_p, y_p, g_p, fn, as_tiles(ffn_out), COMBINE_TILE)
    out_s = _combine(dest_s, y_s, g_s, fn, as_tiles(ffn_out), min(COMBINE_TILE, nsmp))

    kv_shape = (1, batch, WINDOW, A_KV, A_HD)
    zp3 = zp.reshape(batch, seq, PROJ_W)
    zk = zp3[:, seq - WINDOW:, C_AK:C_AK + A_KV * A_HD]
    zv = zp3[:, seq - WINDOW:, C_AV:C_AV + A_KV * A_HD]
    return (
        out_p.reshape(batch, seq, D_MODEL),
        out_s.reshape(nsmp, 1, D_MODEL),
        p_c[None],
        p_nrep[None, :, :, :, 0],
        p_mrep[None, :, :M_HEADS, 0],
        zk.reshape(kv_shape),
        zv.reshape(kv_shape),
        s_c[None],
        s_n.reshape(1, nsmp, M_HEADS, M_DK),
        s_mrep[None, :, :M_HEADS],
        s_k.reshape(1, nsmp, W, A_KV, A_HD),
        s_v.reshape(1, nsmp, W, A_KV, A_HD),
    )
```

```python
import math

import numpy as np
import jax
import jax.numpy as jnp
from jax import lax
from jax.experimental import pallas as pl
from jax.experimental.pallas import tpu as pltpu

F32 = jnp.float32
BF16 = jnp.bfloat16
I32 = jnp.int32

D_MODEL = 1024
M_HEADS = 4
M_DK = 64
M_DV = 128
A_HEADS = 8
A_KV = 2
A_GROUP = A_HEADS // A_KV
A_HD = 64
WINDOW = 128
NUM_BUCKETS = 32
MAX_DISTANCE = 128
N_EXPERTS = 32
TOP_K = 4
D_FF = 1024
SWIGLU_LIMIT = 7.0
SWIGLU_ALPHA = 1.702
EPS = 1e-5

LANES = 128
NEG = -1e30
VMEM_LIMIT = 52 * 1024 * 1024

C_MQ, C_MK, C_MV, C_MO, C_AQ, C_AK, C_AV, C_GATE = 0, 256, 512, 1024, 1536, 2048, 2176, 2304
PROJ_W = 2432

MLSTM_CHUNK = LANES
MLSTM_SEQS = 8
ATT_BLOCK = 128
ATT_QBLOCKS = 2
TOK_TILE = 512
FFN_ROWS = 512
FFN_SUB = 256
FFN_BUFS = 3
ROW_TILES = D_MODEL // LANES
FFN_COLS = 256
CAST_ROWS = 128
COMBINE_TILE = 256
DMA_UNROLL = 8
SAMPLE_MLSTM_TB = 16
SAMPLE_ATT_TB = 8


def _t5_bucket_np(dist):
    n = np.maximum(dist, 0)
    max_exact = NUM_BUCKETS // 2
    ratio = np.log(np.maximum(n, 1).astype(np.float32) / np.float32(max_exact)) / np.float32(
        math.log(MAX_DISTANCE / max_exact))
    large = max_exact + (ratio * np.float32(NUM_BUCKETS - max_exact)).astype(np.int32)
    large = np.minimum(large, NUM_BUCKETS - 1)
    return np.where(n < max_exact, n, large).astype(np.int32)


def _cparams(sem):
    return pltpu.CompilerParams(dimension_semantics=sem, vmem_limit_bytes=VMEM_LIMIT)


def _rms(x, g):
    return x * lax.rsqrt(jnp.mean(x * x, axis=-1, keepdims=True) + EPS) * g


def _log_sigmoid(x):
    return jnp.minimum(x, 0.0) - jnp.log(1.0 + jnp.exp(-jnp.abs(x)))


def _sigmoid(x):
    return 1.0 / (1.0 + jnp.exp(-x))


def _inproj_kernel(x_ref, g_ref, w_ref, z_ref):
    xn = _rms(x_ref[...], g_ref[...]).astype(BF16)
    z_ref[...] = jnp.dot(xn, w_ref[...], preferred_element_type=F32)


def _inproj(x2, norm_row, w_bf16, tile):
    n = x2.shape[0]
    return pl.pallas_call(
        _inproj_kernel,
        out_shape=jax.ShapeDtypeStruct((n, PROJ_W), F32),
        grid=(n // tile,),
        in_specs=[
            pl.BlockSpec((tile, D_MODEL), lambda i: (i, 0)),
            pl.BlockSpec((1, D_MODEL), lambda i: (0, 0)),
            pl.BlockSpec((D_MODEL, PROJ_W), lambda i: (0, 0)),
        ],
        out_specs=pl.BlockSpec((tile, PROJ_W), lambda i: (i, 0)),
        compiler_params=_cparams(("arbitrary",)),
        name="inproj",
    )(x2, norm_row, w_bf16)


def _mlstm_prompt_kernel(q_ref, k_ref, v_ref, o_ref, gt_ref, bias_ref, gain_ref,
                         out_ref, c_ref, n_ref, m_ref, s_scr, m_scr):
    L = MLSTM_CHUNK
    c = pl.program_id(1)

    @pl.when(c == 0)
    def _():
        s_scr[...] = jnp.zeros_like(s_scr)
        m_scr[...] = jnp.zeros_like(m_scr)

    row = lax.broadcasted_iota(I32, (L, L), 0)
    col = lax.broadcasted_iota(I32, (L, L), 1)
    causal = col <= row
    tril = causal.astype(F32)
    ones = jnp.ones((L, M_DV), BF16)

    for nb in range(MLSTM_SEQS):
        gb = gt_ref[nb] + bias_ref[...]
        ls = _log_sigmoid(gb)
        bcum = jnp.dot(tril, ls, preferred_element_type=F32, precision=lax.Precision.HIGHEST)
        gb_t = gb.T
        bcum_t = bcum.T
        k_t = (k_ref[nb] * (M_DK ** -0.5)).T
        for h in range(M_HEADS):
            sh = nb * M_HEADS + h
            b_rep = jnp.broadcast_to(bcum[:, M_HEADS + h:M_HEADS + h + 1], (L, LANES))
            b_row = bcum_t[M_HEADS + h:M_HEADS + h + 1, :]
            i_row = gb_t[h:h + 1, :]
            m_prev = m_scr[nb, h:h + 1, :]
            dmat = jnp.where(causal, b_rep + (i_row - b_row), NEG)
            a_rep = b_rep + m_prev
            mt = jnp.maximum(a_rep, jnp.broadcast_to(jnp.max(dmat, axis=1, keepdims=True), (L, LANES)))
            w_intra = jnp.exp(dmat - mt)
            w_inter = jnp.exp(a_rep - mt)
            q_h = q_ref[nb, :, h * M_DK:(h + 1) * M_DK].astype(BF16)
            kt_h = k_t[h * M_DK:(h + 1) * M_DK, :]
            qk = jnp.dot(q_h, kt_h.astype(BF16), preferred_element_type=F32)
            s_w = (qk * w_intra).astype(BF16)
            v_ext = jnp.concatenate([v_ref[nb, :, h * M_DV:(h + 1) * M_DV].astype(BF16), ones], axis=1)
            state = s_scr[sh]
            inter = jnp.dot(q_h, state.astype(BF16), preferred_element_type=F32)
            intra = jnp.dot(s_w, v_ext, preferred_element_type=F32)
            num = w_inter * inter[:, :M_DV] + intra[:, :M_DV]
            qn = w_inter * inter[:, M_DV:] + intra[:, M_DV:]
            den = jnp.maximum(jnp.abs(qn), jnp.exp(-mt))
            hh = num / den
            hn = hh * lax.rsqrt(jnp.mean(hh * hh, axis=-1, keepdims=True) + EPS)
            hn = hn * gain_ref[:, h * M_DV:(h + 1) * M_DV]
            out = hn * _sigmoid(o_ref[nb, :, h * M_DV:(h + 1) * M_DV])
            out_ref[nb, :, h * M_DV:(h + 1) * M_DV] = out.astype(out_ref.dtype)
            b_last = b_rep[L - 1:L, :]
            m_new = mt[L - 1:L, :]
            g_prev = jnp.exp(b_last + m_prev - m_new)
            g_row = jnp.exp(b_last - b_row + i_row - m_new)
            kg_t = (kt_h * g_row).astype(BF16)
            s_scr[sh] = (jnp.concatenate([g_prev, g_prev], axis=1) * state
                         + jnp.dot(kg_t, v_ext, preferred_element_type=F32))
            m_scr[nb, h:h + 1, :] = m_new

    @pl.when(c == pl.num_programs(1) - 1)
    def _():
        for nb in range(MLSTM_SEQS):
            for h in range(M_HEADS):
                st = s_scr[nb * M_HEADS + h]
                c_ref[nb, h] = st[:, :M_DV]
                n_ref[nb, h] = st[:, M_DV:]
        m_ref[...] = m_scr[...]


def _mlstm_prompt(z, bias_row, gain_row, batch, seq):
    L = MLSTM_CHUNK
    S = MLSTM_SEQS
    z3 = z.reshape(batch, seq, PROJ_W)
    return pl.pallas_call(
        _mlstm_prompt_kernel,
        out_shape=(
            jax.ShapeDtypeStruct((batch, seq, M_HEADS * M_DV), BF16),
            jax.ShapeDtypeStruct((batch, M_HEADS, M_DK, M_DV), F32),
            jax.ShapeDtypeStruct((batch, M_HEADS, M_DK, M_DV), F32),
            jax.ShapeDtypeStruct((batch, 8, LANES), F32),
        ),
        grid=(batch // S, seq // L),
        in_specs=[
            pl.BlockSpec((S, L, 256), lambda b, c: (b, c, C_MQ // 256)),
            pl.BlockSpec((S, L, 256), lambda b, c: (b, c, C_MK // 256)),
            pl.BlockSpec((S, L, 512), lambda b, c: (b, c, C_MV // 512)),
            pl.BlockSpec((S, L, 512), lambda b, c: (b, c, C_MO // 512)),
            pl.BlockSpec((S, L, LANES), lambda b, c: (b, c, C_GATE // LANES)),
            pl.BlockSpec((1, LANES), lambda b, c: (0, 0)),
            pl.BlockSpec((1, M_HEADS * M_DV), lambda b, c: (0, 0)),
        ],
        out_specs=(
            pl.BlockSpec((S, L, M_HEADS * M_DV), lambda b, c: (b, c, 0)),
            pl.BlockSpec((S, M_HEADS, M_DK, M_DV), lambda b, c: (b, 0, 0, 0)),
            pl.BlockSpec((S, M_HEADS, M_DK, M_DV), lambda b, c: (b, 0, 0, 0)),
            pl.BlockSpec((S, 8, LANES), lambda b, c: (b, 0, 0)),
        ),
        scratch_shapes=[pltpu.VMEM((S * M_HEADS, M_DK, 2 * M_DV), F32), pltpu.VMEM((S, 8, LANES), F32)],
        compiler_params=_cparams(("arbitrary", "arbitrary")),
        name="mlstm_prompt",
    )(z3, z3, z3, z3, z3, bias_row, gain_row)


def _attn_prompt_kernel(relb_ref, sink_ref, q_ref, kp_ref, kc_ref, vp_ref, vc_ref, bucket_ref,
                        out_ref, bias_scr):
    B = ATT_BLOCK
    j = pl.program_id(1)

    @pl.when((pl.program_id(0) == 0) & (j == 0))
    def _():
        bucket = bucket_ref[...]
        for h in range(A_HEADS):
            acc = jnp.full((B, 2 * B), NEG, F32)
            for bk in range(NUM_BUCKETS):
                acc = jnp.where(bucket == bk, relb_ref[bk * A_HEADS + h], acc)
            bias_scr[h] = acc

    scale = A_HD ** -0.5
    s_iota = lax.broadcasted_iota(I32, (B, 2 * B), 1)
    first = jnp.where((s_iota < B) & (j == 0), NEG, 0.0)
    for sub in range(ATT_QBLOCKS):
        rows = slice(sub * B, (sub + 1) * B)
        prev_rows = slice((sub - 1) * B, sub * B)
        outs = []
        for h in range(A_HEADS):
            g = h // A_GROUP
            cols = slice(g * A_HD, (g + 1) * A_HD)
            q_h = (q_ref[rows, h * A_HD:(h + 1) * A_HD] * scale).astype(BF16)
            k_prev = kp_ref[:, cols] if sub == 0 else kc_ref[prev_rows, cols]
            v_prev = vp_ref[:, cols] if sub == 0 else vc_ref[prev_rows, cols]
            k2 = jnp.concatenate([k_prev, kc_ref[rows, cols]], axis=0).astype(BF16)
            v2 = jnp.concatenate([v_prev, vc_ref[rows, cols]], axis=0).astype(BF16)
            logits = lax.dot_general(q_h, k2, (((1,), (1,)), ((), ())), preferred_element_type=F32)
            logits = logits + bias_scr[h]
            if sub == 0:
                logits = logits + first
            sink = sink_ref[h]
            m = jnp.maximum(jnp.max(logits, axis=-1, keepdims=True), sink)
            p = jnp.exp(logits - m)
            den = jnp.sum(p, axis=-1, keepdims=True) + jnp.exp(sink - m)
            o = jnp.dot(p.astype(BF16), v2, preferred_element_type=F32) / den
            outs.append(o)
        out_ref[rows, :] = jnp.concatenate(outs, axis=1).astype(out_ref.dtype)


def _attn_prompt(z, rel_bias, sinks, batch, seq):
    B = ATT_BLOCK
    nb = seq // B
    qi = np.arange(B)[:, None]
    si = np.arange(2 * B)[None, :]
    dist = qi + B - si
    bucket = np.where((dist >= 0) & (dist <= WINDOW), _t5_bucket_np(dist), -1).astype(np.int32)
    Q = ATT_QBLOCKS
    ns = nb // Q
    cur = lambda b, j, *_: b * ns + j
    prev = lambda b, j, *_: b * nb + jnp.maximum(Q * j - 1, 0)
    grid_spec = pltpu.PrefetchScalarGridSpec(
        num_scalar_prefetch=2,
        grid=(batch, ns),
        in_specs=[
            pl.BlockSpec((Q * B, 512), lambda b, j, *_: (cur(b, j), C_AQ // 512)),
            pl.BlockSpec((B, LANES), lambda b, j, *_: (prev(b, j), C_AK // LANES)),
            pl.BlockSpec((Q * B, LANES), lambda b, j, *_: (cur(b, j), C_AK // LANES)),
            pl.BlockSpec((B, LANES), lambda b, j, *_: (prev(b, j), C_AV // LANES)),
            pl.BlockSpec((Q * B, LANES), lambda b, j, *_: (cur(b, j), C_AV // LANES)),
            pl.BlockSpec((B, 2 * B), lambda b, j, *_: (0, 0)),
        ],
        out_specs=pl.BlockSpec((Q * B, A_HEADS * A_HD), lambda b, j, *_: (cur(b, j), 0)),
        scratch_shapes=[pltpu.VMEM((A_HEADS, B, 2 * B), F32)],
    )
    return pl.pallas_call(
        _attn_prompt_kernel,
        out_shape=jax.ShapeDtypeStruct((batch * seq, A_HEADS * A_HD), BF16),
        grid_spec=grid_spec,
        compiler_params=_cparams(("arbitrary", "arbitrary")),
        name="attn_prompt",
    )(rel_bias.reshape(-1), sinks, z, z, z, z, z, jnp.asarray(bucket))


def _outproj_router_kernel(x_ref, mo_ref, ao_ref, wm_ref, wa_ref, g_ref, wr_ref, br_ref, cin_ref, *rest):
    y_ref, yn_ref, eidx_ref, gate_ref, rank_ref, cout_ref, carry = rest[-7:]
    T = x_ref.shape[0]
    i = pl.program_id(0)

    @pl.when(i == 0)
    def _():
        carry[...] = cin_ref[...]

    y = (x_ref[...] + jnp.dot(mo_ref[...], wm_ref[...], preferred_element_type=F32)
         + jnp.dot(ao_ref[...], wa_ref[...], preferred_element_type=F32))
    y_ref[...] = y
    yn = _rms(y, g_ref[...])
    for s in range(ROW_TILES):
        yn_ref[pl.ds(s, T, stride=ROW_TILES), :] = yn[:, s * LANES:(s + 1) * LANES]
    yh = yn.astype(BF16)
    yl = (yn - yh.astype(F32)).astype(BF16)
    hh = jnp.dot(yh, wr_ref[...], preferred_element_type=F32)
    lh = jnp.dot(yl, wr_ref[:, :LANES], preferred_element_type=F32)
    logits = hh[:, :LANES] + (hh[:, LANES:] + lh) + br_ref[...]
    lane = lax.broadcasted_iota(I32, (T, LANES), 1)
    lane_f = lane.astype(F32)
    vals, idxs, hots = [], [], []
    l = logits
    for _ in range(TOP_K):
        mx = jnp.max(l, axis=-1, keepdims=True)
        idx = jnp.min(jnp.where(l == mx, lane_f, float(LANES)), axis=-1, keepdims=True)
        hot = lane_f == idx
        l = jnp.where(hot, -jnp.inf, l)
        vals.append(mx)
        idxs.append(idx)
        hots.append(hot)
    es = [jnp.exp(v - vals[0]) for v in vals]
    tot = es[0] + es[1] + es[2] + es[3]
    sel = jnp.where(hots[0] | hots[1] | hots[2] | hots[3], 1.0, 0.0)
    row = lax.broadcasted_iota(I32, (T, T), 0)
    col = lax.broadcasted_iota(I32, (T, T), 1)
    strict = (col < row).astype(BF16)
    before = carry[...] + jnp.dot(strict, sel.astype(BF16), preferred_element_type=F32)
    per_row = LANES // TOP_K
    tok_in_row = lax.broadcasted_iota(I32, (T, LANES), 0) % per_row
    gate = jnp.zeros((T, LANES), F32)
    e_sp = jnp.zeros((T, LANES), F32)
    r_sp = jnp.zeros((T, LANES), F32)
    for k in range(TOP_K):
        r_k = jnp.sum(jnp.where(hots[k], before, 0.0), axis=-1, keepdims=True)
        gate = jnp.where(lane == k, es[k] / tot, gate)
        mine = lane == tok_in_row * TOP_K + k
        e_sp = jnp.where(mine, idxs[k], e_sp)
        r_sp = jnp.where(mine, r_k, r_sp)
    fold = (lax.broadcasted_iota(I32, (T // per_row, T), 1) // per_row
            == lax.broadcasted_iota(I32, (T // per_row, T), 0)).astype(F32)
    exact = dict(preferred_element_type=F32, precision=lax.Precision.HIGHEST)
    eidx_ref[...] = jnp.dot(fold, e_sp, **exact).astype(I32)
    rank_ref[...] = jnp.dot(fold, r_sp, **exact).astype(I32)
    gate_ref[...] = gate
    carry[...] = carry[...] + jnp.sum(sel, axis=0, keepdims=True)
    cout_ref[...] = carry[...]


def _outproj_router(x2, m_out, a_out, w_m, w_a, norm_row, w_r, b_r, counts_in, tile, yn_rows, yn_prev=None):
    n = x2.shape[0]
    tok = lambda i: (i, 0)
    fix = lambda i: (0, 0)
    assert yn_prev is None or (yn_rows - n) % tile == 0
    row0 = 0 if yn_prev is None else (yn_rows - n) // tile
    extra_specs = [] if yn_prev is None else [pl.BlockSpec(memory_space=pl.ANY)]
    extra_args = [] if yn_prev is None else [yn_prev]
    return pl.pallas_call(
        _outproj_router_kernel,
        out_shape=(
            jax.ShapeDtypeStruct((n, D_MODEL), F32),
            jax.ShapeDtypeStruct((yn_rows * ROW_TILES, LANES), F32),
            jax.ShapeDtypeStruct((n * TOP_K // LANES, LANES), I32),
            jax.ShapeDtypeStruct((n, LANES), F32),
            jax.ShapeDtypeStruct((n * TOP_K // LANES, LANES), I32),
            jax.ShapeDtypeStruct((1, LANES), F32),
        ),
        grid=(n // tile,),
        in_specs=[
            pl.BlockSpec((tile, D_MODEL), tok),
            pl.BlockSpec((tile, 512), tok),
            pl.BlockSpec((tile, 512), tok),
            pl.BlockSpec((512, D_MODEL), fix),
            pl.BlockSpec((512, D_MODEL), fix),
            pl.BlockSpec((1, D_MODEL), fix),
            pl.BlockSpec((D_MODEL, 2 * LANES), fix),
            pl.BlockSpec((1, LANES), fix),
            pl.BlockSpec((1, LANES), fix),
        ] + extra_specs,
        out_specs=(
            pl.BlockSpec((tile, D_MODEL), tok),
            pl.BlockSpec((tile * ROW_TILES, LANES), lambda i: (row0 + i, 0)),
            pl.BlockSpec((tile * TOP_K // LANES, LANES), tok),
            pl.BlockSpec((tile, LANES), tok),
            pl.BlockSpec((tile * TOP_K // LANES, LANES), tok),
            pl.BlockSpec((1, LANES), fix),
        ),
        scratch_shapes=[pltpu.VMEM((1, LANES), F32)],
        input_output_aliases={} if yn_prev is None else {9: 1},
        compiler_params=_cparams(("arbitrary",)),
        name="outproj_router",
    )(x2, m_out, a_out, w_m, w_a, norm_row, w_r, b_r, counts_in, *extra_args)


def _ffn_kernel(be_ref, nused_ref, valid_ref, nxt_ref, gp_ref, gs_ref, gn_ref, order_ref,
                yn_hbm, wg_hbm, bg_ref, wu_hbm, bu_ref, wd_hbm, bd_ref, out_ref,
                wbuf, wg_bf, wu_bf, wd_bf, h_scr, xbuf0, xbuf1, xbuf2, wsem, gsem):
    i = pl.program_id(0)
    slot = lax.rem(i, FFN_BUFS)
    R = FFN_ROWS
    w_hbm = (wg_hbm, wu_hbm, wd_hbm)
    w_bf = (wg_bf, wu_bf, wd_bf)
    xbuf = (xbuf0, xbuf1, xbuf2)

    def weight_copies(e):
        return [pltpu.make_async_copy(w_hbm[j].at[e], wbuf.at[j], wsem.at[j]) for j in range(3)]

    def gather_row(b, r, dst_slot):
        tok = order_ref[jnp.where(r < gn_ref[b], gp_ref[b], gs_ref[b]) + r]
        return pltpu.make_async_copy(yn_hbm.at[tok], xbuf[dst_slot].at[pl.ds(r * ROW_TILES, ROW_TILES)],
                                     gsem.at[dst_slot])

    def gather_rolled(b, dst_slot):
        def body(rb, carry):
            for u in range(DMA_UNROLL):
                gather_row(b, rb * DMA_UNROLL + u, dst_slot).start(priority=u % 2)
            return carry
        lax.fori_loop(0, R // DMA_UNROLL, body, 0)

    def wait_rows(s):
        pltpu.make_async_copy(xbuf[(s + 1) % FFN_BUFS], xbuf[s], gsem.at[s]).wait()

    def by_slot(fn):
        for s in range(FFN_BUFS):
            @pl.when(slot == s)
            def _():
                fn(s)

    @pl.when(i < nused_ref[0])
    def _():
        @pl.when((i == 0) | (be_ref[i] != be_ref[jnp.maximum(i - 1, 0)]))
        def _():
            @pl.when(i == 0)
            def _():
                for cp in weight_copies(be_ref[0]):
                    cp.start()

            for cp in weight_copies(be_ref[i]):
                cp.wait()

            for j in range(3):
                for r in range(0, D_MODEL, CAST_ROWS):
                    w_bf[j][r:r + CAST_ROWS, :] = wbuf[j, r:r + CAST_ROWS, :].astype(BF16)

            @pl.when(nxt_ref[i] >= 0)
            def _():
                for cp in weight_copies(nxt_ref[i]):
                    cp.start()

        @pl.when(i == 0)
        def _():
            for b in range(FFN_BUFS - 1):
                gather_rolled(b, b)

        by_slot(wait_rows)

        def ffn_pass(rows, slot):
            x = jnp.concatenate([xbuf[slot][pl.ds(s, rows, stride=ROW_TILES), :].astype(BF16)
                                 for s in range(ROW_TILES)], axis=1)
            ahead = i + (FFN_BUFS - 1)
            g_p, g_s, g_n = gp_ref[ahead], gs_ref[ahead], gn_ref[ahead]
            dst = (slot + FFN_BUFS - 1) % FFN_BUFS
            for r in range(R):
                tok = order_ref[jnp.where(r < g_n, g_p, g_s) + r]
                pltpu.make_async_copy(yn_hbm.at[tok], xbuf[dst].at[pl.ds(r * ROW_TILES, ROW_TILES)],
                                      gsem.at[dst]).start(priority=r % 2)
            for c in range(D_FF // FFN_COLS):
                cs = slice(c * FFN_COLS, (c + 1) * FFN_COLS)
                g = jnp.dot(x, wg_bf[:, cs], preferred_element_type=F32) + bg_ref[:, cs]
                u = jnp.dot(x, wu_bf[:, cs], preferred_element_type=F32) + bu_ref[:, cs]
                g = jnp.minimum(g, SWIGLU_LIMIT)
                u = jnp.clip(u, -SWIGLU_LIMIT, SWIGLU_LIMIT)
                h_scr[0:rows, cs] = ((u + 1.0) * (g * _sigmoid(SWIGLU_ALPHA * g))).astype(BF16)
            out = jnp.dot(h_scr[0:rows, :], wd_bf[...], preferred_element_type=F32) + bd_ref[...]
            for s in range(ROW_TILES):
                out_ref[pl.ds(s, rows, stride=ROW_TILES), :] = out[:, s * LANES:(s + 1) * LANES]

        for s in range(FFN_BUFS):
            for rows in range(FFN_SUB, R + 1, FFN_SUB):
                @pl.when((slot == s) & (valid_ref[i] > rows - FFN_SUB) & (valid_ref[i] <= rows))
                def _():
                    ffn_pass(rows, s)

        @pl.when(i == nused_ref[0] - 1)
        def _():
            def drain(s):
                for d in range(1, FFN_BUFS):
                    wait_rows((s + d) % FFN_BUFS)
            by_slot(drain)


def _ffn(plan, order, yn_tiles, wg, bg, wu, bu, wd, bd, n_blocks):
    blk = lambda i, be, nu, *_: (jnp.minimum(i, nu[0] - 1), 0)
    bsel = lambda i, be, *_: (be[i], 0, 0)
    hbm = pl.BlockSpec(memory_space=pl.ANY)
    grid_spec = pltpu.PrefetchScalarGridSpec(
        num_scalar_prefetch=8,
        grid=(n_blocks,),
        in_specs=[
            hbm,
            hbm,
            pl.BlockSpec((None, 1, D_FF), bsel),
            hbm,
            pl.BlockSpec((None, 1, D_FF), bsel),
            hbm,
            pl.BlockSpec((None, 1, D_MODEL), bsel),
        ],
        out_specs=pl.BlockSpec((FFN_ROWS * ROW_TILES, LANES), blk),
        scratch_shapes=[pltpu.VMEM((3, D_MODEL, D_FF), F32),
                        pltpu.VMEM((D_MODEL, D_FF), BF16), pltpu.VMEM((D_MODEL, D_FF), BF16),
                        pltpu.VMEM((D_FF, D_MODEL), BF16), pltpu.VMEM((FFN_ROWS, D_FF), BF16),
                        pltpu.VMEM((FFN_ROWS * ROW_TILES, LANES), F32),
                        pltpu.VMEM((FFN_ROWS * ROW_TILES, LANES), F32),
                        pltpu.VMEM((FFN_ROWS * ROW_TILES, LANES), F32),
                        pltpu.SemaphoreType.DMA((3,)), pltpu.SemaphoreType.DMA((FFN_BUFS,))],
    )
    return pl.pallas_call(
        _ffn_kernel,
        out_shape=jax.ShapeDtypeStruct((n_blocks * FFN_ROWS * ROW_TILES, LANES), F32),
        grid_spec=grid_spec,
        compiler_params=_cparams(("arbitrary",)),
        name="moe_ffn",
    )(*plan, order, yn_tiles, wg, bg, wu, bu, wd, bd)


def _combine_kernel(dest_ref, y_ref, gate_ref, fn_ref, ffn_ref, out_ref, buf0, buf1, sem):
    T = y_ref.shape[0]
    i = pl.program_id(0)
    n = pl.num_programs(0)
    slot = i % 2
    bufs = (buf0, buf1)

    def row_copy(base, t, k, s):
        d = dest_ref[base + t * TOP_K + k]
        return pltpu.make_async_copy(ffn_ref.at[d], bufs[s].at[k, pl.ds(t * ROW_TILES, ROW_TILES)], sem.at[s])

    @pl.when(i == 0)
    def _():
        def body(tb, carry):
            for u in range(DMA_UNROLL):
                for k in range(TOP_K):
                    row_copy(0, tb * DMA_UNROLL + u, k, 0).start(priority=k % 2)
            return carry
        lax.fori_loop(0, T // DMA_UNROLL, body, 0)

    def step(s, prefetch):
        for k in range(TOP_K):
            pltpu.make_async_copy(bufs[1 - s].at[k], bufs[s].at[k], sem.at[s]).wait()
        if prefetch:
            base = (i + 1) * (T * TOP_K)
            for t in range(T):
                for k in range(TOP_K):
                    row_copy(base, t, k, 1 - s).start(priority=k % 2)
        acc = y_ref[...]
        gate = gate_ref[...]
        for k in range(TOP_K):
            rows = jnp.concatenate([bufs[s][k, pl.ds(c, T, stride=ROW_TILES), :] for c in range(ROW_TILES)],
                                   axis=1)
            acc = acc + gate[:, k:k + 1] * rows
        out_ref[...] = _rms(acc, fn_ref[...])

    for s in range(2):
        @pl.when((slot == s) & (i + 1 < n))
        def _():
            step(s, True)

        @pl.when((slot == s) & (i + 1 >= n))
        def _():
            step(s, False)


def _combine(dest_flat, y, gate, fnorm_row, ffn_out, tile):
    n = y.shape[0]
    grid_spec = pltpu.PrefetchScalarGridSpec(
        num_scalar_prefetch=1,
        grid=(n // tile,),
        in_specs=[
            pl.BlockSpec((tile, D_MODEL), lambda i, *_: (i, 0)),
            pl.BlockSpec((tile, LANES), lambda i, *_: (i, 0)),
            pl.BlockSpec((1, D_MODEL), lambda i, *_: (0, 0)),
            pl.BlockSpec(memory_space=pl.ANY),
        ],
        out_specs=pl.BlockSpec((tile, D_MODEL), lambda i, *_: (i, 0)),
        scratch_shapes=[pltpu.VMEM((TOP_K, tile * ROW_TILES, LANES), F32),
                        pltpu.VMEM((TOP_K, tile * ROW_TILES, LANES), F32), pltpu.SemaphoreType.DMA((2,))],
    )
    return pl.pallas_call(
        _combine_kernel,
        out_shape=jax.ShapeDtypeStruct((n, D_MODEL), F32),
        grid_spec=grid_spec,
        compiler_params=_cparams(("arbitrary",)),
        name="moe_combine",
    )(dest_flat, y, gate, fnorm_row, ffn_out)


def _mlstm_step_kernel(q_ref, k_ref, v_ref, o_ref, gt_ref, bias_ref, gain_ref, c0_ref, n0_ref, m0_ref,
                       out_ref, c_ref, n_ref, m_ref):
    TB = SAMPLE_MLSTM_TB
    gb = gt_ref[...] + bias_ref[...]
    ls = _log_sigmoid(gb)
    lane = lax.broadcasted_iota(I32, (TB, LANES), 1)
    eye = (lax.broadcasted_iota(I32, (M_DK, M_DK), 0) == lax.broadcasted_iota(I32, (M_DK, M_DK), 1)).astype(F32)
    nt = (((1,), (1,)), ((), ()))
    m_all = jnp.zeros((TB, LANES), F32)
    for h in range(M_HEADS):
        i_pre = gb[:, h:h + 1]
        a = ls[:, M_HEADS + h:M_HEADS + h + 1] + m0_ref[:, h:h + 1]
        mt = jnp.maximum(a, i_pre)
        w_intra = jnp.exp(i_pre - mt)
        w_inter = jnp.exp(a - mt)
        q_h = q_ref[:, h * M_DK:(h + 1) * M_DK]
        k_h = k_ref[:, h * M_DK:(h + 1) * M_DK] * (M_DK ** -0.5)
        v_h = v_ref[:, h * M_DV:(h + 1) * M_DV]
        n0_h = n0_ref[:, h, :]
        s = jnp.sum(q_h * k_h, axis=-1, keepdims=True) * w_intra
        qn = w_inter * jnp.sum(q_h * n0_h, axis=-1, keepdims=True) + s
        den = jnp.maximum(jnp.abs(qn), jnp.exp(-mt))
        q_t = lax.dot_general(eye, q_h, nt, preferred_element_type=F32, precision=lax.Precision.HIGHEST)
        k_t = lax.dot_general(eye, k_h, nt, preferred_element_type=F32, precision=lax.Precision.HIGHEST)
        rows = []
        for b in range(TB):
            c0 = c0_ref[b, h]
            qc = jnp.sum(c0 * q_t[:, b:b + 1], axis=0, keepdims=True)
            v_b = v_h[b:b + 1, :]
            rows.append(w_inter[b:b + 1, :] * qc + s[b:b + 1, :] * v_b)
            c_ref[b, h] = w_inter[b:b + 1, :] * c0 + (w_intra[b:b + 1, :] * k_t[:, b:b + 1]) * v_b
        num = jnp.concatenate(rows, axis=0)
        hh = num / den
        hn = hh * lax.rsqrt(jnp.mean(hh * hh, axis=-1, keepdims=True) + EPS)
        hn = hn * gain_ref[:, h * M_DV:(h + 1) * M_DV]
        out_ref[:, h * M_DV:(h + 1) * M_DV] = (hn * _sigmoid(o_ref[:, h * M_DV:(h + 1) * M_DV])).astype(out_ref.dtype)
        n_ref[:, h * M_DK:(h + 1) * M_DK] = w_inter * n0_h + w_intra * k_h
        m_all = jnp.where(lane == h, mt, m_all)
    m_ref[...] = m_all


def _mlstm_step(zs, bias_row, gain_row, c0, n0, m0):
    TB = SAMPLE_MLSTM_TB
    nb = zs.shape[0]
    tok = lambda i: (i, 0)
    return pl.pallas_call(
        _mlstm_step_kernel,
        out_shape=(
            jax.ShapeDtypeStruct((nb, M_HEADS * M_DV), BF16),
            jax.ShapeDtypeStruct((nb, M_HEADS, M_DK, M_DV), F32),
            jax.ShapeDtypeStruct((nb, M_HEADS * M_DK), F32),
            jax.ShapeDtypeStruct((nb, LANES), F32),
        ),
        grid=(nb // TB,),
        in_specs=[
            pl.BlockSpec((TB, 256), lambda i: (i, C_MQ // 256)),
            pl.BlockSpec((TB, 256), lambda i: (i, C_MK // 256)),
            pl.BlockSpec((TB, 512), lambda i: (i, C_MV // 512)),
            pl.BlockSpec((TB, 512), lambda i: (i, C_MO // 512)),
            pl.BlockSpec((TB, LANES), lambda i: (i, C_GATE // LANES)),
            pl.BlockSpec((1, LANES), lambda i: (0, 0)),
            pl.BlockSpec((1, M_HEADS * M_DV), lambda i: (0, 0)),
            pl.BlockSpec((TB, M_HEADS, M_DK, M_DV), lambda i: (i, 0, 0, 0)),
            pl.BlockSpec((TB, M_HEADS, M_DK), lambda i: (i, 0, 0)),
            pl.BlockSpec((TB, M_HEADS), tok),
        ],
        out_specs=(
            pl.BlockSpec((TB, M_HEADS * M_DV), tok),
            pl.BlockSpec((TB, M_HEADS, M_DK, M_DV), lambda i: (i, 0, 0, 0)),
            pl.BlockSpec((TB, M_HEADS * M_DK), tok),
            pl.BlockSpec((TB, LANES), tok),
        ),
        compiler_params=_cparams(("arbitrary",)),
        name="mlstm_step",
    )(zs, zs, zs, zs, zs, bias_row, gain_row, c0, n0, m0)


def _attn_step_kernel(q_ref, kn_ref, vn_ref, ck_ref, cv_ref, bucket_ref, relt_ref, sink_ref,
                      out_ref, nk_ref, nv_ref, bias_scr):
    TB = SAMPLE_ATT_TB
    W = ck_ref.shape[1]

    @pl.when(pl.program_id(0) == 0)
    def _():
        bucket = jnp.broadcast_to(bucket_ref[...], (A_HEADS, W))
        acc = jnp.zeros((A_HEADS, W), F32)
        for bk in range(NUM_BUCKETS):
            acc = jnp.where(bucket == bk, relt_ref[:, bk:bk + 1], acc)
        bias_scr[...] = acc

    scale = A_HD ** -0.5
    nt = (((1,), (1,)), ((), ()))
    bias = bias_scr[...]
    bias_new = relt_ref[:, 0:1]
    sink = sink_ref[...]
    low = lax.broadcasted_iota(I32, (A_HEADS, 1), 0) < A_GROUP
    for b in range(TB):
        q = q_ref[b]
        qb = q.astype(BF16)
        kc = ck_ref[b]
        vc = cv_ref[b]
        kn = kn_ref[b:b + 1, :]
        vn = vn_ref[b:b + 1, :]
        l0 = lax.dot_general(qb, kc[:, :A_HD].astype(BF16), nt, preferred_element_type=F32)
        l1 = lax.dot_general(qb, kc[:, A_HD:].astype(BF16), nt, preferred_element_type=F32)
        logits = jnp.where(low, l0, l1) * scale + bias
        kn_h = jnp.where(low, kn[:, :A_HD], kn[:, A_HD:])
        vn_h = jnp.where(low, vn[:, :A_HD], vn[:, A_HD:])
        l_new = jnp.sum(q * kn_h, axis=-1, keepdims=True) * scale + bias_new
        m = jnp.maximum(jnp.maximum(jnp.max(logits, axis=-1, keepdims=True), l_new), sink)
        p = jnp.exp(logits - m)
        p_new = jnp.exp(l_new - m)
        den = jnp.sum(p, axis=-1, keepdims=True) + p_new + jnp.exp(sink - m)
        pb = p.astype(BF16)
        o0 = jnp.dot(pb, vc[:, :A_HD].astype(BF16), preferred_element_type=F32)
        o1 = jnp.dot(pb, vc[:, A_HD:].astype(BF16), preferred_element_type=F32)
        o = jnp.where(low, o0, o1) + p_new * vn_h
        out_ref[b] = o / den
        nk_ref[b, 0:W - 1, :] = ck_ref[b, 1:W, :]
        nk_ref[b, W - 1:W, :] = kn
        nv_ref[b, 0:W - 1, :] = cv_ref[b, 1:W, :]
        nv_ref[b, W - 1:W, :] = vn


def _attn_step(q3, k_new, v_new, ck, cv, rel_bias, sinks):
    TB = SAMPLE_ATT_TB
    nb, W = ck.shape[0], ck.shape[1]
    bucket = _t5_bucket_np(W - np.arange(W))[None, :].astype(np.int32)
    tok = lambda i: (i, 0)
    tok3 = lambda i: (i, 0, 0)
    fix = lambda i: (0, 0)
    return pl.pallas_call(
        _attn_step_kernel,
        out_shape=(
            jax.ShapeDtypeStruct((nb, A_HEADS, A_HD), F32),
            jax.ShapeDtypeStruct(ck.shape, F32),
            jax.ShapeDtypeStruct(cv.shape, F32),
        ),
        grid=(nb // TB,),
        in_specs=[
            pl.BlockSpec((TB, A_HEADS, A_HD), tok3),
            pl.BlockSpec((TB, LANES), lambda i: (i, C_AK // LANES)),
            pl.BlockSpec((TB, LANES), lambda i: (i, C_AV // LANES)),
            pl.BlockSpec((TB, W, A_KV * A_HD), tok3),
            pl.BlockSpec((TB, W, A_KV * A_HD), tok3),
            pl.BlockSpec((1, W), fix),
            pl.BlockSpec((A_HEADS, NUM_BUCKETS), fix),
            pl.BlockSpec((A_HEADS, 1), fix),
        ],
        out_specs=(
            pl.BlockSpec((TB, A_HEADS, A_HD), tok3),
            pl.BlockSpec((TB, W, A_KV * A_HD), tok3),
            pl.BlockSpec((TB, W, A_KV * A_HD), tok3),
        ),
        scratch_shapes=[pltpu.VMEM((A_HEADS, W), F32)],
        compiler_params=_cparams(("arbitrary",)),
        name="attn_step",
    )(q3, k_new, v_new, ck, cv, jnp.asarray(bucket), rel_bias.T, sinks.reshape(A_HEADS, 1))


def _reorder_w_in(w_in):
    o = 0
    parts = {}
    for name, width in (("mq", 256), ("mk", 256), ("mv", 512), ("mo", 512), ("mi", 4), ("mf", 4),
                        ("aq", 512), ("ak", 128), ("av", 128)):
        parts[name] = w_in[:, o:o + width]
        o += width
    pad = jnp.zeros((w_in.shape[0], LANES - 2 * M_HEADS), w_in.dtype)
    cols = [parts[n] for n in ("mq", "mk", "mv", "mo", "aq", "ak", "av", "mi", "mf")] + [pad]
    return jnp.concatenate(cols, axis=1).astype(BF16)


def _lane_row(v, fill=0.0):
    return jnp.concatenate([v.astype(F32), jnp.full((LANES - v.shape[0],), fill, F32)])[None, :]


def kernel(x_prompt, x_sample, state_C, state_n, state_m, cache_k, cache_v, rel_bias, norm1, w_in, b_if,
           m_gain, sinks, w_out, norm2, w_router, b_router, w_gate, b_gate, w_up, b_up, w_down, b_down,
           final_norm):
    assert norm1.shape[0] == 1, "single-layer trunk"
    batch, seq, _ = x_prompt.shape
    nsmp = x_sample.shape[0]
    n_p = batch * seq
    W = cache_k.shape[2]

    xp = x_prompt.reshape(n_p, D_MODEL)
    xs_ = x_sample.reshape(nsmp, D_MODEL)
    w_in_r = _reorder_w_in(w_in[0])
    n1 = norm1[0][None, :]
    n2 = norm2[0][None, :]
    fn = final_norm[None, :]
    bias_row = _lane_row(b_if[0])
    gain_row = m_gain[0][None, :]
    w_m = w_out[0][:M_HEADS * M_DV].astype(BF16)
    w_a = w_out[0][M_HEADS * M_DV:].astype(BF16)
    w_r32 = jnp.concatenate([w_router[0], jnp.zeros((D_MODEL, LANES - N_EXPERTS), F32)], axis=1)
    w_r_hi = w_r32.astype(BF16)
    w_r = jnp.concatenate([w_r_hi, (w_r32 - w_r_hi.astype(F32)).astype(BF16)], axis=1)
    b_r = _lane_row(b_router[0], NEG)

    zp = _inproj(xp, n1, w_in_r, 2 * TOK_TILE)
    zs = _inproj(xs_, n1, w_in_r, nsmp)
    m_out_p, p_c, p_nrep, p_mrep = _mlstm_prompt(zp, bias_row, gain_row, batch, seq)
    m_out_p = m_out_p.reshape(n_p, M_HEADS * M_DV)
    a_out_p = _attn_prompt(zp, rel_bias, sinks[0], batch, seq)
    m_out_s, s_c, s_n, s_mrep = _mlstm_step(zs, bias_row, gain_row, state_C[0], state_n[0], state_m[0])
    q3 = zs[:, C_AQ:C_AQ + A_HEADS * A_HD].reshape(nsmp, A_HEADS, A_HD)
    a3, s_k, s_v = _attn_step(q3, zs, zs, cache_k[0].reshape(nsmp, W, A_KV * A_HD),
                              cache_v[0].reshape(nsmp, W, A_KV * A_HD), rel_bias, sinks[0])
    a_out_s = a3.reshape(nsmp, A_HEADS * A_HD).astype(BF16)

    n_tot = n_p + nsmp
    zero_counts = jnp.zeros((1, LANES), F32)
    y_p, yn_all, e_p, g_p, r_p, cnt_p = _outproj_router(xp, m_out_p, a_out_p, w_m, w_a, n2, w_r, b_r,
                                                        zero_counts, TOK_TILE, n_tot)
    y_s, yn_all, e_s, g_s, r_s, cnt = _outproj_router(xs_, m_out_s, a_out_s, w_m, w_a, n2, w_r, b_r,
                                                      cnt_p, nsmp, n_tot, yn_all)

    counts = cnt[0, :N_EXPERTS].astype(I32)
    counts_p = cnt_p[0, :N_EXPERTS].astype(I32)
    counts_s = counts - counts_p
    start_p = jnp.cumsum(counts_p) - counts_p
    start_s = jnp.cumsum(counts_s) - counts_s
    nblk = (counts + FFN_ROWS - 1) // FFN_ROWS
    blk_end = jnp.cumsum(nblk)
    ex = jnp.arange(N_EXPERTS, dtype=I32)

    def per_slot(e, r, table):
        hot = e[:, :, None] == ex
        return jnp.sum(jnp.where(hot, table, 0), axis=-1) + r

    order = jnp.concatenate([
        jnp.argsort(per_slot(e_p, r_p, start_p).reshape(-1)).astype(I32) // TOP_K,
        jnp.argsort(per_slot(e_s, r_s, start_s - counts_p).reshape(-1)).astype(I32) // TOP_K + n_p,
        jnp.zeros((2 * FFN_ROWS,), I32)])
    pad_start = (blk_end - nblk) * FFN_ROWS
    dest_p = per_slot(e_p, r_p, pad_start).reshape(-1).astype(I32)
    dest_s = per_slot(e_s, r_s, pad_start).reshape(-1).astype(I32)

    n_blocks = (n_tot * TOP_K + N_EXPERTS * (FFN_ROWS - 1) + FFN_ROWS - 1) // FFN_ROWS
    nused = jnp.maximum(blk_end[-1], 1).astype(I32)
    bi = jnp.minimum(jnp.arange(n_blocks + FFN_BUFS - 1, dtype=I32), nused - 1)
    block_e = jnp.minimum(jnp.sum((bi[:, None] >= blk_end[None, :]).astype(I32), axis=1), N_EXPERTS - 1)
    hot = block_e[:, None] == ex
    pick = lambda v: jnp.sum(jnp.where(hot, v, 0), axis=1)
    row0 = (bi - pick(blk_end - nblk)) * FFN_ROWS
    blk_valid = jnp.clip(pick(counts) - row0, 0, FFN_ROWS).astype(I32)
    real = jnp.arange(n_blocks + FFN_BUFS - 1, dtype=I32) < nused
    blk_gn = jnp.where(real, jnp.clip(pick(counts_p) - row0, 0, FFN_ROWS), 0).astype(I32)
    blk_gp = jnp.where(real, pick(start_p) + row0, 0).astype(I32)
    blk_gs = jnp.where(real, n_p * TOP_K + pick(start_s - counts_p) + row0, n_tot * TOP_K).astype(I32)
    later = jnp.where((ex[None, :] > ex[:, None]) & (counts[None, :] > 0), ex[None, :], N_EXPERTS)
    next_e = jnp.min(later, axis=1)
    blk_next = pick(jnp.where(next_e < N_EXPERTS, next_e, -1)).astype(I32)
    plan = (block_e[:n_blocks].astype(I32), nused.reshape(1), blk_valid[:n_blocks], blk_next[:n_blocks],
            blk_gp, blk_gs, blk_gn)

    as_tiles = lambda a: a.reshape(-1, ROW_TILES, LANES)
    ffn_out = _ffn(plan, order, as_tiles(yn_all), w_gate[0], b_gate[0][:, None, :], w_up[0], b_up[0][:, None, :],
                   w_down[0], b_down[0][:, None, :], n_blocks)
    out_p = _combine(dest_p, y_p, g_p, fn, as_tiles(ffn_out), COMBINE_TILE)
    out_s = _combine(dest_s, y_s, g_s, fn, as_tiles(ffn_out), min(COMBINE_TILE, nsmp))

    kv_shape = (1, batch, WINDOW, A_KV, A_HD)
    zp3 = zp.reshape(batch, seq, PROJ_W)
    zk = zp3[:, seq - WINDOW:, C_AK:C_AK + A_KV * A_HD]
    zv = zp3[:, seq - WINDOW:, C_AV:C_AV + A_KV * A_HD]
    return (
        out_p.reshape(batch, seq, D_MODEL),
        out_s.reshape(nsmp, 1, D_MODEL),
        p_c[None],
        p_nrep[None, :, :, :, 0],
        p_mrep[None, :, :M_HEADS, 0],
        zk.reshape(kv_shape),
        zv.reshape(kv_shape),
        s_c[None],
        s_n.reshape(1, nsmp, M_HEADS, M_DK),
        s_mrep[None, :, :M_HEADS],
        s_k.reshape(1, nsmp, W, A_KV, A_HD),
        s_v.reshape(1, nsmp, W, A_KV, A_HD),
    )
```

```python
import math

import numpy as np
import jax
import jax.numpy as jnp
from jax import lax
from jax.experimental import pallas as pl
from jax.experimental.pallas import tpu as pltpu

F32 = jnp.float32
BF16 = jnp.bfloat16
I32 = jnp.int32

D_MODEL = 1024
M_HEADS = 4
M_DK = 64
M_DV = 128
A_HEADS = 8
A_KV = 2
A_GROUP = A_HEADS // A_KV
A_HD = 64
WINDOW = 128
NUM_BUCKETS = 32
MAX_DISTANCE = 128
N_EXPERTS = 32
TOP_K = 4
D_FF = 1024
SWIGLU_LIMIT = 7.0
SWIGLU_ALPHA = 1.702
EPS = 1e-5

LANES = 128
NEG = -1e30
VMEM_LIMIT = 52 * 1024 * 1024

C_MQ, C_MK, C_MV, C_MO, C_AQ, C_AK, C_AV, C_GATE = 0, 256, 512, 1024, 1536, 2048, 2176, 2304
PROJ_W = 2432

MLSTM_CHUNK = LANES
MLSTM_SEQS = 8
ATT_BLOCK = 128
ATT_QBLOCKS = 2
TOK_TILE = 512
FFN_ROWS = 512
FFN_SUB = 256
FFN_BUFS = 3
ROW_TILES = D_MODEL // LANES
FFN_COLS = 256
CAST_ROWS = 128
COMBINE_TILE = 256
DMA_UNROLL = 8
SAMPLE_MLSTM_TB = 16
SAMPLE_ATT_TB = 8


def _t5_bucket_np(dist):
    n = np.maximum(dist, 0)
    max_exact = NUM_BUCKETS // 2
    ratio = np.log(np.maximum(n, 1).astype(np.float32) / np.float32(max_exact)) / np.float32(
        math.log(MAX_DISTANCE / max_exact))
    large = max_exact + (ratio * np.float32(NUM_BUCKETS - max_exact)).astype(np.int32)
    large = np.minimum(large, NUM_BUCKETS - 1)
    return np.where(n < max_exact, n, large).astype(np.int32)


def _cparams(sem):
    return pltpu.CompilerParams(dimension_semantics=sem, vmem_limit_bytes=VMEM_LIMIT)


def _rms(x, g):
    return x * lax.rsqrt(jnp.mean(x * x, axis=-1, keepdims=True) + EPS) * g


def _log_sigmoid(x):
    return jnp.minimum(x, 0.0) - jnp.log(1.0 + jnp.exp(-jnp.abs(x)))


def _sigmoid(x):
    return 1.0 / (1.0 + jnp.exp(-x))


def _inproj_kernel(x_ref, g_ref, w_ref, z_ref):
    xn = _rms(x_ref[...], g_ref[...]).astype(BF16)
    z_ref[...] = jnp.dot(xn, w_ref[...], preferred_element_type=F32)


def _inproj(x2, norm_row, w_bf16, tile):
    n = x2.shape[0]
    return pl.pallas_call(
        _inproj_kernel,
        out_shape=jax.ShapeDtypeStruct((n, PROJ_W), F32),
        grid=(n // tile,),
        in_specs=[
            pl.BlockSpec((tile, D_MODEL), lambda i: (i, 0)),
            pl.BlockSpec((1, D_MODEL), lambda i: (0, 0)),
            pl.BlockSpec((D_MODEL, PROJ_W), lambda i: (0, 0)),
        ],
        out_specs=pl.BlockSpec((tile, PROJ_W), lambda i: (i, 0)),
        compiler_params=_cparams(("arbitrary",)),
        name="inproj",
    )(x2, norm_row, w_bf16)


def _mlstm_prompt_kernel(q_ref, k_ref, v_ref, o_ref, gt_ref, bias_ref, gain_ref,
                         out_ref, c_ref, n_ref, m_ref, s_scr, m_scr):
    L = MLSTM_CHUNK
    c = pl.program_id(1)

    @pl.when(c == 0)
    def _():
        s_scr[...] = jnp.zeros_like(s_scr)
        m_scr[...] = jnp.zeros_like(m_scr)

    row = lax.broadcasted_iota(I32, (L, L), 0)
    col = lax.broadcasted_iota(I32, (L, L), 1)
    causal = col <= row
    tril = causal.astype(F32)
    ones = jnp.ones((L, M_DV), BF16)

    for nb in range(MLSTM_SEQS):
        gb = gt_ref[nb] + bias_ref[...]
        ls = _log_sigmoid(gb)
        bcum = jnp.dot(tril, ls, preferred_element_type=F32, precision=lax.Precision.HIGHEST)
        gb_t = gb.T
        bcum_t = bcum.T
        k_t = (k_ref[nb] * (M_DK ** -0.5)).T
        for h in range(M_HEADS):
            sh = nb * M_HEADS + h
            b_rep = jnp.broadcast_to(bcum[:, M_HEADS + h:M_HEADS + h + 1], (L, LANES))
            b_row = bcum_t[M_HEADS + h:M_HEADS + h + 1, :]
            i_row = gb_t[h:h + 1, :]
            m_prev = m_scr[nb, h:h + 1, :]
            dmat = jnp.where(causal, b_rep + (i_row - b_row), NEG)
            a_rep = b_rep + m_prev
            mt = jnp.maximum(a_rep, jnp.broadcast_to(jnp.max(dmat, axis=1, keepdims=True), (L, LANES)))
            w_intra = jnp.exp(dmat - mt)
            w_inter = jnp.exp(a_rep - mt)
            q_h = q_ref[nb, :, h * M_DK:(h + 1) * M_DK].astype(BF16)
            kt_h = k_t[h * M_DK:(h + 1) * M_DK, :]
            qk = jnp.dot(q_h, kt_h.astype(BF16), preferred_element_type=F32)
            s_w = (qk * w_intra).astype(BF16)
            v_ext = jnp.concatenate([v_ref[nb, :, h * M_DV:(h + 1) * M_DV].astype(BF16), ones], axis=1)
            state = s_scr[sh]
            inter = jnp.dot(q_h, state.astype(BF16), preferred_element_type=F32)
            intra = jnp.dot(s_w, v_ext, preferred_element_type=F32)
            num = w_inter * inter[:, :M_DV] + intra[:, :M_DV]
            qn = w_inter * inter[:, M_DV:] + intra[:, M_DV:]
            den = jnp.maximum(jnp.abs(qn), jnp.exp(-mt))
            hh = num / den
            hn = hh * lax.rsqrt(jnp.mean(hh * hh, axis=-1, keepdims=True) + EPS)
            hn = hn * gain_ref[:, h * M_DV:(h + 1) * M_DV]
            out = hn * _sigmoid(o_ref[nb, :, h * M_DV:(h + 1) * M_DV])
            out_ref[nb, :, h * M_DV:(h + 1) * M_DV] = out.astype(out_ref.dtype)
            b_last = b_rep[L - 1:L, :]
            m_new = mt[L - 1:L, :]
            g_prev = jnp.exp(b_last + m_prev - m_new)
            g_row = jnp.exp(b_last - b_row + i_row - m_new)
            kg_t = (kt_h * g_row).astype(BF16)
            s_scr[sh] = (jnp.concatenate([g_prev, g_prev], axis=1) * state
                         + jnp.dot(kg_t, v_ext, preferred_element_type=F32))
            m_scr[nb, h:h + 1, :] = m_new

    @pl.when(c == pl.num_programs(1) - 1)
    def _():
        for nb in range(MLSTM_SEQS):
            for h in range(M_HEADS):
                st = s_scr[nb * M_HEADS + h]
                c_ref[nb, h] = st[:, :M_DV]
                n_ref[nb, h] = st[:, M_DV:]
        m_ref[...] = m_scr[...]


def _mlstm_prompt(z, bias_row, gain_row, batch, seq):
    L = MLSTM_CHUNK
    S = MLSTM_SEQS
    z3 = z.reshape(batch, seq, PROJ_W)
    return pl.pallas_call(
        _mlstm_prompt_kernel,
        out_shape=(
            jax.ShapeDtypeStruct((batch, seq, M_HEADS * M_DV), BF16),
            jax.ShapeDtypeStruct((batch, M_HEADS, M_DK, M_DV), F32),
            jax.ShapeDtypeStruct((batch, M_HEADS, M_DK, M_DV), F32),
            jax.ShapeDtypeStruct((batch, 8, LANES), F32),
        ),
        grid=(batch // S, seq // L),
        in_specs=[
            pl.BlockSpec((S, L, 256), lambda b, c: (b, c, C_MQ // 256)),
            pl.BlockSpec((S, L, 256), lambda b, c: (b, c, C_MK // 256)),
            pl.BlockSpec((S, L, 512), lambda b, c: (b, c, C_MV // 512)),
            pl.BlockSpec((S, L, 512), lambda b, c: (b, c, C_MO // 512)),
            pl.BlockSpec((S, L, LANES), lambda b, c: (b, c, C_GATE // LANES)),
            pl.BlockSpec((1, LANES), lambda b, c: (0, 0)),
            pl.BlockSpec((1, M_HEADS * M_DV), lambda b, c: (0, 0)),
        ],
        out_specs=(
            pl.BlockSpec((S, L, M_HEADS * M_DV), lambda b, c: (b, c, 0)),
            pl.BlockSpec((S, M_HEADS, M_DK, M_DV), lambda b, c: (b, 0, 0, 0)),
            pl.BlockSpec((S, M_HEADS, M_DK, M_DV), lambda b, c: (b, 0, 0, 0)),
            pl.BlockSpec((S, 8, LANES), lambda b, c: (b, 0, 0)),
        ),
        scratch_shapes=[pltpu.VMEM((S * M_HEADS, M_DK, 2 * M_DV), F32), pltpu.VMEM((S, 8, LANES), F32)],
        compiler_params=_cparams(("arbitrary", "arbitrary")),
        name="mlstm_prompt",
    )(z3, z3, z3, z3, z3, bias_row, gain_row)


def _attn_prompt_kernel(relb_ref, sink_ref, q_ref, kp_ref, kc_ref, vp_ref, vc_ref, bucket_ref,
                        out_ref, bias_scr):
    B = ATT_BLOCK
    j = pl.program_id(1)

    @pl.when((pl.program_id(0) == 0) & (j == 0))
    def _():
        bucket = bucket_ref[...]
        for h in range(A_HEADS):
            acc = jnp.full((B, 2 * B), NEG, F32)
            for bk in range(NUM_BUCKETS):
                acc = jnp.where(bucket == bk, relb_ref[bk * A_HEADS + h], acc)
            bias_scr[h] = acc

    scale = A_HD ** -0.5
    s_iota = lax.broadcasted_iota(I32, (B, 2 * B), 1)
    first = jnp.where((s_iota < B) & (j == 0), NEG, 0.0)
    for sub in range(ATT_QBLOCKS):
        rows = slice(sub * B, (sub + 1) * B)
        prev_rows = slice((sub - 1) * B, sub * B)
        outs = []
        for h in range(A_HEADS):
            g = h // A_GROUP
            cols = slice(g * A_HD, (g + 1) * A_HD)
            q_h = (q_ref[rows, h * A_HD:(h + 1) * A_HD] * scale).astype(BF16)
            k_prev = kp_ref[:, cols] if sub == 0 else kc_ref[prev_rows, cols]
            v_prev = vp_ref[:, cols] if sub == 0 else vc_ref[prev_rows, cols]
            k2 = jnp.concatenate([k_prev, kc_ref[rows, cols]], axis=0).astype(BF16)
            v2 = jnp.concatenate([v_prev, vc_ref[rows, cols]], axis=0).astype(BF16)
            logits = lax.dot_general(q_h, k2, (((1,), (1,)), ((), ())), preferred_element_type=F32)
            logits = logits + bias_scr[h]
            if sub == 0:
                logits = logits + first
            sink = sink_ref[h]
            m = jnp.maximum(jnp.max(logits, axis=-1, keepdims=True), sink)
            p = jnp.exp(logits - m)
            den = jnp.sum(p, axis=-1, keepdims=True) + jnp.exp(sink - m)
            o = jnp.dot(p.astype(BF16), v2, preferred_element_type=F32) / den
            outs.append(o)
        out_ref[rows, :] = jnp.concatenate(outs, axis=1).astype(out_ref.dtype)


def _attn_prompt(z, rel_bias, sinks, batch, seq):
    B = ATT_BLOCK
    nb = seq // B
    qi = np.arange(B)[:, None]
    si = np.arange(2 * B)[None, :]
    dist = qi + B - si
    bucket = np.where((dist >= 0) & (dist <= WINDOW), _t5_bucket_np(dist), -1).astype(np.int32)
    Q = ATT_QBLOCKS
    ns = nb // Q
    cur = lambda b, j, *_: b * ns + j
    prev = lambda b, j, *_: b * nb + jnp.maximum(Q * j - 1, 0)
    grid_spec = pltpu.PrefetchScalarGridSpec(
        num_scalar_prefetch=2,
        grid=(batch, ns),
        in_specs=[
            pl.BlockSpec((Q * B, 512), lambda b, j, *_: (cur(b, j), C_AQ // 512)),
            pl.BlockSpec((B, LANES), lambda b, j, *_: (prev(b, j), C_AK // LANES)),
            pl.BlockSpec((Q * B, LANES), lambda b, j, *_: (cur(b, j), C_AK // LANES)),
            pl.BlockSpec((B, LANES), lambda b, j, *_: (prev(b, j), C_AV // LANES)),
            pl.BlockSpec((Q * B, LANES), lambda b, j, *_: (cur(b, j), C_AV // LANES)),
            pl.BlockSpec((B, 2 * B), lambda b, j, *_: (0, 0)),
        ],
        out_specs=pl.BlockSpec((Q * B, A_HEADS * A_HD), lambda b, j, *_: (cur(b, j), 0)),
        scratch_shapes=[pltpu.VMEM((A_HEADS, B, 2 * B), F32)],
    )
    return pl.pallas_call(
        _attn_prompt_kernel,
        out_shape=jax.ShapeDtypeStruct((batch * seq, A_HEADS * A_HD), BF16),
        grid_spec=grid_spec,
        compiler_params=_cparams(("arbitrary", "arbitrary")),
        name="attn_prompt",
    )(rel_bias.reshape(-1), sinks, z, z, z, z, z, jnp.asarray(bucket))


def _outproj_router_kernel(x_ref, mo_ref, ao_ref, wm_ref, wa_ref, g_ref, wr_ref, br_ref, cin_ref, *rest):
    y_ref, yn_ref, eidx_ref, gate_ref, rank_ref, cout_ref, carry = rest[-7:]
    T = x_ref.shape[0]
    i = pl.program_id(0)

    @pl.when(i == 0)
    def _():
        carry[...] = cin_ref[...]

    y = (x_ref[...] + jnp.dot(mo_ref[...], wm_ref[...], preferred_element_type=F32)
         + jnp.dot(ao_ref[...], wa_ref[...], preferred_element_type=F32))
    y_ref[...] = y
    yn = _rms(y, g_ref[...])
    for s in range(ROW_TILES):
        yn_ref[pl.ds(s, T, stride=ROW_TILES), :] = yn[:, s * LANES:(s + 1) * LANES]
    yh = yn.astype(BF16)
    yl = (yn - yh.astype(F32)).astype(BF16)
    hh = jnp.dot(yh, wr_ref[...], preferred_element_type=F32)
    lh = jnp.dot(yl, wr_ref[:, :LANES], preferred_element_type=F32)
    logits = hh[:, :LANES] + (hh[:, LANES:] + lh) + br_ref[...]
    lane = lax.broadcasted_iota(I32, (T, LANES), 1)
    lane_f = lane.astype(F32)
    vals, idxs, hots = [], [], []
    l = logits
    for _ in range(TOP_K):
        mx = jnp.max(l, axis=-1, keepdims=True)
        idx = jnp.min(jnp.where(l == mx, lane_f, float(LANES)), axis=-1, keepdims=True)
        hot = lane_f == idx
        l = jnp.where(hot, -jnp.inf, l)
        vals.append(mx)
        idxs.append(idx)
        hots.append(hot)
    es = [jnp.exp(v - vals[0]) for v in vals]
    tot = es[0] + es[1] + es[2] + es[3]
    sel = jnp.where(hots[0] | hots[1] | hots[2] | hots[3], 1.0, 0.0)
    row = lax.broadcasted_iota(I32, (T, T), 0)
    col = lax.broadcasted_iota(I32, (T, T), 1)
    strict = (col < row).astype(BF16)
    before = carry[...] + jnp.dot(strict, sel.astype(BF16), preferred_element_type=F32)
    per_row = LANES // TOP_K
    tok_in_row = lax.broadcasted_iota(I32, (T, LANES), 0) % per_row
    gate = jnp.zeros((T, LANES), F32)
    e_sp = jnp.zeros((T, LANES), F32)
    r_sp = jnp.zeros((T, LANES), F32)
    for k in range(TOP_K):
        r_k = jnp.sum(jnp.where(hots[k], before, 0.0), axis=-1, keepdims=True)
        gate = jnp.where(lane == k, es[k] / tot, gate)
        mine = lane == tok_in_row * TOP_K + k
        e_sp = jnp.where(mine, idxs[k], e_sp)
        r_sp = jnp.where(mine, r_k, r_sp)
    fold = (lax.broadcasted_iota(I32, (T // per_row, T), 1) // per_row
            == lax.broadcasted_iota(I32, (T // per_row, T), 0)).astype(F32)
    exact = dict(preferred_element_type=F32, precision=lax.Precision.HIGHEST)
    eidx_ref[...] = jnp.dot(fold, e_sp, **exact).astype(I32)
    rank_ref[...] = jnp.dot(fold, r_sp, **exact).astype(I32)
    gate_ref[...] = gate
    carry[...] = carry[...] + jnp.sum(sel, axis=0, keepdims=True)
    cout_ref[...] = carry[...]


def _outproj_router(x2, m_out, a_out, w_m, w_a, norm_row, w_r, b_r, counts_in, tile, yn_rows, yn_prev=None):
    n = x2.shape[0]
    tok = lambda i: (i, 0)
    fix = lambda i: (0, 0)
    assert yn_prev is None or (yn_rows - n) % tile == 0
    row0 = 0 if yn_prev is None else (yn_rows - n) // tile
    extra_specs = [] if yn_prev is None else [pl.BlockSpec(memory_space=pl.ANY)]
    extra_args = [] if yn_prev is None else [yn_prev]
    return pl.pallas_call(
        _outproj_router_kernel,
        out_shape=(
            jax.ShapeDtypeStruct((n, D_MODEL), F32),
            jax.ShapeDtypeStruct((yn_rows * ROW_TILES, LANES), F32),
            jax.ShapeDtypeStruct((n * TOP_K // LANES, LANES), I32),
            jax.ShapeDtypeStruct((n, LANES), F32),
            jax.ShapeDtypeStruct((n * TOP_K // LANES, LANES), I32),
            jax.ShapeDtypeStruct((1, LANES), F32),
        ),
        grid=(n // tile,),
        in_specs=[
            pl.BlockSpec((tile, D_MODEL), tok),
            pl.BlockSpec((tile, 512), tok),
            pl.BlockSpec((tile, 512), tok),
            pl.BlockSpec((512, D_MODEL), fix),
            pl.BlockSpec((512, D_MODEL), fix),
            pl.BlockSpec((1, D_MODEL), fix),
            pl.BlockSpec((D_MODEL, 2 * LANES), fix),
            pl.BlockSpec((1, LANES), fix),
            pl.BlockSpec((1, LANES), fix),
        ] + extra_specs,
        out_specs=(
            pl.BlockSpec((tile, D_MODEL), tok),
            pl.BlockSpec((tile * ROW_TILES, LANES), lambda i: (row0 + i, 0)),
            pl.BlockSpec((tile * TOP_K // LANES, LANES), tok),
            pl.BlockSpec((tile, LANES), tok),
            pl.BlockSpec((tile * TOP_K // LANES, LANES), tok),
            pl.BlockSpec((1, LANES), fix),
        ),
        scratch_shapes=[pltpu.VMEM((1, LANES), F32)],
        input_output_aliases={} if yn_prev is None else {9: 1},
        compiler_params=_cparams(("arbitrary",)),
        name="outproj_router",
    )(x2, m_out, a_out, w_m, w_a, norm_row, w_r, b_r, counts_in, *extra_args)


def _ffn_kernel(be_ref, nused_ref, valid_ref, nxt_ref, gp_ref, gs_ref, gn_ref, order_ref,
                yn_hbm, wg_hbm, bg_ref, wu_hbm, bu_ref, wd_hbm, bd_ref, out_ref,
                wbuf, wg_bf, wu_bf, wd_bf, h_scr, xbuf0, xbuf1, xbuf2, wsem, gsem):
    i = pl.program_id(0)
    slot = lax.rem(i, FFN_BUFS)
    R = FFN_ROWS
    w_hbm = (wg_hbm, wu_hbm, wd_hbm)
    w_bf = (wg_bf, wu_bf, wd_bf)
    xbuf = (xbuf0, xbuf1, xbuf2)

    def weight_copies(e):
        return [pltpu.make_async_copy(w_hbm[j].at[e], wbuf.at[j], wsem.at[j]) for j in range(3)]

    def gather_row(b, r, dst_slot):
        tok = order_ref[jnp.where(r < gn_ref[b], gp_ref[b], gs_ref[b]) + r]
        return pltpu.make_async_copy(yn_hbm.at[tok], xbuf[dst_slot].at[pl.ds(r * ROW_TILES, ROW_TILES)],
                                     gsem.at[dst_slot])

    def gather_rolled(b, dst_slot):
        def body(rb, carry):
            for u in range(DMA_UNROLL):
                gather_row(b, rb * DMA_UNROLL + u, dst_slot).start(priority=u % 2)
            return carry
        lax.fori_loop(0, R // DMA_UNROLL, body, 0)

    def wait_rows(s):
        pltpu.make_async_copy(xbuf[(s + 1) % FFN_BUFS], xbuf[s], gsem.at[s]).wait()

    def by_slot(fn):
        for s in range(FFN_BUFS):
            @pl.when(slot == s)
            def _():
                fn(s)

    @pl.when(i < nused_ref[0])
    def _():
        @pl.when((i == 0) | (be_ref[i] != be_ref[jnp.maximum(i - 1, 0)]))
        def _():
            @pl.when(i == 0)
            def _():
                for cp in weight_copies(be_ref[0]):
                    cp.start()

            for cp in weight_copies(be_ref[i]):
                cp.wait()

            for j in range(3):
                for r in range(0, D_MODEL, CAST_ROWS):
                    w_bf[j][r:r + CAST_ROWS, :] = wbuf[j, r:r + CAST_ROWS, :].astype(BF16)

            @pl.when(nxt_ref[i] >= 0)
            def _():
                for cp in weight_copies(nxt_ref[i]):
                    cp.start()

        @pl.when(i == 0)
        def _():
            for b in range(FFN_BUFS - 1):
                gather_rolled(b, b)

        by_slot(wait_rows)

        def ffn_pass(rows, slot):
            x = jnp.concatenate([xbuf[slot][pl.ds(s, rows, stride=ROW_TILES), :].astype(BF16)
                                 for s in range(ROW_TILES)], axis=1)
            ahead = i + (FFN_BUFS - 1)
            g_p, g_s, g_n = gp_ref[ahead], gs_ref[ahead], gn_ref[ahead]
            dst = (slot + FFN_BUFS - 1) % FFN_BUFS
            for r in range(R):
                tok = order_ref[jnp.where(r < g_n, g_p, g_s) + r]
                pltpu.make_async_copy(yn_hbm.at[tok], xbuf[dst].at[pl.ds(r * ROW_TILES, ROW_TILES)],
                                      gsem.at[dst]).start(priority=r % 2)
            for c in range(D_FF // FFN_COLS):
                cs = slice(c * FFN_COLS, (c + 1) * FFN_COLS)
                g = jnp.dot(x, wg_bf[:, cs], preferred_element_type=F32) + bg_ref[:, cs]
                u = jnp.dot(x, wu_bf[:, cs], preferred_element_type=F32) + bu_ref[:, cs]
                g = jnp.minimum(g, SWIGLU_LIMIT)
                u = jnp.clip(u, -SWIGLU_LIMIT, SWIGLU_LIMIT)
                h_scr[0:rows, cs] = ((u + 1.0) * (g * _sigmoid(SWIGLU_ALPHA * g))).astype(BF16)
            out = jnp.dot(h_scr[0:rows, :], wd_bf[...], preferred_element_type=F32) + bd_ref[...]
            for s in range(ROW_TILES):
                out_ref[pl.ds(s, rows, stride=ROW_TILES), :] = out[:, s * LANES:(s + 1) * LANES]

        for s in range(FFN_BUFS):
            for rows in range(FFN_SUB, R + 1, FFN_SUB):
                @pl.when((slot == s) & (valid_ref[i] > rows - FFN_SUB) & (valid_ref[i] <= rows))
                def _():
                    ffn_pass(rows, s)

        @pl.when(i == nused_ref[0] - 1)
        def _():
            def drain(s):
                for d in range(1, FFN_BUFS):
                    wait_rows((s + d) % FFN_BUFS)
            by_slot(drain)


def _ffn(plan, order, yn_tiles, wg, bg, wu, bu, wd, bd, n_blocks):
    blk = lambda i, be, nu, *_: (jnp.minimum(i, nu[0] - 1), 0)
    bsel = lambda i, be, *_: (be[i], 0, 0)
    hbm = pl.BlockSpec(memory_space=pl.ANY)
    grid_spec = pltpu.PrefetchScalarGridSpec(
        num_scalar_prefetch=8,
        grid=(n_blocks,),
        in_specs=[
            hbm,
            hbm,
            pl.BlockSpec((None, 1, D_FF), bsel),
            hbm,
            pl.BlockSpec((None, 1, D_FF), bsel),
            hbm,
            pl.BlockSpec((None, 1, D_MODEL), bsel),
        ],
        out_specs=pl.BlockSpec((FFN_ROWS * ROW_TILES, LANES), blk),
        scratch_shapes=[pltpu.VMEM((3, D_MODEL, D_FF), F32),
                        pltpu.VMEM((D_MODEL, D_FF), BF16), pltpu.VMEM((D_MODEL, D_FF), BF16),
                        pltpu.VMEM((D_FF, D_MODEL), BF16), pltpu.VMEM((FFN_ROWS, D_FF), BF16),
                        pltpu.VMEM((FFN_ROWS * ROW_TILES, LANES), F32),
                        pltpu.VMEM((FFN_ROWS * ROW_TILES, LANES), F32),
                        pltpu.VMEM((FFN_ROWS * ROW_TILES, LANES), F32),
                        pltpu.SemaphoreType.DMA((3,)), pltpu.SemaphoreType.DMA((FFN_BUFS,))],
    )
    return pl.pallas_call(
        _ffn_kernel,
        out_shape=jax.ShapeDtypeStruct((n_blocks * FFN_ROWS * ROW_TILES, LANES), F32),
        grid_spec=grid_spec,
        compiler_params=_cparams(("arbitrary",)),
        name="moe_ffn",
    )(*plan, order, yn_tiles, wg, bg, wu, bu, wd, bd)


def _combine_kernel(dest_ref, y_ref, gate_ref, fn_ref, ffn_ref, out_ref, buf0, buf1, sem):
    T = y_ref.shape[0]
    i = pl.program_id(0)
    n = pl.num_programs(0)
    slot = i % 2
    bufs = (buf0, buf1)

    def row_copy(base, t, k, s):
        d = dest_ref[base + t * TOP_K + k]
        return pltpu.make_async_copy(ffn_ref.at[d], bufs[s].at[k, pl.ds(t * ROW_TILES, ROW_TILES)], sem.at[s])

    @pl.when(i == 0)
    def _():
        def body(tb, carry):
            for u in range(DMA_UNROLL):
                for k in range(TOP_K):
                    row_copy(0, tb * DMA_UNROLL + u, k, 0).start(priority=k % 2)
            return carry
        lax.fori_loop(0, T // DMA_UNROLL, body, 0)

    def step(s, prefetch):
        for k in range(TOP_K):
            pltpu.make_async_copy(bufs[1 - s].at[k], bufs[s].at[k], sem.at[s]).wait()
        if prefetch:
            base = (i + 1) * (T * TOP_K)
            for t in range(T):
                for k in range(TOP_K):
                    row_copy(base, t, k, 1 - s).start(priority=k % 2)
        acc = y_ref[...]
        gate = gate_ref[...]
        for k in range(TOP_K):
            rows = jnp.concatenate([bufs[s][k, pl.ds(c, T, stride=ROW_TILES), :] for c in range(ROW_TILES)],
                                   axis=1)
            acc = acc + gate[:, k:k + 1] * rows
        out_ref[...] = _rms(acc, fn_ref[...])

    for s in range(2):
        @pl.when((slot == s) & (i + 1 < n))
        def _():
            step(s, True)

        @pl.when((slot == s) & (i + 1 >= n))
        def _():
            step(s, False)


def _combine(dest_flat, y, gate, fnorm_row, ffn_out, tile):
    n = y.shape[0]
    grid_spec = pltpu.PrefetchScalarGridSpec(
        num_scalar_prefetch=1,
        grid=(n // tile,),
        in_specs=[
            pl.BlockSpec((tile, D_MODEL), lambda i, *_: (i, 0)),
            pl.BlockSpec((tile, LANES), lambda i, *_: (i, 0)),
            pl.BlockSpec((1, D_MODEL), lambda i, *_: (0, 0)),
            pl.BlockSpec(memory_space=pl.ANY),
        ],
        out_specs=pl.BlockSpec((tile, D_MODEL), lambda i, *_: (i, 0)),
        scratch_shapes=[pltpu.VMEM((TOP_K, tile * ROW_TILES, LANES), F32),
                        pltpu.VMEM((TOP_K, tile * ROW_TILES, LANES), F32), pltpu.SemaphoreType.DMA((2,))],
    )
    return pl.pallas_call(
        _combine_kernel,
        out_shape=jax.ShapeDtypeStruct((n, D_MODEL), F32),
        grid_spec=grid_spec,
        compiler_params=_cparams(("arbitrary",)),
        name="moe_combine",
    )(dest_flat, y, gate, fnorm_row, ffn_out)


def _mlstm_step_kernel(q_ref, k_ref, v_ref, o_ref, gt_ref, bias_ref, gain_ref, c0_ref, n0_ref, m0_ref,
                       out_ref, c_ref, n_ref, m_ref):
    TB = SAMPLE_MLSTM_TB
    gb = gt_ref[...] + bias_ref[...]
    ls = _log_sigmoid(gb)
    lane = lax.broadcasted_iota(I32, (TB, LANES), 1)
    eye = (lax.broadcasted_iota(I32, (M_DK, M_DK), 0) == lax.broadcasted_iota(I32, (M_DK, M_DK), 1)).astype(F32)
    nt = (((1,), (1,)), ((), ()))
    m_all = jnp.zeros((TB, LANES), F32)
    for h in range(M_HEADS):
        i_pre = gb[:, h:h + 1]
        a = ls[:, M_HEADS + h:M_HEADS + h + 1] + m0_ref[:, h:h + 1]
        mt = jnp.maximum(a, i_pre)
        w_intra = jnp.exp(i_pre - mt)
        w_inter = jnp.exp(a - mt)
        q_h = q_ref[:, h * M_DK:(h + 1) * M_DK]
        k_h = k_ref[:, h * M_DK:(h + 1) * M_DK] * (M_DK ** -0.5)
        v_h = v_ref[:, h * M_DV:(h + 1) * M_DV]
        n0_h = n0_ref[:, h, :]
        s = jnp.sum(q_h * k_h, axis=-1, keepdims=True) * w_intra
        qn = w_inter * jnp.sum(q_h * n0_h, axis=-1, keepdims=True) + s
        den = jnp.maximum(jnp.abs(qn), jnp.exp(-mt))
        q_t = lax.dot_general(eye, q_h, nt, preferred_element_type=F32, precision=lax.Precision.HIGHEST)
        k_t = lax.dot_general(eye, k_h, nt, preferred_element_type=F32, precision=lax.Precision.HIGHEST)
        rows = []
        for b in range(TB):
            c0 = c0_ref[b, h]
            qc = jnp.sum(c0 * q_t[:, b:b + 1], axis=0, keepdims=True)
            v_b = v_h[b:b + 1, :]
            rows.append(w_inter[b:b + 1, :] * qc + s[b:b + 1, :] * v_b)
            c_ref[b, h] = w_inter[b:b + 1, :] * c0 + (w_intra[b:b + 1, :] * k_t[:, b:b + 1]) * v_b
        num = jnp.concatenate(rows, axis=0)
        hh = num / den
        hn = hh * lax.rsqrt(jnp.mean(hh * hh, axis=-1, keepdims=True) + EPS)
        hn = hn * gain_ref[:, h * M_DV:(h + 1) * M_DV]
        out_ref[:, h * M_DV:(h + 1) * M_DV] = (hn * _sigmoid(o_ref[:, h * M_DV:(h + 1) * M_DV])).astype(out_ref.dtype)
        n_ref[:, h * M_DK:(h + 1) * M_DK] = w_inter * n0_h + w_intra * k_h
        m_all = jnp.where(lane == h, mt, m_all)
    m_ref[...] = m_all


def _mlstm_step(zs, bias_row, gain_row, c0, n0, m0):
    TB = SAMPLE_MLSTM_TB
    nb = zs.shape[0]
    tok = lambda i: (i, 0)
    return pl.pallas_call(
        _mlstm_step_kernel,
        out_shape=(
            jax.ShapeDtypeStruct((nb, M_HEADS * M_DV), BF16),
            jax.ShapeDtypeStruct((nb, M_HEADS, M_DK, M_DV), F32),
            jax.ShapeDtypeStruct((nb, M_HEADS * M_DK), F32),
            jax.ShapeDtypeStruct((nb, LANES), F32),
        ),
        grid=(nb // TB,),
        in_specs=[
            pl.BlockSpec((TB, 256), lambda i: (i, C_MQ // 256)),
            pl.BlockSpec((TB, 256), lambda i: (i, C_MK // 256)),
            pl.BlockSpec((TB, 512), lambda i: (i, C_MV // 512)),
            pl.BlockSpec((TB, 512), lambda i: (i, C_MO // 512)),
            pl.BlockSpec((TB, LANES), lambda i: (i, C_GATE // LANES)),
            pl.BlockSpec((1, LANES), lambda i: (0, 0)),
            pl.BlockSpec((1, M_HEADS * M_DV), lambda i: (0, 0)),
            pl.BlockSpec((TB, M_HEADS, M_DK, M_DV), lambda i: (i, 0, 0, 0)),
            pl.BlockSpec((TB, M_HEADS, M_DK), lambda i: (i, 0, 0)),
            pl.BlockSpec((TB, M_HEADS), tok),
        ],
        out_specs=(
            pl.BlockSpec((TB, M_HEADS * M_DV), tok),
            pl.BlockSpec((TB, M_HEADS, M_DK, M_DV), lambda i: (i, 0, 0, 0)),
            pl.BlockSpec((TB, M_HEADS * M_DK), tok),
            pl.BlockSpec((TB, LANES), tok),
        ),
        compiler_params=_cparams(("arbitrary",)),
        name="mlstm_step",
    )(zs, zs, zs, zs, zs, bias_row, gain_row, c0, n0, m0)


def _attn_step_kernel(q_ref, kn_ref, vn_ref, ck_ref, cv_ref, bucket_ref, relt_ref, sink_ref,
                      out_ref, nk_ref, nv_ref, bias_scr):
    TB = SAMPLE_ATT_TB
    W = ck_ref.shape[1]
    NR = TB * A_HEADS
    row = lax.broadcasted_iota(I32, (NR, 1), 0)
    low = row % A_HEADS < A_GROUP

    @pl.when(pl.program_id(0) == 0)
    def _():
        bucket = jnp.broadcast_to(bucket_ref[...], (A_HEADS, W))
        acc = jnp.zeros((A_HEADS, W), F32)
        for bk in range(NUM_BUCKETS):
            acc = jnp.where(bucket == bk, relt_ref[:, bk:bk + 1], acc)
        tiled = jnp.concatenate([jnp.concatenate([acc] * TB, axis=1)] * TB, axis=0)
        own = (lax.broadcasted_iota(I32, (NR, TB * W), 0) // A_HEADS
               == lax.broadcasted_iota(I32, (NR, TB * W), 1) // W)
        bias_scr[...] = jnp.where(own, tiled, NEG)

    scale = A_HD ** -0.5
    nt = (((1,), (1,)), ((), ()))
    exact = dict(preferred_element_type=F32, precision=lax.Precision.HIGHEST)
    bias_new = jnp.concatenate([relt_ref[:, 0:1]] * TB, axis=0)
    sink = jnp.concatenate([sink_ref[...]] * TB, axis=0)
    rep = (lax.broadcasted_iota(I32, (NR, TB), 0) // A_HEADS == lax.broadcasted_iota(I32, (NR, TB), 1)).astype(F32)
    kn_rows = jnp.dot(rep, kn_ref[...], **exact)
    vn_rows = jnp.dot(rep, vn_ref[...], **exact)
    kn_h = jnp.where(low, kn_rows[:, :A_HD], kn_rows[:, A_HD:])
    vn_h = jnp.where(low, vn_rows[:, :A_HD], vn_rows[:, A_HD:])

    q = q_ref[...]
    qb = q.astype(BF16)
    kc = ck_ref[...].reshape(TB * W, A_KV * A_HD).astype(BF16)
    vc = cv_ref[...].reshape(TB * W, A_KV * A_HD).astype(BF16)
    l0 = lax.dot_general(qb, kc[:, :A_HD], nt, preferred_element_type=F32)
    l1 = lax.dot_general(qb, kc[:, A_HD:], nt, preferred_element_type=F32)
    logits = jnp.where(low, l0, l1) * scale + bias_scr[...]
    l_new = jnp.sum(q * kn_h, axis=-1, keepdims=True) * scale + bias_new
    m = jnp.maximum(jnp.maximum(jnp.max(logits, axis=-1, keepdims=True), l_new), sink)
    p = jnp.exp(logits - m)
    p_new = jnp.exp(l_new - m)
    den = jnp.sum(p, axis=-1, keepdims=True) + p_new + jnp.exp(sink - m)
    pb = p.astype(BF16)
    o0 = jnp.dot(pb, vc[:, :A_HD], preferred_element_type=F32)
    o1 = jnp.dot(pb, vc[:, A_HD:], preferred_element_type=F32)
    out_ref[...] = (jnp.where(low, o0, o1) + p_new * vn_h) / den

    nk_ref[:, 0:W - 1, :] = ck_ref[:, 1:W, :]
    nv_ref[:, 0:W - 1, :] = cv_ref[:, 1:W, :]
    for b in range(TB):
        nk_ref[b, W - 1:W, :] = kn_ref[b:b + 1, :]
        nv_ref[b, W - 1:W, :] = vn_ref[b:b + 1, :]


def _attn_step(q3, k_new, v_new, ck, cv, rel_bias, sinks):
    TB = SAMPLE_ATT_TB
    nb, W = ck.shape[0], ck.shape[1]
    bucket = _t5_bucket_np(W - np.arange(W))[None, :].astype(np.int32)
    tok = lambda i: (i, 0)
    tok3 = lambda i: (i, 0, 0)
    fix = lambda i: (0, 0)
    return pl.pallas_call(
        _attn_step_kernel,
        out_shape=(
            jax.ShapeDtypeStruct((nb * A_HEADS, A_HD), F32),
            jax.ShapeDtypeStruct(ck.shape, F32),
            jax.ShapeDtypeStruct(cv.shape, F32),
        ),
        grid=(nb // TB,),
        in_specs=[
            pl.BlockSpec((TB * A_HEADS, A_HD), tok),
            pl.BlockSpec((TB, LANES), lambda i: (i, C_AK // LANES)),
            pl.BlockSpec((TB, LANES), lambda i: (i, C_AV // LANES)),
            pl.BlockSpec((TB, W, A_KV * A_HD), tok3),
            pl.BlockSpec((TB, W, A_KV * A_HD), tok3),
            pl.BlockSpec((1, W), fix),
            pl.BlockSpec((A_HEADS, NUM_BUCKETS), fix),
            pl.BlockSpec((A_HEADS, 1), fix),
        ],
        out_specs=(
            pl.BlockSpec((TB * A_HEADS, A_HD), tok),
            pl.BlockSpec((TB, W, A_KV * A_HD), tok3),
            pl.BlockSpec((TB, W, A_KV * A_HD), tok3),
        ),
        scratch_shapes=[pltpu.VMEM((TB * A_HEADS, TB * W), F32)],
        compiler_params=_cparams(("arbitrary",)),
        name="attn_step",
    )(q3.reshape(nb * A_HEADS, A_HD), k_new, v_new, ck, cv, jnp.asarray(bucket), rel_bias.T,
      sinks.reshape(A_HEADS, 1))


def _reorder_w_in(w_in):
    o = 0
    parts = {}
    for name, width in (("mq", 256), ("mk", 256), ("mv", 512), ("mo", 512), ("mi", 4), ("mf", 4),
                        ("aq", 512), ("ak", 128), ("av", 128)):
        parts[name] = w_in[:, o:o + width]
        o += width
    pad = jnp.zeros((w_in.shape[0], LANES - 2 * M_HEADS), w_in.dtype)
    cols = [parts[n] for n in ("mq", "mk", "mv", "mo", "aq", "ak", "av", "mi", "mf")] + [pad]
    return jnp.concatenate(cols, axis=1).astype(BF16)


def _lane_row(v, fill=0.0):
    return jnp.concatenate([v.astype(F32), jnp.full((LANES - v.shape[0],), fill, F32)])[None, :]


def kernel(x_prompt, x_sample, state_C, state_n, state_m, cache_k, cache_v, rel_bias, norm1, w_in, b_if,
           m_gain, sinks, w_out, norm2, w_router, b_router, w_gate, b_gate, w_up, b_up, w_down, b_down,
           final_norm):
    assert norm1.shape[0] == 1, "single-layer trunk"
    batch, seq, _ = x_prompt.shape
    nsmp = x_sample.shape[0]
    n_p = batch * seq
    W = cache_k.shape[2]

    xp = x_prompt.reshape(n_p, D_MODEL)
    xs_ = x_sample.reshape(nsmp, D_MODEL)
    w_in_r = _reorder_w_in(w_in[0])
    n1 = norm1[0][None, :]
    n2 = norm2[0][None, :]
    fn = final_norm[None, :]
    bias_row = _lane_row(b_if[0])
    gain_row = m_gain[0][None, :]
    w_m = w_out[0][:M_HEADS * M_DV].astype(BF16)
    w_a = w_out[0][M_HEADS * M_DV:].astype(BF16)
    w_r32 = jnp.concatenate([w_router[0], jnp.zeros((D_MODEL, LANES - N_EXPERTS), F32)], axis=1)
    w_r_hi = w_r32.astype(BF16)
    w_r = jnp.concatenate([w_r_hi, (w_r32 - w_r_hi.astype(F32)).astype(BF16)], axis=1)
    b_r = _lane_row(b_router[0], NEG)

    zp = _inproj(xp, n1, w_in_r, 2 * TOK_TILE)
    zs = _inproj(xs_, n1, w_in_r, nsmp)
    m_out_p, p_c, p_nrep, p_mrep = _mlstm_prompt(zp, bias_row, gain_row, batch, seq)
    m_out_p = m_out_p.reshape(n_p, M_HEADS * M_DV)
    a_out_p = _attn_prompt(zp, rel_bias, sinks[0], batch, seq)
    m_out_s, s_c, s_n, s_mrep = _mlstm_step(zs, bias_row, gain_row, state_C[0], state_n[0], state_m[0])
    q3 = zs[:, C_AQ:C_AQ + A_HEADS * A_HD].reshape(nsmp, A_HEADS, A_HD)
    a3, s_k, s_v = _attn_step(q3, zs, zs, cache_k[0].reshape(nsmp, W, A_KV * A_HD),
                              cache_v[0].reshape(nsmp, W, A_KV * A_HD), rel_bias, sinks[0])
    a_out_s = a3.reshape(nsmp, A_HEADS * A_HD).astype(BF16)

    n_tot = n_p + nsmp
    zero_counts = jnp.zeros((1, LANES), F32)
    y_p, yn_all, e_p, g_p, r_p, cnt_p = _outproj_router(xp, m_out_p, a_out_p, w_m, w_a, n2, w_r, b_r,
                                                        zero_counts, TOK_TILE, n_tot)
    y_s, yn_all, e_s, g_s, r_s, cnt = _outproj_router(xs_, m_out_s, a_out_s, w_m, w_a, n2, w_r, b_r,
                                                      cnt_p, nsmp, n_tot, yn_all)

    counts = cnt[0, :N_EXPERTS].astype(I32)
    counts_p = cnt_p[0, :N_EXPERTS].astype(I32)
    counts_s = counts - counts_p
    start_p = jnp.cumsum(counts_p) - counts_p
    start_s = jnp.cumsum(counts_s) - counts_s
    nblk = (counts + FFN_ROWS - 1) // FFN_ROWS
    blk_end = jnp.cumsum(nblk)
    ex = jnp.arange(N_EXPERTS, dtype=I32)

    def per_slot(e, r, table):
        hot = e[:, :, None] == ex
        return jnp.sum(jnp.where(hot, table, 0), axis=-1) + r

    order = jnp.concatenate([
        jnp.argsort(per_slot(e_p, r_p, start_p).reshape(-1)).astype(I32) // TOP_K,
        jnp.argsort(per_slot(e_s, r_s, start_s - counts_p).reshape(-1)).astype(I32) // TOP_K + n_p,
        jnp.zeros((2 * FFN_ROWS,), I32)])
    pad_start = (blk_end - nblk) * FFN_ROWS
    dest_p = per_slot(e_p, r_p, pad_start).reshape(-1).astype(I32)
    dest_s = per_slot(e_s, r_s, pad_start).reshape(-1).astype(I32)

    n_blocks = (n_tot * TOP_K + N_EXPERTS * (FFN_ROWS - 1) + FFN_ROWS - 1) // FFN_ROWS
    nused = jnp.maximum(blk_end[-1], 1).astype(I32)
    bi = jnp.minimum(jnp.arange(n_blocks + FFN_BUFS - 1, dtype=I32), nused - 1)
    block_e = jnp.minimum(jnp.sum((bi[:, None] >= blk_end[None, :]).astype(I32), axis=1), N_EXPERTS - 1)
    hot = block_e[:, None] == ex
    pick = lambda v: jnp.sum(jnp.where(hot, v, 0), axis=1)
    row0 = (bi - pick(blk_end - nblk)) * FFN_ROWS
    blk_valid = jnp.clip(pick(counts) - row0, 0, FFN_ROWS).astype(I32)
    real = jnp.arange(n_blocks + FFN_BUFS - 1, dtype=I32) < nused
    blk_gn = jnp.where(real, jnp.clip(pick(counts_p) - row0, 0, FFN_ROWS), 0).astype(I32)
    blk_gp = jnp.where(real, pick(start_p) + row0, 0).astype(I32)
    blk_gs = jnp.where(real, n_p * TOP_K + pick(start_s - counts_p) + row0, n_tot * TOP_K).astype(I32)
    later = jnp.where((ex[None, :] > ex[:, None]) & (counts[None, :] > 0), ex[None, :], N_EXPERTS)
    next_e = jnp.min(later, axis=1)
    blk_next = pick(jnp.where(next_e < N_EXPERTS, next_e, -1)).astype(I32)
    plan = (block_e[:n_blocks].astype(I32), nused.reshape(1), blk_valid[:n_blocks], blk_next[:n_blocks],
            blk_gp, blk_gs, blk_gn)

    as_tiles = lambda a: a.reshape(-1, ROW_TILES, LANES)
    ffn_out = _ffn(plan, order, as_tiles(yn_all), w_gate[0], b_gate[0][:, None, :], w_up[0], b_up[0][:, None, :],
                   w_down[0], b_down[0][:, None, :], n_blocks)
    out_p = _combine(dest_p, y_p, g_p, fn, as_tiles(ffn_out), COMBINE_TILE)
    out_s = _combine(dest_s, y_s, g_s, fn, as_tiles(ffn_out), min(COMBINE_TILE, nsmp))

    kv_shape = (1, batch, WINDOW, A_KV, A_HD)
    zp3 = zp.reshape(batch, seq, PROJ_W)
    zk = zp3[:, seq - WINDOW:, C_AK:C_AK + A_KV * A_HD]
    zv = zp3[:, seq - WINDOW:, C_AV:C_AV + A_KV * A_HD]
    return (
        out_p.reshape(batch, seq, D_MODEL),
        out_s.reshape(nsmp, 1, D_MODEL),
        p_c[None],
        p_nrep[None, :, :, :, 0],
        p_mrep[None, :, :M_HEADS, 0],
        zk.reshape(kv_shape),
        zv.reshape(kv_shape),
        s_c[None],
        s_n.reshape(1, nsmp, M_HEADS, M_DK),
        s_mrep[None, :, :M_HEADS],
        s_k.reshape(1, nsmp, W, A_KV, A_HD),
        s_v.reshape(1, nsmp, W, A_KV, A_HD),
    )
```

```python
import math

import numpy as np
import jax
import jax.numpy as jnp
from jax import lax
from jax.experimental import pallas as pl
from jax.experimental.pallas import tpu as pltpu

F32 = jnp.float32
BF16 = jnp.bfloat16
I32 = jnp.int32

D_MODEL = 1024
M_HEADS = 4
M_DK = 64
M_DV = 128
A_HEADS = 8
A_KV = 2
A_GROUP = A_HEADS // A_KV
A_HD = 64
WINDOW = 128
NUM_BUCKETS = 32
MAX_DISTANCE = 128
N_EXPERTS = 32
TOP_K = 4
D_FF = 1024
SWIGLU_LIMIT = 7.0
SWIGLU_ALPHA = 1.702
EPS = 1e-5

LANES = 128
NEG = -1e30
VMEM_LIMIT = 52 * 1024 * 1024

C_MQ, C_MK, C_MV, C_MO, C_AQ, C_AK, C_AV, C_GATE = 0, 256, 512, 1024, 1536, 2048, 2176, 2304
PROJ_W = 2432

MLSTM_CHUNK = LANES
MLSTM_SEQS = 8
ATT_BLOCK = 128
ATT_QBLOCKS = 2
TOK_TILE = 512
FFN_ROWS = 512
FFN_SUB = 256
FFN_BUFS = 3
ROW_TILES = D_MODEL // LANES
FFN_COLS = 256
CAST_ROWS = 128
COMBINE_TILE = 256
DMA_UNROLL = 8
SAMPLE_MLSTM_TB = 16
SAMPLE_ATT_TB = 8


def _t5_bucket_np(dist):
    n = np.maximum(dist, 0)
    max_exact = NUM_BUCKETS // 2
    ratio = np.log(np.maximum(n, 1).astype(np.float32) / np.float32(max_exact)) / np.float32(
        math.log(MAX_DISTANCE / max_exact))
    large = max_exact + (ratio * np.float32(NUM_BUCKETS - max_exact)).astype(np.int32)
    large = np.minimum(large, NUM_BUCKETS - 1)
    return np.where(n < max_exact, n, large).astype(np.int32)


def _cparams(sem):
    return pltpu.CompilerParams(dimension_semantics=sem, vmem_limit_bytes=VMEM_LIMIT)


def _rms(x, g):
    return x * lax.rsqrt(jnp.mean(x * x, axis=-1, keepdims=True) + EPS) * g


def _log_sigmoid(x):
    return jnp.minimum(x, 0.0) - jnp.log(1.0 + jnp.exp(-jnp.abs(x)))


def _sigmoid(x):
    return 1.0 / (1.0 + jnp.exp(-x))


def _inproj_kernel(x_ref, g_ref, w_ref, z_ref):
    xn = _rms(x_ref[...], g_ref[...]).astype(BF16)
    z_ref[...] = jnp.dot(xn, w_ref[...], preferred_element_type=F32)


def _inproj(x2, norm_row, w_bf16, tile):
    n = x2.shape[0]
    return pl.pallas_call(
        _inproj_kernel,
        out_shape=jax.ShapeDtypeStruct((n, PROJ_W), F32),
        grid=(n // tile,),
        in_specs=[
            pl.BlockSpec((tile, D_MODEL), lambda i: (i, 0)),
            pl.BlockSpec((1, D_MODEL), lambda i: (0, 0)),
            pl.BlockSpec((D_MODEL, PROJ_W), lambda i: (0, 0)),
        ],
        out_specs=pl.BlockSpec((tile, PROJ_W), lambda i: (i, 0)),
        compiler_params=_cparams(("arbitrary",)),
        name="inproj",
    )(x2, norm_row, w_bf16)


def _mlstm_prompt_kernel(q_ref, k_ref, v_ref, o_ref, gt_ref, bias_ref, gain_ref,
                         out_ref, c_ref, n_ref, m_ref, s_scr, m_scr):
    L = MLSTM_CHUNK
    c = pl.program_id(1)

    @pl.when(c == 0)
    def _():
        s_scr[...] = jnp.zeros_like(s_scr)
        m_scr[...] = jnp.zeros_like(m_scr)

    row = lax.broadcasted_iota(I32, (L, L), 0)
    col = lax.broadcasted_iota(I32, (L, L), 1)
    causal = col <= row
    tril = causal.astype(F32)
    ones = jnp.ones((L, M_DV), BF16)

    for nb in range(MLSTM_SEQS):
        gb = gt_ref[nb] + bias_ref[...]
        ls = _log_sigmoid(gb)
        bcum = jnp.dot(tril, ls, preferred_element_type=F32, precision=lax.Precision.HIGHEST)
        gb_t = gb.T
        bcum_t = bcum.T
        k_t = (k_ref[nb] * (M_DK ** -0.5)).T
        for h in range(M_HEADS):
            sh = nb * M_HEADS + h
            b_rep = jnp.broadcast_to(bcum[:, M_HEADS + h:M_HEADS + h + 1], (L, LANES))
            b_row = bcum_t[M_HEADS + h:M_HEADS + h + 1, :]
            i_row = gb_t[h:h + 1, :]
            m_prev = m_scr[nb, h:h + 1, :]
            dmat = jnp.where(causal, b_rep + (i_row - b_row), NEG)
            a_rep = b_rep + m_prev
            mt = jnp.maximum(a_rep, jnp.broadcast_to(jnp.max(dmat, axis=1, keepdims=True), (L, LANES)))
            w_intra = jnp.exp(dmat - mt)
            w_inter = jnp.exp(a_rep - mt)
            q_h = q_ref[nb, :, h * M_DK:(h + 1) * M_DK].astype(BF16)
            kt_h = k_t[h * M_DK:(h + 1) * M_DK, :]
            qk = jnp.dot(q_h, kt_h.astype(BF16), preferred_element_type=F32)
            s_w = (qk * w_intra).astype(BF16)
            v_ext = jnp.concatenate([v_ref[nb, :, h * M_DV:(h + 1) * M_DV].astype(BF16), ones], axis=1)
            state = s_scr[sh]
            inter = jnp.dot(q_h, state.astype(BF16), preferred_element_type=F32)
            intra = jnp.dot(s_w, v_ext, preferred_element_type=F32)
            num = w_inter * inter[:, :M_DV] + intra[:, :M_DV]
            qn = w_inter * inter[:, M_DV:] + intra[:, M_DV:]
            den = jnp.maximum(jnp.abs(qn), jnp.exp(-mt))
            hh = num / den
            hn = hh * lax.rsqrt(jnp.mean(hh * hh, axis=-1, keepdims=True) + EPS)
            hn = hn * gain_ref[:, h * M_DV:(h + 1) * M_DV]
            out = hn * _sigmoid(o_ref[nb, :, h * M_DV:(h + 1) * M_DV])
            out_ref[nb, :, h * M_DV:(h + 1) * M_DV] = out.astype(out_ref.dtype)
            b_last = b_rep[L - 1:L, :]
            m_new = mt[L - 1:L, :]
            g_prev = jnp.exp(b_last + m_prev - m_new)
            g_row = jnp.exp(b_last - b_row + i_row - m_new)
            kg_t = (kt_h * g_row).astype(BF16)
            s_scr[sh] = (jnp.concatenate([g_prev, g_prev], axis=1) * state
                         + jnp.dot(kg_t, v_ext, preferred_element_type=F32))
            m_scr[nb, h:h + 1, :] = m_new

    @pl.when(c == pl.num_programs(1) - 1)
    def _():
        for nb in range(MLSTM_SEQS):
            for h in range(M_HEADS):
                st = s_scr[nb * M_HEADS + h]
                c_ref[nb, h] = st[:, :M_DV]
                n_ref[nb, h] = st[:, M_DV:]
        m_ref[...] = m_scr[...]


def _mlstm_prompt(z, bias_row, gain_row, batch, seq):
    L = MLSTM_CHUNK
    S = MLSTM_SEQS
    z3 = z.reshape(batch, seq, PROJ_W)
    return pl.pallas_call(
        _mlstm_prompt_kernel,
        out_shape=(
            jax.ShapeDtypeStruct((batch, seq, M_HEADS * M_DV), BF16),
            jax.ShapeDtypeStruct((batch, M_HEADS, M_DK, M_DV), F32),
            jax.ShapeDtypeStruct((batch, M_HEADS, M_DK, M_DV), F32),
            jax.ShapeDtypeStruct((batch, 8, LANES), F32),
        ),
        grid=(batch // S, seq // L),
        in_specs=[
            pl.BlockSpec((S, L, 256), lambda b, c: (b, c, C_MQ // 256)),
            pl.BlockSpec((S, L, 256), lambda b, c: (b, c, C_MK // 256)),
            pl.BlockSpec((S, L, 512), lambda b, c: (b, c, C_MV // 512)),
            pl.BlockSpec((S, L, 512), lambda b, c: (b, c, C_MO // 512)),
            pl.BlockSpec((S, L, LANES), lambda b, c: (b, c, C_GATE // LANES)),
            pl.BlockSpec((1, LANES), lambda b, c: (0, 0)),
            pl.BlockSpec((1, M_HEADS * M_DV), lambda b, c: (0, 0)),
        ],
        out_specs=(
            pl.BlockSpec((S, L, M_HEADS * M_DV), lambda b, c: (b, c, 0)),
            pl.BlockSpec((S, M_HEADS, M_DK, M_DV), lambda b, c: (b, 0, 0, 0)),
            pl.BlockSpec((S, M_HEADS, M_DK, M_DV), lambda b, c: (b, 0, 0, 0)),
            pl.BlockSpec((S, 8, LANES), lambda b, c: (b, 0, 0)),
        ),
        scratch_shapes=[pltpu.VMEM((S * M_HEADS, M_DK, 2 * M_DV), F32), pltpu.VMEM((S, 8, LANES), F32)],
        compiler_params=_cparams(("arbitrary", "arbitrary")),
        name="mlstm_prompt",
    )(z3, z3, z3, z3, z3, bias_row, gain_row)


def _attn_prompt_kernel(relb_ref, sink_ref, q_ref, kp_ref, kc_ref, vp_ref, vc_ref, bucket_ref,
                        out_ref, bias_scr):
    B = ATT_BLOCK
    j = pl.program_id(1)

    @pl.when((pl.program_id(0) == 0) & (j == 0))
    def _():
        bucket = bucket_ref[...]
        for h in range(A_HEADS):
            acc = jnp.full((B, 2 * B), NEG, F32)
            for bk in range(NUM_BUCKETS):
                acc = jnp.where(bucket == bk, relb_ref[bk * A_HEADS + h], acc)
            bias_scr[h] = acc

    scale = A_HD ** -0.5
    s_iota = lax.broadcasted_iota(I32, (B, 2 * B), 1)
    first = jnp.where((s_iota < B) & (j == 0), NEG, 0.0)
    for sub in range(ATT_QBLOCKS):
        rows = slice(sub * B, (sub + 1) * B)
        prev_rows = slice((sub - 1) * B, sub * B)
        outs = []
        for h in range(A_HEADS):
            g = h // A_GROUP
            cols = slice(g * A_HD, (g + 1) * A_HD)
            q_h = (q_ref[rows, h * A_HD:(h + 1) * A_HD] * scale).astype(BF16)
            k_prev = kp_ref[:, cols] if sub == 0 else kc_ref[prev_rows, cols]
            v_prev = vp_ref[:, cols] if sub == 0 else vc_ref[prev_rows, cols]
            k2 = jnp.concatenate([k_prev, kc_ref[rows, cols]], axis=0).astype(BF16)
            v2 = jnp.concatenate([v_prev, vc_ref[rows, cols]], axis=0).astype(BF16)
            logits = lax.dot_general(q_h, k2, (((1,), (1,)), ((), ())), preferred_element_type=F32)
            logits = logits + bias_scr[h]
            if sub == 0:
                logits = logits + first
            sink = sink_ref[h]
            m = jnp.maximum(jnp.max(logits, axis=-1, keepdims=True), sink)
            p = jnp.exp(logits - m)
            den = jnp.sum(p, axis=-1, keepdims=True) + jnp.exp(sink - m)
            o = jnp.dot(p.astype(BF16), v2, preferred_element_type=F32) / den
            outs.append(o)
        out_ref[rows, :] = jnp.concatenate(outs, axis=1).astype(out_ref.dtype)


def _attn_prompt(z, rel_bias, sinks, batch, seq):
    B = ATT_BLOCK
    nb = seq // B
    qi = np.arange(B)[:, None]
    si = np.arange(2 * B)[None, :]
    dist = qi + B - si
    bucket = np.where((dist >= 0) & (dist <= WINDOW), _t5_bucket_np(dist), -1).astype(np.int32)
    Q = ATT_QBLOCKS
    ns = nb // Q
    cur = lambda b, j, *_: b * ns + j
    prev = lambda b, j, *_: b * nb + jnp.maximum(Q * j - 1, 0)
    grid_spec = pltpu.PrefetchScalarGridSpec(
        num_scalar_prefetch=2,
        grid=(batch, ns),
        in_specs=[
            pl.BlockSpec((Q * B, 512), lambda b, j, *_: (cur(b, j), C_AQ // 512)),
            pl.BlockSpec((B, LANES), lambda b, j, *_: (prev(b, j), C_AK // LANES)),
            pl.BlockSpec((Q * B, LANES), lambda b, j, *_: (cur(b, j), C_AK // LANES)),
            pl.BlockSpec((B, LANES), lambda b, j, *_: (prev(b, j), C_AV // LANES)),
            pl.BlockSpec((Q * B, LANES), lambda b, j, *_: (cur(b, j), C_AV // LANES)),
            pl.BlockSpec((B, 2 * B), lambda b, j, *_: (0, 0)),
        ],
        out_specs=pl.BlockSpec((Q * B, A_HEADS * A_HD), lambda b, j, *_: (cur(b, j), 0)),
        scratch_shapes=[pltpu.VMEM((A_HEADS, B, 2 * B), F32)],
    )
    return pl.pallas_call(
        _attn_prompt_kernel,
        out_shape=jax.ShapeDtypeStruct((batch * seq, A_HEADS * A_HD), BF16),
        grid_spec=grid_spec,
        compiler_params=_cparams(("arbitrary", "arbitrary")),
        name="attn_prompt",
    )(rel_bias.reshape(-1), sinks, z, z, z, z, z, jnp.asarray(bucket))


def _outproj_router_kernel(x_ref, mo_ref, ao_ref, wm_ref, wa_ref, g_ref, wr_ref, br_ref, cin_ref, *rest):
    y_ref, yn_ref, eidx_ref, gate_ref, rank_ref, cout_ref, carry = rest[-7:]
    T = x_ref.shape[0]
    i = pl.program_id(0)

    @pl.when(i == 0)
    def _():
        carry[...] = cin_ref[...]

    y = (x_ref[...] + jnp.dot(mo_ref[...], wm_ref[...], preferred_element_type=F32)
         + jnp.dot(ao_ref[...], wa_ref[...], preferred_element_type=F32))
    y_ref[...] = y
    yn = _rms(y, g_ref[...])
    for s in range(ROW_TILES):
        yn_ref[pl.ds(s, T, stride=ROW_TILES), :] = yn[:, s * LANES:(s + 1) * LANES]
    logits = jnp.dot(yn.astype(BF16), wr_ref[...], preferred_element_type=F32) + br_ref[...]
    lane = lax.broadcasted_iota(I32, (T, LANES), 1)
    lane_f = lane.astype(F32)
    vals, idxs, hots = [], [], []
    l = logits
    for _ in range(TOP_K):
        mx = jnp.max(l, axis=-1, keepdims=True)
        idx = jnp.min(jnp.where(l == mx, lane_f, float(LANES)), axis=-1, keepdims=True)
        hot = lane_f == idx
        l = jnp.where(hot, -jnp.inf, l)
        vals.append(mx)
        idxs.append(idx)
        hots.append(hot)
    es = [jnp.exp(v - vals[0]) for v in vals]
    tot = es[0] + es[1] + es[2] + es[3]
    sel = jnp.where(hots[0] | hots[1] | hots[2] | hots[3], 1.0, 0.0)
    row = lax.broadcasted_iota(I32, (T, T), 0)
    col = lax.broadcasted_iota(I32, (T, T), 1)
    strict = (col < row).astype(BF16)
    before = carry[...] + jnp.dot(strict, sel.astype(BF16), preferred_element_type=F32)
    per_row = LANES // TOP_K
    tok_in_row = lax.broadcasted_iota(I32, (T, LANES), 0) % per_row
    gate = jnp.zeros((T, LANES), F32)
    e_sp = jnp.zeros((T, LANES), F32)
    r_sp = jnp.zeros((T, LANES), F32)
    for k in range(TOP_K):
        r_k = jnp.sum(jnp.where(hots[k], before, 0.0), axis=-1, keepdims=True)
        gate = jnp.where(lane == k, es[k] / tot, gate)
        mine = lane == tok_in_row * TOP_K + k
        e_sp = jnp.where(mine, idxs[k], e_sp)
        r_sp = jnp.where(mine, r_k, r_sp)
    fold = (lax.broadcasted_iota(I32, (T // per_row, T), 1) // per_row
            == lax.broadcasted_iota(I32, (T // per_row, T), 0)).astype(F32)
    exact = dict(preferred_element_type=F32, precision=lax.Precision.HIGHEST)
    eidx_ref[...] = jnp.dot(fold, e_sp, **exact).astype(I32)
    rank_ref[...] = jnp.dot(fold, r_sp, **exact).astype(I32)
    gate_ref[...] = gate
    carry[...] = carry[...] + jnp.sum(sel, axis=0, keepdims=True)
    cout_ref[...] = carry[...]


def _outproj_router(x2, m_out, a_out, w_m, w_a, norm_row, w_r, b_r, counts_in, tile, yn_rows, yn_prev=None):
    n = x2.shape[0]
    tok = lambda i: (i, 0)
    fix = lambda i: (0, 0)
    assert yn_prev is None or (yn_rows - n) % tile == 0
    row0 = 0 if yn_prev is None else (yn_rows - n) // tile
    extra_specs = [] if yn_prev is None else [pl.BlockSpec(memory_space=pl.ANY)]
    extra_args = [] if yn_prev is None else [yn_prev]
    return pl.pallas_call(
        _outproj_router_kernel,
        out_shape=(
            jax.ShapeDtypeStruct((n, D_MODEL), F32),
            jax.ShapeDtypeStruct((yn_rows * ROW_TILES, LANES), F32),
            jax.ShapeDtypeStruct((n * TOP_K // LANES, LANES), I32),
            jax.ShapeDtypeStruct((n, LANES), F32),
            jax.ShapeDtypeStruct((n * TOP_K // LANES, LANES), I32),
            jax.ShapeDtypeStruct((1, LANES), F32),
        ),
        grid=(n // tile,),
        in_specs=[
            pl.BlockSpec((tile, D_MODEL), tok),
            pl.BlockSpec((tile, 512), tok),
            pl.BlockSpec((tile, 512), tok),
            pl.BlockSpec((512, D_MODEL), fix),
            pl.BlockSpec((512, D_MODEL), fix),
            pl.BlockSpec((1, D_MODEL), fix),
            pl.BlockSpec((D_MODEL, LANES), fix),
            pl.BlockSpec((1, LANES), fix),
            pl.BlockSpec((1, LANES), fix),
        ] + extra_specs,
        out_specs=(
            pl.BlockSpec((tile, D_MODEL), tok),
            pl.BlockSpec((tile * ROW_TILES, LANES), lambda i: (row0 + i, 0)),
            pl.BlockSpec((tile * TOP_K // LANES, LANES), tok),
            pl.BlockSpec((tile, LANES), tok),
            pl.BlockSpec((tile * TOP_K // LANES, LANES), tok),
            pl.BlockSpec((1, LANES), fix),
        ),
        scratch_shapes=[pltpu.VMEM((1, LANES), F32)],
        input_output_aliases={} if yn_prev is None else {9: 1},
        compiler_params=_cparams(("arbitrary",)),
        name="outproj_router",
    )(x2, m_out, a_out, w_m, w_a, norm_row, w_r, b_r, counts_in, *extra_args)


def _ffn_kernel(be_ref, nused_ref, valid_ref, nxt_ref, gp_ref, gs_ref, gn_ref, order_ref,
                yn_hbm, wg_hbm, bg_ref, wu_hbm, bu_ref, wd_hbm, bd_ref, out_ref,
                wbuf, wg_bf, wu_bf, wd_bf, h_scr, xbuf0, xbuf1, xbuf2, wsem, gsem):
    i = pl.program_id(0)
    slot = lax.rem(i, FFN_BUFS)
    R = FFN_ROWS
    w_hbm = (wg_hbm, wu_hbm, wd_hbm)
    w_bf = (wg_bf, wu_bf, wd_bf)
    xbuf = (xbuf0, xbuf1, xbuf2)

    def weight_copies(e):
        return [pltpu.make_async_copy(w_hbm[j].at[e], wbuf.at[j], wsem.at[j]) for j in range(3)]

    def gather_row(b, r, dst_slot):
        tok = order_ref[jnp.where(r < gn_ref[b], gp_ref[b], gs_ref[b]) + r]
        return pltpu.make_async_copy(yn_hbm.at[tok], xbuf[dst_slot].at[pl.ds(r * ROW_TILES, ROW_TILES)],
                                     gsem.at[dst_slot])

    def gather_rolled(b, dst_slot):
        def body(rb, carry):
            for u in range(DMA_UNROLL):
                gather_row(b, rb * DMA_UNROLL + u, dst_slot).start(priority=u % 2)
            return carry
        lax.fori_loop(0, R // DMA_UNROLL, body, 0)

    def wait_rows(s):
        pltpu.make_async_copy(xbuf[(s + 1) % FFN_BUFS], xbuf[s], gsem.at[s]).wait()

    def by_slot(fn):
        for s in range(FFN_BUFS):
            @pl.when(slot == s)
            def _():
                fn(s)

    @pl.when(i < nused_ref[0])
    def _():
        @pl.when((i == 0) | (be_ref[i] != be_ref[jnp.maximum(i - 1, 0)]))
        def _():
            @pl.when(i == 0)
            def _():
                for cp in weight_copies(be_ref[0]):
                    cp.start()

            for cp in weight_copies(be_ref[i]):
                cp.wait()

            for j in range(3):
                for r in range(0, D_MODEL, CAST_ROWS):
                    w_bf[j][r:r + CAST_ROWS, :] = wbuf[j, r:r + CAST_ROWS, :].astype(BF16)

            @pl.when(nxt_ref[i] >= 0)
            def _():
                for cp in weight_copies(nxt_ref[i]):
                    cp.start()

        @pl.when(i == 0)
        def _():
            for b in range(FFN_BUFS - 1):
                gather_rolled(b, b)

        by_slot(wait_rows)

        def ffn_pass(rows, slot):
            x = jnp.concatenate([xbuf[slot][pl.ds(s, rows, stride=ROW_TILES), :].astype(BF16)
                                 for s in range(ROW_TILES)], axis=1)
            ahead = i + (FFN_BUFS - 1)
            g_p, g_s, g_n = gp_ref[ahead], gs_ref[ahead], gn_ref[ahead]
            dst = (slot + FFN_BUFS - 1) % FFN_BUFS
            for r in range(R):
                tok = order_ref[jnp.where(r < g_n, g_p, g_s) + r]
                pltpu.make_async_copy(yn_hbm.at[tok], xbuf[dst].at[pl.ds(r * ROW_TILES, ROW_TILES)],
                                      gsem.at[dst]).start(priority=r % 2)
            for c in range(D_FF // FFN_COLS):
                cs = slice(c * FFN_COLS, (c + 1) * FFN_COLS)
                g = jnp.dot(x, wg_bf[:, cs], preferred_element_type=F32) + bg_ref[:, cs]
                u = jnp.dot(x, wu_bf[:, cs], preferred_element_type=F32) + bu_ref[:, cs]
                g = jnp.minimum(g, SWIGLU_LIMIT)
                u = jnp.clip(u, -SWIGLU_LIMIT, SWIGLU_LIMIT)
                h_scr[0:rows, cs] = ((u + 1.0) * (g * _sigmoid(SWIGLU_ALPHA * g))).astype(BF16)
            out = jnp.dot(h_scr[0:rows, :], wd_bf[...], preferred_element_type=F32) + bd_ref[...]
            for s in range(ROW_TILES):
                out_ref[pl.ds(s, rows, stride=ROW_TILES), :] = out[:, s * LANES:(s + 1) * LANES]

        for s in range(FFN_BUFS):
            for rows in range(FFN_SUB, R + 1, FFN_SUB):
                @pl.when((slot == s) & (valid_ref[i] > rows - FFN_SUB) & (valid_ref[i] <= rows))
                def _():
                    ffn_pass(rows, s)

        @pl.when(i == nused_ref[0] - 1)
        def _():
            def drain(s):
                for d in range(1, FFN_BUFS):
                    wait_rows((s + d) % FFN_BUFS)
            by_slot(drain)


def _ffn(plan, order, yn_tiles, wg, bg, wu, bu, wd, bd, n_blocks):
    blk = lambda i, be, nu, *_: (jnp.minimum(i, nu[0] - 1), 0)
    bsel = lambda i, be, *_: (be[i], 0, 0)
    hbm = pl.BlockSpec(memory_space=pl.ANY)
    grid_spec = pltpu.PrefetchScalarGridSpec(
        num_scalar_prefetch=8,
        grid=(n_blocks,),
        in_specs=[
            hbm,
            hbm,
            pl.BlockSpec((None, 1, D_FF), bsel),
            hbm,
            pl.BlockSpec((None, 1, D_FF), bsel),
            hbm,
            pl.BlockSpec((None, 1, D_MODEL), bsel),
        ],
        out_specs=pl.BlockSpec((FFN_ROWS * ROW_TILES, LANES), blk),
        scratch_shapes=[pltpu.VMEM((3, D_MODEL, D_FF), F32),
                        pltpu.VMEM((D_MODEL, D_FF), BF16), pltpu.VMEM((D_MODEL, D_FF), BF16),
                        pltpu.VMEM((D_FF, D_MODEL), BF16), pltpu.VMEM((FFN_ROWS, D_FF), BF16),
                        pltpu.VMEM((FFN_ROWS * ROW_TILES, LANES), F32),
                        pltpu.VMEM((FFN_ROWS * ROW_TILES, LANES), F32),
                        pltpu.VMEM((FFN_ROWS * ROW_TILES, LANES), F32),
                        pltpu.SemaphoreType.DMA((3,)), pltpu.SemaphoreType.DMA((FFN_BUFS,))],
    )
    return pl.pallas_call(
        _ffn_kernel,
        out_shape=jax.ShapeDtypeStruct((n_blocks * FFN_ROWS * ROW_TILES, LANES), F32),
        grid_spec=grid_spec,
        compiler_params=_cparams(("arbitrary",)),
        name="moe_ffn",
    )(*plan, order, yn_tiles, wg, bg, wu, bu, wd, bd)


def _combine_kernel(dest_ref, y_ref, gate_ref, fn_ref, ffn_ref, out_ref, buf0, buf1, sem):
    T = y_ref.shape[0]
    i = pl.program_id(0)
    n = pl.num_programs(0)
    slot = i % 2
    bufs = (buf0, buf1)

    def row_copy(base, t, k, s):
        d = dest_ref[base + t * TOP_K + k]
        return pltpu.make_async_copy(ffn_ref.at[d], bufs[s].at[k, pl.ds(t * ROW_TILES, ROW_TILES)], sem.at[s])

    @pl.when(i == 0)
    def _():
        def body(tb, carry):
            for u in range(DMA_UNROLL):
                for k in range(TOP_K):
                    row_copy(0, tb * DMA_UNROLL + u, k, 0).start(priority=k % 2)
            return carry
        lax.fori_loop(0, T // DMA_UNROLL, body, 0)

    def step(s, prefetch):
        for k in range(TOP_K):
            pltpu.make_async_copy(bufs[1 - s].at[k], bufs[s].at[k], sem.at[s]).wait()
        if prefetch:
            base = (i + 1) * (T * TOP_K)
            for t in range(T):
                for k in range(TOP_K):
                    row_copy(base, t, k, 1 - s).start(priority=k % 2)
        acc = y_ref[...]
        gate = gate_ref[...]
        for k in range(TOP_K):
            rows = jnp.concatenate([bufs[s][k, pl.ds(c, T, stride=ROW_TILES), :] for c in range(ROW_TILES)],
                                   axis=1)
            acc = acc + gate[:, k:k + 1] * rows
        out_ref[...] = _rms(acc, fn_ref[...])

    for s in range(2):
        @pl.when((slot == s) & (i + 1 < n))
        def _():
            step(s, True)

        @pl.when((slot == s) & (i + 1 >= n))
        def _():
            step(s, False)


def _combine(dest_flat, y, gate, fnorm_row, ffn_out, tile):
    n = y.shape[0]
    grid_spec = pltpu.PrefetchScalarGridSpec(
        num_scalar_prefetch=1,
        grid=(n // tile,),
        in_specs=[
            pl.BlockSpec((tile, D_MODEL), lambda i, *_: (i, 0)),
            pl.BlockSpec((tile, LANES), lambda i, *_: (i, 0)),
            pl.BlockSpec((1, D_MODEL), lambda i, *_: (0, 0)),
            pl.BlockSpec(memory_space=pl.ANY),
        ],
        out_specs=pl.BlockSpec((tile, D_MODEL), lambda i, *_: (i, 0)),
        scratch_shapes=[pltpu.VMEM((TOP_K, tile * ROW_TILES, LANES), F32),
                        pltpu.VMEM((TOP_K, tile * ROW_TILES, LANES), F32), pltpu.SemaphoreType.DMA((2,))],
    )
    return pl.pallas_call(
        _combine_kernel,
        out_shape=jax.ShapeDtypeStruct((n, D_MODEL), F32),
        grid_spec=grid_spec,
        compiler_params=_cparams(("arbitrary",)),
        name="moe_combine",
    )(dest_flat, y, gate, fnorm_row, ffn_out)


def _mlstm_step_kernel(q_ref, k_ref, v_ref, o_ref, gt_ref, bias_ref, gain_ref, c0_ref, n0_ref, m0_ref,
                       out_ref, c_ref, n_ref, m_ref):
    TB = SAMPLE_MLSTM_TB
    gb = gt_ref[...] + bias_ref[...]
    ls = _log_sigmoid(gb)
    lane = lax.broadcasted_iota(I32, (TB, LANES), 1)
    eye = (lax.broadcasted_iota(I32, (M_DK, M_DK), 0) == lax.broadcasted_iota(I32, (M_DK, M_DK), 1)).astype(F32)
    nt = (((1,), (1,)), ((), ()))
    m_all = jnp.zeros((TB, LANES), F32)
    for h in range(M_HEADS):
        i_pre = gb[:, h:h + 1]
        a = ls[:, M_HEADS + h:M_HEADS + h + 1] + m0_ref[:, h:h + 1]
        mt = jnp.maximum(a, i_pre)
        w_intra = jnp.exp(i_pre - mt)
        w_inter = jnp.exp(a - mt)
        q_h = q_ref[:, h * M_DK:(h + 1) * M_DK]
        k_h = k_ref[:, h * M_DK:(h + 1) * M_DK] * (M_DK ** -0.5)
        v_h = v_ref[:, h * M_DV:(h + 1) * M_DV]
        n0_h = n0_ref[:, h, :]
        s = jnp.sum(q_h * k_h, axis=-1, keepdims=True) * w_intra
        qn = w_inter * jnp.sum(q_h * n0_h, axis=-1, keepdims=True) + s
        den = jnp.maximum(jnp.abs(qn), jnp.exp(-mt))
        q_t = lax.dot_general(eye, q_h, nt, preferred_element_type=F32, precision=lax.Precision.HIGHEST)
        k_t = lax.dot_general(eye, k_h, nt, preferred_element_type=F32, precision=lax.Precision.HIGHEST)
        rows = []
        for b in range(TB):
            c0 = c0_ref[b, h]
            qc = jnp.sum(c0 * q_t[:, b:b + 1], axis=0, keepdims=True)
            v_b = v_h[b:b + 1, :]
            rows.append(w_inter[b:b + 1, :] * qc + s[b:b + 1, :] * v_b)
            c_ref[b, h] = w_inter[b:b + 1, :] * c0 + (w_intra[b:b + 1, :] * k_t[:, b:b + 1]) * v_b
        num = jnp.concatenate(rows, axis=0)
        hh = num / den
        hn = hh * lax.rsqrt(jnp.mean(hh * hh, axis=-1, keepdims=True) + EPS)
        hn = hn * gain_ref[:, h * M_DV:(h + 1) * M_DV]
        out_ref[:, h * M_DV:(h + 1) * M_DV] = (hn * _sigmoid(o_ref[:, h * M_DV:(h + 1) * M_DV])).astype(out_ref.dtype)
        n_ref[:, h * M_DK:(h + 1) * M_DK] = w_inter * n0_h + w_intra * k_h
        m_all = jnp.where(lane == h, mt, m_all)
    m_ref[...] = m_all


def _mlstm_step(zs, bias_row, gain_row, c0, n0, m0):
    TB = SAMPLE_MLSTM_TB
    nb = zs.shape[0]
    tok = lambda i: (i, 0)
    return pl.pallas_call(
        _mlstm_step_kernel,
        out_shape=(
            jax.ShapeDtypeStruct((nb, M_HEADS * M_DV), BF16),
            jax.ShapeDtypeStruct((nb, M_HEADS, M_DK, M_DV), F32),
            jax.ShapeDtypeStruct((nb, M_HEADS * M_DK), F32),
            jax.ShapeDtypeStruct((nb, LANES), F32),
        ),
        grid=(nb // TB,),
        in_specs=[
            pl.BlockSpec((TB, 256), lambda i: (i, C_MQ // 256)),
            pl.BlockSpec((TB, 256), lambda i: (i, C_MK // 256)),
            pl.BlockSpec((TB, 512), lambda i: (i, C_MV // 512)),
            pl.BlockSpec((TB, 512), lambda i: (i, C_MO // 512)),
            pl.BlockSpec((TB, LANES), lambda i: (i, C_GATE // LANES)),
            pl.BlockSpec((1, LANES), lambda i: (0, 0)),
            pl.BlockSpec((1, M_HEADS * M_DV), lambda i: (0, 0)),
            pl.BlockSpec((TB, M_HEADS, M_DK, M_DV), lambda i: (i, 0, 0, 0)),
            pl.BlockSpec((TB, M_HEADS, M_DK), lambda i: (i, 0, 0)),
            pl.BlockSpec((TB, M_HEADS), tok),
        ],
        out_specs=(
            pl.BlockSpec((TB, M_HEADS * M_DV), tok),
            pl.BlockSpec((TB, M_HEADS, M_DK, M_DV), lambda i: (i, 0, 0, 0)),
            pl.BlockSpec((TB, M_HEADS * M_DK), tok),
            pl.BlockSpec((TB, LANES), tok),
        ),
        compiler_params=_cparams(("arbitrary",)),
        name="mlstm_step",
    )(zs, zs, zs, zs, zs, bias_row, gain_row, c0, n0, m0)


def _attn_step_kernel(q_ref, kn_ref, vn_ref, ck_ref, cv_ref, bucket_ref, relt_ref, sink_ref,
                      out_ref, nk_ref, nv_ref, bias_scr):
    TB = SAMPLE_ATT_TB
    W = ck_ref.shape[1]
    NR = TB * A_HEADS
    row = lax.broadcasted_iota(I32, (NR, 1), 0)
    low = row % A_HEADS < A_GROUP

    @pl.when(pl.program_id(0) == 0)
    def _():
        bucket = jnp.broadcast_to(bucket_ref[...], (A_HEADS, W))
        acc = jnp.zeros((A_HEADS, W), F32)
        for bk in range(NUM_BUCKETS):
            acc = jnp.where(bucket == bk, relt_ref[:, bk:bk + 1], acc)
        tiled = jnp.concatenate([jnp.concatenate([acc] * TB, axis=1)] * TB, axis=0)
        own = (lax.broadcasted_iota(I32, (NR, TB * W), 0) // A_HEADS
               == lax.broadcasted_iota(I32, (NR, TB * W), 1) // W)
        bias_scr[...] = jnp.where(own, tiled, NEG)

    scale = A_HD ** -0.5
    nt = (((1,), (1,)), ((), ()))
    exact = dict(preferred_element_type=F32, precision=lax.Precision.HIGHEST)
    bias_new = jnp.concatenate([relt_ref[:, 0:1]] * TB, axis=0)
    sink = jnp.concatenate([sink_ref[...]] * TB, axis=0)
    rep = (lax.broadcasted_iota(I32, (NR, TB), 0) // A_HEADS == lax.broadcasted_iota(I32, (NR, TB), 1)).astype(F32)
    kn_rows = jnp.dot(rep, kn_ref[...], **exact)
    vn_rows = jnp.dot(rep, vn_ref[...], **exact)
    kn_h = jnp.where(low, kn_rows[:, :A_HD], kn_rows[:, A_HD:])
    vn_h = jnp.where(low, vn_rows[:, :A_HD], vn_rows[:, A_HD:])

    q = q_ref[...]
    qb = q.astype(BF16)
    kc = ck_ref[...].reshape(TB * W, A_KV * A_HD).astype(BF16)
    vc = cv_ref[...].reshape(TB * W, A_KV * A_HD).astype(BF16)
    l0 = lax.dot_general(qb, kc[:, :A_HD], nt, preferred_element_type=F32)
    l1 = lax.dot_general(qb, kc[:, A_HD:], nt, preferred_element_type=F32)
    logits = jnp.where(low, l0, l1) * scale + bias_scr[...]
    l_new = jnp.sum(q * kn_h, axis=-1, keepdims=True) * scale + bias_new
    m = jnp.maximum(jnp.maximum(jnp.max(logits, axis=-1, keepdims=True), l_new), sink)
    p = jnp.exp(logits - m)
    p_new = jnp.exp(l_new - m)
    den = jnp.sum(p, axis=-1, keepdims=True) + p_new + jnp.exp(sink - m)
    pb = p.astype(BF16)
    o0 = jnp.dot(pb, vc[:, :A_HD], preferred_element_type=F32)
    o1 = jnp.dot(pb, vc[:, A_HD:], preferred_element_type=F32)
    out_ref[...] = (jnp.where(low, o0, o1) + p_new * vn_h) / den

    nk_ref[:, 0:W - 1, :] = ck_ref[:, 1:W, :]
    nv_ref[:, 0:W - 1, :] = cv_ref[:, 1:W, :]
    for b in range(TB):
        nk_ref[b, W - 1:W, :] = kn_ref[b:b + 1, :]
        nv_ref[b, W - 1:W, :] = vn_ref[b:b + 1, :]


def _attn_step(q3, k_new, v_new, ck, cv, rel_bias, sinks):
    TB = SAMPLE_ATT_TB
    nb, W = ck.shape[0], ck.shape[1]
    bucket = _t5_bucket_np(W - np.arange(W))[None, :].astype(np.int32)
    tok = lambda i: (i, 0)
    tok3 = lambda i: (i, 0, 0)
    fix = lambda i: (0, 0)
    return pl.pallas_call(
        _attn_step_kernel,
        out_shape=(
            jax.ShapeDtypeStruct((nb * A_HEADS, A_HD), F32),
            jax.ShapeDtypeStruct(ck.shape, F32),
            jax.ShapeDtypeStruct(cv.shape, F32),
        ),
        grid=(nb // TB,),
        in_specs=[
            pl.BlockSpec((TB * A_HEADS, A_HD), tok),
            pl.BlockSpec((TB, LANES), lambda i: (i, C_AK // LANES)),
            pl.BlockSpec((TB, LANES), lambda i: (i, C_AV // LANES)),
            pl.BlockSpec((TB, W, A_KV * A_HD), tok3),
            pl.BlockSpec((TB, W, A_KV * A_HD), tok3),
            pl.BlockSpec((1, W), fix),
            pl.BlockSpec((A_HEADS, NUM_BUCKETS), fix),
            pl.BlockSpec((A_HEADS, 1), fix),
        ],
        out_specs=(
            pl.BlockSpec((TB * A_HEADS, A_HD), tok),
            pl.BlockSpec((TB, W, A_KV * A_HD), tok3),
            pl.BlockSpec((TB, W, A_KV * A_HD), tok3),
        ),
        scratch_shapes=[pltpu.VMEM((TB * A_HEADS, TB * W), F32)],
        compiler_params=_cparams(("arbitrary",)),
        name="attn_step",
    )(q3.reshape(nb * A_HEADS, A_HD), k_new, v_new, ck, cv, jnp.asarray(bucket), rel_bias.T,
      sinks.reshape(A_HEADS, 1))


def _reorder_w_in(w_in):
    o = 0
    parts = {}
    for name, width in (("mq", 256), ("mk", 256), ("mv", 512), ("mo", 512), ("mi", 4), ("mf", 4),
                        ("aq", 512), ("ak", 128), ("av", 128)):
        parts[name] = w_in[:, o:o + width]
        o += width
    pad = jnp.zeros((w_in.shape[0], LANES - 2 * M_HEADS), w_in.dtype)
    cols = [parts[n] for n in ("mq", "mk", "mv", "mo", "aq", "ak", "av", "mi", "mf")] + [pad]
    return jnp.concatenate(cols, axis=1).astype(BF16)


def _lane_row(v, fill=0.0):
    return jnp.concatenate([v.astype(F32), jnp.full((LANES - v.shape[0],), fill, F32)])[None, :]


def kernel(x_prompt, x_sample, state_C, state_n, state_m, cache_k, cache_v, rel_bias, norm1, w_in, b_if,
           m_gain, sinks, w_out, norm2, w_router, b_router, w_gate, b_gate, w_up, b_up, w_down, b_down,
           final_norm):
    assert norm1.shape[0] == 1, "single-layer trunk"
    batch, seq, _ = x_prompt.shape
    nsmp = x_sample.shape[0]
    n_p = batch * seq
    W = cache_k.shape[2]

    xp = x_prompt.reshape(n_p, D_MODEL)
    xs_ = x_sample.reshape(nsmp, D_MODEL)
    w_in_r = _reorder_w_in(w_in[0])
    n1 = norm1[0][None, :]
    n2 = norm2[0][None, :]
    fn = final_norm[None, :]
    bias_row = _lane_row(b_if[0])
    gain_row = m_gain[0][None, :]
    w_m = w_out[0][:M_HEADS * M_DV].astype(BF16)
    w_a = w_out[0][M_HEADS * M_DV:].astype(BF16)
    w_r32 = jnp.concatenate([w_router[0], jnp.zeros((D_MODEL, LANES - N_EXPERTS), F32)], axis=1)
    w_r = w_r32.astype(BF16)
    b_r = _lane_row(b_router[0], NEG)

    zp = _inproj(xp, n1, w_in_r, 2 * TOK_TILE)
    zs = _inproj(xs_, n1, w_in_r, nsmp)
    m_out_p, p_c, p_nrep, p_mrep = _mlstm_prompt(zp, bias_row, gain_row, batch, seq)
    m_out_p = m_out_p.reshape(n_p, M_HEADS * M_DV)
    a_out_p = _attn_prompt(zp, rel_bias, sinks[0], batch, seq)
    m_out_s, s_c, s_n, s_mrep = _mlstm_step(zs, bias_row, gain_row, state_C[0], state_n[0], state_m[0])
    q3 = zs[:, C_AQ:C_AQ + A_HEADS * A_HD].reshape(nsmp, A_HEADS, A_HD)
    a3, s_k, s_v = _attn_step(q3, zs, zs, cache_k[0].reshape(nsmp, W, A_KV * A_HD),
                              cache_v[0].reshape(nsmp, W, A_KV * A_HD), rel_bias, sinks[0])
    a_out_s = a3.reshape(nsmp, A_HEADS * A_HD).astype(BF16)

    n_tot = n_p + nsmp
    zero_counts = jnp.zeros((1, LANES), F32)
    y_p, yn_all, e_p, g_p, r_p, cnt_p = _outproj_router(xp, m_out_p, a_out_p, w_m, w_a, n2, w_r, b_r,
                                                        zero_counts, TOK_TILE, n_tot)
    y_s, yn_all, e_s, g_s, r_s, cnt = _outproj_router(xs_, m_out_s, a_out_s, w_m, w_a, n2, w_r, b_r,
                                                      cnt_p, nsmp, n_tot, yn_all)

    counts = cnt[0, :N_EXPERTS].astype(I32)
    counts_p = cnt_p[0, :N_EXPERTS].astype(I32)
    counts_s = counts - counts_p
    start_p = jnp.cumsum(counts_p) - counts_p
    start_s = jnp.cumsum(counts_s) - counts_s
    nblk = (counts + FFN_ROWS - 1) // FFN_ROWS
    blk_end = jnp.cumsum(nblk)
    ex = jnp.arange(N_EXPERTS, dtype=I32)

    def per_slot(e, r, table):
        hot = e[:, :, None] == ex
        return jnp.sum(jnp.where(hot, table, 0), axis=-1) + r

    order = jnp.concatenate([
        jnp.argsort(per_slot(e_p, r_p, start_p).reshape(-1)).astype(I32) // TOP_K,
        jnp.argsort(per_slot(e_s, r_s, start_s - counts_p).reshape(-1)).astype(I32) // TOP_K + n_p,
        jnp.zeros((2 * FFN_ROWS,), I32)])
    pad_start = (blk_end - nblk) * FFN_ROWS
    dest_p = per_slot(e_p, r_p, pad_start).reshape(-1).astype(I32)
    dest_s = per_slot(e_s, r_s, pad_start).reshape(-1).astype(I32)

    n_blocks = (n_tot * TOP_K + N_EXPERTS * (FFN_ROWS - 1) + FFN_ROWS - 1) // FFN_ROWS
    nused = jnp.maximum(blk_end[-1], 1).astype(I32)
    bi = jnp.minimum(jnp.arange(n_blocks + FFN_BUFS - 1, dtype=I32), nused - 1)
    block_e = jnp.minimum(jnp.sum((bi[:, None] >= blk_end[None, :]).astype(I32), axis=1), N_EXPERTS - 1)
    hot = block_e[:, None] == ex
    pick = lambda v: jnp.sum(jnp.where(hot, v, 0), axis=1)
    row0 = (bi - pick(blk_end - nblk)) * FFN_ROWS
    blk_valid = jnp.clip(pick(counts) - row0, 0, FFN_ROWS).astype(I32)
    real = jnp.arange(n_blocks + FFN_BUFS - 1, dtype=I32) < nused
    blk_gn = jnp.where(real, jnp.clip(pick(counts_p) - row0, 0, FFN_ROWS), 0).astype(I32)
    blk_gp = jnp.where(real, pick(start_p) + row0, 0).astype(I32)
    blk_gs = jnp.where(real, n_p * TOP_K + pick(start_s - counts_p) + row0, n_tot * TOP_K).astype(I32)
    later = jnp.where((ex[None, :] > ex[:, None]) & (counts[None, :] > 0), ex[None, :], N_EXPERTS)
    next_e = jnp.min(later, axis=1)
    blk_next = pick(jnp.where(next_e < N_EXPERTS, next_e, -1)).astype(I32)
    plan = (block_e[:n_blocks].astype(I32), nused.reshape(1), blk_valid[:n_blocks], blk_next[:n_blocks],
            blk_gp, blk_gs, blk_gn)

    as_tiles = lambda a: a.reshape(-1, ROW_TILES, LANES)
    ffn_out = _ffn(plan, order, as_tiles(yn_all), w_gate[0], b_gate[0][:, None, :], w_up[0], b_up[0][:, None, :],
                   w_down[0], b_down[0][:, None, :], n_blocks)
    out_p = _combine(dest_p, y_p, g_p, fn, as_tiles(ffn_out), COMBINE_TILE)
    out_s = _combine(dest_s, y_s, g_s, fn, as_tiles(ffn_out), min(COMBINE_TILE, nsmp))

    kv_shape = (1, batch, WINDOW, A_KV, A_HD)
    zp3 = zp.reshape(batch, seq, PROJ_W)
    zk = zp3[:, seq - WINDOW:, C_AK:C_AK + A_KV * A_HD]
    zv = zp3[:, seq - WINDOW:, C_AV:C_AV + A_KV * A_HD]
    return (
        out_p.reshape(batch, seq, D_MODEL),
        out_s.reshape(nsmp, 1, D_MODEL),
        p_c[None],
        p_nrep[None, :, :, :, 0],
        p_mrep[None, :, :M_HEADS, 0],
        zk.reshape(kv_shape),
        zv.reshape(kv_shape),
        s_c[None],
        s_n.reshape(1, nsmp, M_HEADS, M_DK),
        s_mrep[None, :, :M_HEADS],
        s_k.reshape(1, nsmp, W, A_KV, A_HD),
        s_v.reshape(1, nsmp, W, A_KV, A_HD),
    )
```
